```python
import math
import jax, jax.numpy as jnp
from jax import lax
import numpy as np

D_MODEL = 1024
BATCH = 8
SEQ = 2048
DEPTH = 2

N_MIXERS = 2
HEAD_DIM = 64
N_HEADS = D_MODEL // HEAD_DIM
DILATED_GROUPS = ((128, 1), (512, 4), (2048, 16))
N_GROUPS = len(DILATED_GROUPS)
QKV_WIDTH = N_GROUPS * 3 * N_HEADS * HEAD_DIM
N_BUCKETS = 32
MAX_DISTANCE = 1024
CONV_CHANNELS = D_MODEL
CONV_WIDTH = 31
D_FF = 4 * D_MODEL
RMS_EPS = 1e-6
LN_EPS = 1e-5
NEG_INF = -1e30
N_ATTN = (DEPTH + N_MIXERS - 1) // N_MIXERS
N_CONV = DEPTH // N_MIXERS

kernel_name = "hybrid_dilated_attn_conformer_conv_encoder"


def rmsnorm(x, g):
    xf = x.astype(jnp.float32)
    y = xf * lax.rsqrt(jnp.mean(xf * xf, axis=-1, keepdims=True) + RMS_EPS)
    return (y * g.astype(jnp.float32)).astype(x.dtype)


def layernorm(x, g, b):
    xf = x.astype(jnp.float32)
    mu = jnp.mean(xf, axis=-1, keepdims=True)
    var = jnp.mean(jnp.square(xf - mu), axis=-1, keepdims=True)
    y = (xf - mu) * lax.rsqrt(var + LN_EPS)
    return (y * g.astype(jnp.float32) + b.astype(jnp.float32)).astype(x.dtype)


def t5_bucket(rel):
    nb = N_BUCKETS // 2
    max_exact = nb // 2
    base = jnp.where(rel > 0, nb, 0)
    n = jnp.abs(rel)
    nf = jnp.maximum(n, 1).astype(jnp.float32)
    large = max_exact + (jnp.log(nf / max_exact) / math.log(MAX_DISTANCE / max_exact)
                         * (nb - max_exact)).astype(jnp.int32)
    large = jnp.minimum(large, nb - 1)
    return base + jnp.where(n < max_exact, n, large)


def dilated_window_attention(q, k, v, bias_table, window, dilation):
    B, S, H, Dh = q.shape
    r = dilation
    n_side = (window // 2) // r
    blk = n_side
    L = S // r
    nb = -(-L // blk)
    Lp = nb * blk

    def to_sub(t):
        t = t.reshape(B, L, r, H, Dh).transpose(0, 2, 1, 3, 4)
        return jnp.pad(t, ((0, 0), (0, 0), (0, Lp - L), (0, 0), (0, 0)))

    def band(t):
        t = jnp.pad(t, ((0, 0), (0, 0), (blk, blk), (0, 0), (0, 0))).reshape(B, r, nb + 2, blk, H, Dh)
        return jnp.concatenate([t[:, :, :-2], t[:, :, 1:-1], t[:, :, 2:]], axis=3)

    qb = to_sub(q).reshape(B, r, nb, blk, H, Dh)
    kb = band(to_sub(k))
    vb = band(to_sub(v))

    logits = jnp.einsum('brnqhd,brnkhd->brnhqk', qb, kb).astype(jnp.float32) * (Dh ** -0.5)
    qi = jnp.arange(blk)[:, None]
    kj = jnp.arange(3 * blk)[None, :] - blk
    delta = kj - qi
    bias = bias_table[t5_bucket(delta * r)].astype(jnp.float32).transpose(2, 0, 1)
    kpos = jnp.arange(nb)[:, None, None] * blk + kj[None]
    valid = (jnp.abs(delta) <= n_side)[None] & (kpos >= 0) & (kpos < L)
    logits = jnp.where(valid[:, None], logits + bias, NEG_INF)

    lse = jax.nn.logsumexp(logits, axis=-1)
    p = jnp.exp(logits - lse[..., None]).astype(v.dtype)
    out = jnp.einsum('brnhqk,brnkhd->brnqhd', p, vb)

    out = out.reshape(B, r, Lp, H, Dh)[:, :, :L].transpose(0, 2, 1, 3, 4).reshape(B, S, H, Dh)
    lse = lse.transpose(0, 1, 2, 4, 3).reshape(B, r, Lp, H)[:, :, :L]
    lse = lse.transpose(0, 2, 1, 3).reshape(B, S, H)
    return out, lse


def dilated_attention_mixer(h, w_qkv, w_o, rel_bias):
    B, S, _ = h.shape
    qkv = (h @ w_qkv).reshape(B, S, N_GROUPS, 3, N_HEADS, HEAD_DIM)
    outs, lses = [], []
    for g, (window, dilation) in enumerate(DILATED_GROUPS):
        o, lse = dilated_window_attention(qkv[:, :, g, 0], qkv[:, :, g, 1], qkv[:, :, g, 2],
                                          rel_bias[:, g * N_HEADS:(g + 1) * N_HEADS],
                                          window, dilation)
        outs.append(o)
        lses.append(lse)
    alpha = jax.nn.softmax(jnp.stack(lses), axis=0)
    o = jnp.einsum('gbsh,gbshd->bshd', alpha, jnp.stack(outs).astype(jnp.float32))
    return o.reshape(B, S, N_HEADS * HEAD_DIM).astype(h.dtype) @ w_o


def conformer_conv_mixer(h, w_pw1, b_pw1, w_dw, b_dw, ln_g, ln_b, w_pw2, b_pw2):
    u = h @ w_pw1 + b_pw1
    a, gate = jnp.split(u, 2, axis=-1)
    u = a * jax.nn.sigmoid(gate)
    u = lax.conv_general_dilated(u, w_dw[:, None, :], window_strides=(1,),
                                 padding=[(CONV_WIDTH // 2, CONV_WIDTH // 2)],
                                 dimension_numbers=('NWC', 'WIO', 'NWC'),
                                 feature_group_count=CONV_CHANNELS) + b_dw
    u = jax.nn.silu(layernorm(u, ln_g, ln_b))
    return u @ w_pw2 + b_pw2


def squared_relu_mlp(h, w_up, w_down):
    return jnp.square(jax.nn.relu(h @ w_up)) @ w_down


def setup_inputs(seed: int = 0) -> dict:
    key = jax.random.key(seed)
    ks = jax.random.split(key, 20)
    nrm = jax.random.normal
    f32 = jnp.float32

    def gain(k, shape):
        return 1.0 + 0.05 * nrm(k, shape, f32)

    return {
        "x": nrm(ks[0], (BATCH, SEQ, D_MODEL), f32),
        "rel_bias": 0.2 * nrm(ks[1], (N_BUCKETS, N_GROUPS * N_HEADS), f32),
        "norm_mix_pre": gain(ks[2], (DEPTH, D_MODEL)),
        "norm_mix_post": gain(ks[3], (DEPTH, D_MODEL)),
        "norm_mlp_pre": gain(ks[4], (DEPTH, D_MODEL)),
        "norm_mlp_post": gain(ks[5], (DEPTH, D_MODEL)),
        "attn_w_qkv": nrm(ks[6], (N_ATTN, D_MODEL, QKV_WIDTH), f32) * D_MODEL ** -0.5,
        "attn_w_o": nrm(ks[7], (N_ATTN, N_HEADS * HEAD_DIM, D_MODEL), f32) * (N_HEADS * HEAD_DIM) ** -0.5,
        "conv_w_pw1": nrm(ks[8], (N_CONV, D_MODEL, 2 * CONV_CHANNELS), f32) * D_MODEL ** -0.5,
        "conv_b_pw1": 0.02 * nrm(ks[9], (N_CONV, 2 * CONV_CHANNELS), f32),
        "conv_w_dw": nrm(ks[10], (N_CONV, CONV_WIDTH, CONV_CHANNELS), f32) * CONV_WIDTH ** -0.5,
        "conv_b_dw": 0.02 * nrm(ks[11], (N_CONV, CONV_CHANNELS), f32),
        "conv_ln_g": gain(ks[12], (N_CONV, CONV_CHANNELS)),
        "conv_ln_b": 0.02 * nrm(ks[13], (N_CONV, CONV_CHANNELS), f32),
        "conv_w_pw2": nrm(ks[14], (N_CONV, CONV_CHANNELS, D_MODEL), f32) * CONV_CHANNELS ** -0.5,
        "conv_b_pw2": 0.02 * nrm(ks[15], (N_CONV, D_MODEL), f32),
        "mlp_w_up": nrm(ks[16], (DEPTH, D_MODEL, D_FF), f32) * D_MODEL ** -0.5,
        "mlp_w_down": nrm(ks[17], (DEPTH, D_FF, D_MODEL), f32) * D_FF ** -0.5,
    }


def reference(x, rel_bias, norm_mix_pre, norm_mix_post, norm_mlp_pre, norm_mlp_post,
              attn_w_qkv, attn_w_o, conv_w_pw1, conv_b_pw1, conv_w_dw, conv_b_dw,
              conv_ln_g, conv_ln_b, conv_w_pw2, conv_b_pw2, mlp_w_up, mlp_w_down):
    for i in range(DEPTH):
        j = i // N_MIXERS
        h = rmsnorm(x, norm_mix_pre[i])
        if i % N_MIXERS == 0:
            m = dilated_attention_mixer(h, attn_w_qkv[j], attn_w_o[j], rel_bias)
        else:
            m = conformer_conv_mixer(h, conv_w_pw1[j], conv_b_pw1[j], conv_w_dw[j], conv_b_dw[j],
                                     conv_ln_g[j], conv_ln_b[j], conv_w_pw2[j], conv_b_pw2[j])
        x = x + rmsnorm(m, norm_mix_post[i])
        h = rmsnorm(x, norm_mlp_pre[i])
        x = x + rmsnorm(squared_relu_mlp(h, mlp_w_up[i], mlp_w_down[i]), norm_mlp_post[i])
    return x
```

```python
import functools
import math

import jax
import jax.numpy as jnp
from jax import lax
from jax.experimental import pallas as pl
from jax.experimental.pallas import tpu as pltpu

D_MODEL = 1024
BATCH = 8
SEQ = 2048
HEAD_DIM = 64
N_HEADS = 16
DILATIONS = (1, 4, 16)
N_SIDE = 64
N_GROUPS = 3
GROUP_WIDTH = 3 * N_HEADS * HEAD_DIM
N_BUCKETS = 32
MAX_DISTANCE = 1024
CONV_WIDTH = 31
D_FF = 4 * D_MODEL
RMS_EPS = 1e-6
LN_EPS = 1e-5
NEG_INF = -1e30

F32 = jnp.float32
BF16 = jnp.bfloat16

LANES = 128
Q_TILE = 128
K_TILE = Q_TILE + 2 * N_SIDE
VMEM_LIMIT = 56 * 1024 * 1024


def _rms(x, g):
    return x * lax.rsqrt(jnp.mean(x * x, axis=-1, keepdims=True) + RMS_EPS) * g


def _params(semantics):
    return pltpu.CompilerParams(dimension_semantics=semantics, vmem_limit_bytes=VMEM_LIMIT)


def _prenorm_kernel(x_ref, g_ref, o_ref):
    o_ref[...] = _rms(x_ref[...], g_ref[...]).astype(BF16)


def _prenorm(x2d, g):
    rows = 1024
    n = x2d.shape[0]
    return pl.pallas_call(
        _prenorm_kernel,
        grid=(n // rows,),
        in_specs=[pl.BlockSpec((rows, D_MODEL), lambda i: (i, 0)),
                  pl.BlockSpec((1, D_MODEL), lambda i: (0, 0))],
        out_specs=pl.BlockSpec((rows, D_MODEL), lambda i: (i, 0)),
        out_shape=jax.ShapeDtypeStruct((n, D_MODEL), BF16),
        compiler_params=_params(("arbitrary",)),
        name="prenorm",
    )(x2d, g.reshape(1, D_MODEL))


QKV_TN = 1024
QKV_TM = 512


def _qkv_kernel(h_ref, w_ref, o_ref, lhs_ref, *, r):
    L = SEQ // r
    if r == 1:
        lhs_ref = h_ref.at[0]
    else:
        @pl.when(pl.program_id(1) == 0)
        def _():
            for c in range(r):
                lhs_ref[c * L:(c + 1) * L, :] = h_ref[0, :, c * D_MODEL:(c + 1) * D_MODEL]

    for m in range(SEQ // QKV_TM):
        rows = slice(m * QKV_TM, (m + 1) * QKV_TM)
        o_ref[0, rows, :] = jnp.dot(lhs_ref[rows, :], w_ref[...],
                                    preferred_element_type=F32).astype(BF16)


def _qkv_proj(h, w_qkv, g):
    r = DILATIONS[g]
    L = SEQ // r
    nj = GROUP_WIDTH // QKV_TN
    return pl.pallas_call(
        functools.partial(_qkv_kernel, r=r),
        grid=(BATCH, nj),
        in_specs=[pl.BlockSpec((1, L, r * D_MODEL), lambda b, j: (b, 0, 0)),
                  pl.BlockSpec((D_MODEL, QKV_TN), lambda b, j: (0, g * nj + j))],
        out_specs=pl.BlockSpec((1, SEQ, QKV_TN), lambda b, j: (b, 0, j)),
        out_shape=jax.ShapeDtypeStruct((BATCH, SEQ, GROUP_WIDTH), BF16),
        scratch_shapes=[pltpu.VMEM((SEQ, D_MODEL), BF16)],
        compiler_params=_params(("arbitrary", "arbitrary")),
        name=f"qkv_proj_g{g}",
    )(h.reshape(BATCH, L, r * D_MODEL), w_qkv)


HEADS_PER_STEP = 4
PAIRS_PER_STEP = HEADS_PER_STEP // 2
STEP_LANES = HEADS_PER_STEP * HEAD_DIM
N_Q_TILES = SEQ // Q_TILE
N_KINDS = 3


def _bias_row(rel_ref, g, head):
    r = DILATIONS[g]
    nb = N_BUCKETS // 2
    max_exact = nb // 2
    delta = lax.broadcasted_iota(jnp.int32, (8, K_TILE), 1) - N_SIDE
    rel = delta * r
    n = jnp.abs(rel)
    nf = jnp.maximum(n, 1).astype(F32)
    large = max_exact + (jnp.log(nf / max_exact) / math.log(MAX_DISTANCE / max_exact)
                         * (nb - max_exact)).astype(jnp.int32)
    large = jnp.minimum(large, nb - 1)
    bucket = jnp.where(rel > 0, nb, 0) + jnp.where(n < max_exact, n, large)
    u = jnp.zeros((8, K_TILE), F32)
    for b in range(N_BUCKETS):
        u = jnp.where(bucket == b, rel_ref[b, g * N_HEADS + head], u)
    return jnp.where(jnp.abs(delta) <= N_SIDE, u, NEG_INF)


def _build_bias_tiles(rel_ref, bias_ref, head0):
    qi = lax.broadcasted_iota(jnp.int32, (Q_TILE, K_TILE), 0)
    kj = lax.broadcasted_iota(jnp.int32, (Q_TILE, K_TILE), 1)
    for g in range(N_GROUPS):
        for h in range(HEADS_PER_STEP):
            u = _bias_row(rel_ref, g, head0 + h)
            ub = jnp.broadcast_to(u[0:1, :], (Q_TILE, K_TILE))
            for kind in range(N_KINDS):
                shift = (N_SIDE * kind - N_SIDE) % K_TILE
                t = pltpu.roll(ub, shift, 1, stride=1, stride_axis=0)
                in_band = jnp.abs(kj - qi - N_SIDE * kind) <= N_SIDE
                bias_ref[g, h, kind] = jnp.where(in_band, t, NEG_INF)


def _attn_unit(q_ref, k_ref, v_ref, lanes, q0, k0, n_keys, bias0, bias1):
    q = q_ref[0, pl.ds(q0, Q_TILE), lanes]
    k = k_ref[0, pl.ds(k0, n_keys), lanes]
    v = v_ref[0, pl.ds(k0, n_keys), lanes]
    lane = lax.broadcasted_iota(jnp.int32, (1, LANES), 1)
    first = lane < HEAD_DIM
    scale = HEAD_DIM ** -0.5
    m0 = jnp.where(first, scale, 0.0).astype(BF16)
    m1 = jnp.where(first, 0.0, scale).astype(BF16)
    q2 = jnp.concatenate([q * m0, q * m1], axis=0)
    s = lax.dot_general(q2, k, (((1,), (1,)), ((), ())), preferred_element_type=F32)
    s = s + jnp.concatenate([bias0, bias1], axis=0)
    m = jnp.max(s, axis=-1, keepdims=True)
    p = jnp.exp(s - m)
    l = jnp.sum(p, axis=-1, keepdims=True)
    o2 = jnp.dot(p.astype(BF16), v, preferred_element_type=F32)
    o2 = o2 / l
    lse = m + jnp.log(l)
    o = jnp.where(first, o2[:Q_TILE], o2[Q_TILE:])
    lse_t = jnp.where(first, lse[:Q_TILE], lse[Q_TILE:])
    return o, lse_t


def _attn_kernel(rel_ref, q0_ref, k0_ref, v0_ref, q1_ref, k1_ref, v1_ref, q2_ref, k2_ref, v2_ref,
                 o_ref, bias_ref, og_ref, lse_ref):
    hp = pl.program_id(0)

    @pl.when(pl.program_id(1) == 0)
    def _():
        _build_bias_tiles(rel_ref, bias_ref, hp * HEADS_PER_STEP)

    qkv = ((q0_ref, k0_ref, v0_ref), (q1_ref, k1_ref, v1_ref), (q2_ref, k2_ref, v2_ref))
    for pair in range(PAIRS_PER_STEP):
        lanes = slice(pair * LANES, (pair + 1) * LANES)
        for g in range(N_GROUPS):
            r = DILATIONS[g]
            L = SEQ // r
            tiles_per_seq = L // Q_TILE
            q_ref, k_ref, v_ref = qkv[g]

            def tile_body(t, carry, g=g, r=r, tiles_per_seq=tiles_per_seq,
                          q_ref=q_ref, k_ref=k_ref, v_ref=v_ref, lanes=lanes, pair=pair):
                q0 = pl.multiple_of(t * Q_TILE, Q_TILE)
                if tiles_per_seq == 1:
                    k0, n_keys = q0, Q_TILE
                    bias0 = bias_ref[g, 2 * pair, 0, :, :Q_TILE]
                    bias1 = bias_ref[g, 2 * pair + 1, 0, :, :Q_TILE]
                else:
                    tl = t % tiles_per_seq
                    kind = jnp.where(tl == 0, 0, jnp.where(tl == tiles_per_seq - 1, 2, 1))
                    k0, n_keys = pl.multiple_of(q0 - N_SIDE * kind, N_SIDE), K_TILE
                    bias0 = bias_ref[g, 2 * pair, kind]
                    bias1 = bias_ref[g, 2 * pair + 1, kind]
                o, lse = _attn_unit(q_ref, k_ref, v_ref, lanes, q0, k0, n_keys, bias0, bias1)
                c = t // tiles_per_seq
                l0 = (t % tiles_per_seq) * Q_TILE
                if r == 1:
                    rows = pl.ds(q0, Q_TILE)
                else:
                    rows = pl.ds(l0 * r + c, Q_TILE, stride=r)
                og_ref[g, rows, :] = o
                lse_ref[g, rows, :] = lse
                return carry

            lax.fori_loop(0, N_Q_TILES, tile_body, 0)

        def merge_body(t, carry, lanes=lanes):
            rows = pl.ds(pl.multiple_of(t * Q_TILE, Q_TILE), Q_TILE)
            lse = [lse_ref[g, rows, :] for g in range(N_GROUPS)]
            top = jnp.maximum(jnp.maximum(lse[0], lse[1]), lse[2])
            w = [jnp.exp(x - top) for x in lse]
            den = w[0] + w[1] + w[2]
            num = w[0] * og_ref[0, rows, :] + w[1] * og_ref[1, rows, :] + w[2] * og_ref[2, rows, :]
            o_ref[0, rows, lanes] = (num / den).astype(BF16)
            return carry

        lax.fori_loop(0, N_Q_TILES, merge_body, 0)


def _attention(qkv, rel_bias):
    n_hp = N_HEADS // HEADS_PER_STEP
    blk = (1, SEQ, STEP_LANES)
    specs = [pl.BlockSpec(memory_space=pltpu.SMEM)]
    args = [rel_bias]
    for g in range(N_GROUPS):
        for part in range(3):
            specs.append(pl.BlockSpec(blk, lambda hp, b, part=part: (b, 0, part * n_hp + hp)))
            args.append(qkv[g])
    return pl.pallas_call(
        _attn_kernel,
        grid=(n_hp, BATCH),
        in_specs=specs,
        out_specs=pl.BlockSpec(blk, lambda hp, b: (b, 0, hp)),
        out_shape=jax.ShapeDtypeStruct((BATCH, SEQ, D_MODEL), BF16),
        scratch_shapes=[pltpu.VMEM((N_GROUPS, HEADS_PER_STEP, N_KINDS, Q_TILE, K_TILE), F32),
                        pltpu.VMEM((N_GROUPS, SEQ, LANES), F32),
                        pltpu.VMEM((N_GROUPS, SEQ, LANES), F32)],
        compiler_params=_params(("arbitrary", "arbitrary")),
        name="dilated_attention",
    )(*args)


PROJ_TM = 1024


def _oproj_kernel(a_ref, w_ref, g_ref, x_ref, o_ref):
    m = jnp.dot(a_ref[...], w_ref[...], preferred_element_type=F32)
    o_ref[...] = x_ref[...] + _rms(m, g_ref[...])


def _out_proj(a2d, w_o, g_post, x2d):
    n = x2d.shape[0]
    return pl.pallas_call(
        _oproj_kernel,
        grid=(n // PROJ_TM,),
        in_specs=[pl.BlockSpec((PROJ_TM, D_MODEL), lambda i: (i, 0)),
                  pl.BlockSpec((D_MODEL, D_MODEL), lambda i: (0, 0)),
                  pl.BlockSpec((1, D_MODEL), lambda i: (0, 0)),
                  pl.BlockSpec((PROJ_TM, D_MODEL), lambda i: (i, 0))],
        out_specs=pl.BlockSpec((PROJ_TM, D_MODEL), lambda i: (i, 0)),
        out_shape=jax.ShapeDtypeStruct((n, D_MODEL), F32),
        compiler_params=_params(("arbitrary",)),
        name="attn_out_proj",
    )(a2d, w_o, g_post.reshape(1, D_MODEL), x2d)


MLP_TM = 512
MLP_FC = 1024


def _mlp_kernel(x_ref, gpre_ref, gpost_ref, wup_ref, wdn_ref, o_ref):
    x = x_ref[...]
    h = _rms(x, gpre_ref[...]).astype(BF16)
    acc = jnp.zeros((MLP_TM, D_MODEL), F32)
    for c in range(D_FF // MLP_FC):
        cols = slice(c * MLP_FC, (c + 1) * MLP_FC)
        u = jnp.dot(h, wup_ref[:, cols], preferred_element_type=F32)
        u = jnp.square(jnp.maximum(u, 0.0)).astype(BF16)
        acc = acc + jnp.dot(u, wdn_ref[cols, :], preferred_element_type=F32)
    o_ref[...] = x + _rms(acc, gpost_ref[...])


def _mlp(x2d, g_pre, g_post, w_up, w_down):
    n = x2d.shape[0]
    return pl.pallas_call(
        _mlp_kernel,
        grid=(n // MLP_TM,),
        in_specs=[pl.BlockSpec((MLP_TM, D_MODEL), lambda i: (i, 0)),
                  pl.BlockSpec((1, D_MODEL), lambda i: (0, 0)),
                  pl.BlockSpec((1, D_MODEL), lambda i: (0, 0)),
                  pl.BlockSpec((D_MODEL, D_FF), lambda i: (0, 0)),
                  pl.BlockSpec((D_FF, D_MODEL), lambda i: (0, 0))],
        out_specs=pl.BlockSpec((MLP_TM, D_MODEL), lambda i: (i, 0)),
        out_shape=jax.ShapeDtypeStruct((n, D_MODEL), F32),
        compiler_params=_params(("arbitrary",)),
        name="mlp",
    )(x2d, g_pre.reshape(1, D_MODEL), g_post.reshape(1, D_MODEL), w_up, w_down)


CONV_TS = 512
HALO = 16
CONV_ROWS = CONV_TS + 2 * HALO
CONV_RB = 64


def _conv_kernel(x_ref, gpre_ref, w1_ref, b1_ref, wdw_ref, bdw_ref, lng_ref, lnb_ref,
                 w2_ref, b2_ref, gpost_ref, o_ref, h_ref, u_ref, c_ref):
    j = pl.program_id(1)
    nj = pl.num_programs(1)
    s0 = pl.multiple_of(j * CONV_TS, CONV_TS)
    top = pl.multiple_of(jnp.maximum(s0 - HALO, 0), HALO)
    bot = pl.multiple_of(jnp.minimum(s0 + CONV_TS, SEQ - HALO), HALO)
    gpre = gpre_ref[...]

    h_ref[0:HALO, :] = _rms(x_ref[0, pl.ds(top, HALO), :], gpre).astype(BF16)
    h_ref[HALO:HALO + CONV_TS, :] = _rms(x_ref[0, pl.ds(s0, CONV_TS), :], gpre).astype(BF16)
    h_ref[HALO + CONV_TS:, :] = _rms(x_ref[0, pl.ds(bot, HALO), :], gpre).astype(BF16)

    h = h_ref[...]
    a = jnp.dot(h, w1_ref[:, :D_MODEL], preferred_element_type=F32) + b1_ref[:, :D_MODEL]
    gate = jnp.dot(h, w1_ref[:, D_MODEL:], preferred_element_type=F32) + b1_ref[:, D_MODEL:]
    u_ref[...] = a * jax.nn.sigmoid(gate)

    @pl.when(j == 0)
    def _():
        u_ref[0:HALO, :] = jnp.zeros((HALO, D_MODEL), F32)

    @pl.when(j == nj - 1)
    def _():
        u_ref[HALO + CONV_TS:, :] = jnp.zeros((HALO, D_MODEL), F32)

    def conv_block(rb, carry):
        r0 = pl.multiple_of(rb * CONV_RB, CONV_RB)
        for lc in range(D_MODEL // LANES):
            lanes = slice(lc * LANES, (lc + 1) * LANES)
            win = u_ref[pl.ds(r0, CONV_RB + 2 * HALO), lanes]
            acc = jnp.zeros((CONV_RB, LANES), F32)
            for t in range(CONV_WIDTH):
                off = HALO - CONV_WIDTH // 2 + t
                acc = acc + win[off:off + CONV_RB, :] * wdw_ref[t:t + 1, lanes]
            c_ref[pl.ds(r0, CONV_RB), lanes] = acc + bdw_ref[:, lanes]
        return carry

    lax.fori_loop(0, CONV_TS // CONV_RB, conv_block, 0)

    v = c_ref[...]
    mu = jnp.mean(v, axis=-1, keepdims=True)
    var = jnp.mean(jnp.square(v - mu), axis=-1, keepdims=True)
    y = (v - mu) * lax.rsqrt(var + LN_EPS) * lng_ref[...] + lnb_ref[...]
    y = y * jax.nn.sigmoid(y)
    z = jnp.dot(y.astype(BF16), w2_ref[...], preferred_element_type=F32) + b2_ref[...]
    o_ref[0] = x_ref[0, pl.ds(s0, CONV_TS), :] + _rms(z, gpost_ref[...])


def _conv_layer(x, g_pre, w1, b1, wdw, bdw, lng, lnb, w2, b2, g_post):
    row = lambda v: v.reshape(1, -1)
    const = lambda shape: pl.BlockSpec(shape, lambda b, j: (0, 0))
    return pl.pallas_call(
        _conv_kernel,
        grid=(BATCH, SEQ // CONV_TS),
        in_specs=[pl.BlockSpec((1, SEQ, D_MODEL), lambda b, j: (b, 0, 0)),
                  const((1, D_MODEL)),
                  const((D_MODEL, 2 * D_MODEL)),
                  const((1, 2 * D_MODEL)),
                  const((CONV_WIDTH, D_MODEL)),
                  const((1, D_MODEL)),
                  const((1, D_MODEL)),
                  const((1, D_MODEL)),
                  const((D_MODEL, D_MODEL)),
                  const((1, D_MODEL)),
                  const((1, D_MODEL))],
        out_specs=pl.BlockSpec((1, CONV_TS, D_MODEL), lambda b, j: (b, j, 0)),
        out_shape=jax.ShapeDtypeStruct((BATCH, SEQ, D_MODEL), F32),
        scratch_shapes=[pltpu.VMEM((CONV_ROWS, D_MODEL), BF16),
                        pltpu.VMEM((CONV_ROWS, D_MODEL), F32),
                        pltpu.VMEM((CONV_TS, D_MODEL), F32)],
        compiler_params=_params(("arbitrary", "arbitrary")),
        name="conformer_conv",
    )(x, row(g_pre), w1, row(b1), wdw, row(bdw), row(lng), row(lnb), w2, row(b2), row(g_post))


def kernel(x, rel_bias, norm_mix_pre, norm_mix_post, norm_mlp_pre, norm_mlp_post, attn_w_qkv,
           attn_w_o, conv_w_pw1, conv_b_pw1, conv_w_dw, conv_b_dw, conv_ln_g, conv_ln_b,
           conv_w_pw2, conv_b_pw2, mlp_w_up, mlp_w_down):
    n_tok = BATCH * SEQ
    x2d = x.reshape(n_tok, D_MODEL)

    h = _prenorm(x2d, norm_mix_pre[0])
    w_qkv = attn_w_qkv[0].astype(BF16)
    qkv = [_qkv_proj(h, w_qkv, g) for g in range(N_GROUPS)]
    a = _attention(qkv, rel_bias)
    x2d = _out_proj(a.reshape(n_tok, D_MODEL), attn_w_o[0].astype(BF16), norm_mix_post[0], x2d)
    x2d = _mlp(x2d, norm_mlp_pre[0], norm_mlp_post[0],
               mlp_w_up[0].astype(BF16), mlp_w_down[0].astype(BF16))

    x3d = _conv_layer(x2d.reshape(BATCH, SEQ, D_MODEL), norm_mix_pre[1],
                      conv_w_pw1[0].astype(BF16), conv_b_pw1[0], conv_w_dw[0], conv_b_dw[0],
                      conv_ln_g[0], conv_ln_b[0], conv_w_pw2[0].astype(BF16), conv_b_pw2[0],
                      norm_mix_post[1])
    x2d = _mlp(x3d.reshape(n_tok, D_MODEL), norm_mlp_pre[1], norm_mlp_post[1],
               mlp_w_up[1].astype(BF16), mlp_w_down[1].astype(BF16))
    return x2d.reshape(BATCH, SEQ, D_MODEL)
```

```python
import functools
import math

import jax
import jax.numpy as jnp
from jax import lax
from jax.experimental import pallas as pl
from jax.experimental.pallas import tpu as pltpu

D_MODEL = 1024
BATCH = 8
SEQ = 2048
HEAD_DIM = 64
N_HEADS = 16
DILATIONS = (1, 4, 16)
N_SIDE = 64
N_GROUPS = 3
GROUP_WIDTH = 3 * N_HEADS * HEAD_DIM
N_BUCKETS = 32
MAX_DISTANCE = 1024
CONV_WIDTH = 31
D_FF = 4 * D_MODEL
RMS_EPS = 1e-6
LN_EPS = 1e-5
NEG_INF = -1e30

F32 = jnp.float32
BF16 = jnp.bfloat16

LANES = 128
Q_TILE = 128
K_TILE = Q_TILE + 2 * N_SIDE
VMEM_LIMIT = 56 * 1024 * 1024


def _rms(x, g):
    return x * lax.rsqrt(jnp.mean(x * x, axis=-1, keepdims=True) + RMS_EPS) * g


def _params(semantics):
    return pltpu.CompilerParams(dimension_semantics=semantics, vmem_limit_bytes=VMEM_LIMIT)


NORM_ROWS = 256


def _prenorm_kernel(x_ref, g_ref, *refs):
    out_refs, slab_ref = refs[:-1], refs[-1]

    def chunk(i, carry):
        rows = pl.ds(pl.multiple_of(i * NORM_ROWS, NORM_ROWS), NORM_ROWS)
        hn = _rms(x_ref[0, rows, :], g_ref[...])
        for k in range(D_MODEL // LANES):
            slab_ref[k, rows, :] = hn[:, k * LANES:(k + 1) * LANES]
        for out_ref, r in zip(out_refs, DILATIONS):
            if r == 1:
                out_ref[0, rows, :] = hn.astype(BF16)
        return carry

    lax.fori_loop(0, SEQ // NORM_ROWS, chunk, 0)
    for out_ref, r in zip(out_refs, DILATIONS):
        if r == 1:
            continue
        L = SEQ // r
        for c in range(r):
            for k in range(D_MODEL // LANES):
                out_ref[0, c * L:(c + 1) * L, k * LANES:(k + 1) * LANES] = (
                    slab_ref[k, pl.ds(c, L, stride=r), :].astype(BF16))


def _prenorm(x, g):
    blk = pl.BlockSpec((1, SEQ, D_MODEL), lambda b: (b, 0, 0))
    return pl.pallas_call(
        _prenorm_kernel,
        grid=(BATCH,),
        in_specs=[blk, pl.BlockSpec((1, D_MODEL), lambda b: (0, 0))],
        out_specs=[blk] * N_GROUPS,
        out_shape=[jax.ShapeDtypeStruct((BATCH, SEQ, D_MODEL), BF16)] * N_GROUPS,
        scratch_shapes=[pltpu.VMEM((D_MODEL // LANES, SEQ, LANES), F32)],
        compiler_params=_params(("arbitrary",)),
        name="prenorm",
    )(x, g.reshape(1, D_MODEL))


QKV_TN = 1024
QKV_TM = 512


def _qkv_kernel(h_ref, w_ref, o_ref):
    for m in range(SEQ // QKV_TM):
        rows = slice(m * QKV_TM, (m + 1) * QKV_TM)
        o_ref[0, rows, :] = jnp.dot(h_ref[0, rows, :], w_ref[...],
                                    preferred_element_type=F32).astype(BF16)


def _qkv_proj(h, w_qkv, g):
    nj = GROUP_WIDTH // QKV_TN
    return pl.pallas_call(
        _qkv_kernel,
        grid=(BATCH, nj),
        in_specs=[pl.BlockSpec((1, SEQ, D_MODEL), lambda b, j: (b, 0, 0)),
                  pl.BlockSpec((D_MODEL, QKV_TN), lambda b, j: (0, g * nj + j))],
        out_specs=pl.BlockSpec((1, SEQ, QKV_TN), lambda b, j: (b, 0, j)),
        out_shape=jax.ShapeDtypeStruct((BATCH, SEQ, GROUP_WIDTH), BF16),
        compiler_params=_params(("arbitrary", "arbitrary")),
        name=f"qkv_proj_g{g}",
    )(h, w_qkv)


HEADS_PER_STEP = 4
PAIRS_PER_STEP = HEADS_PER_STEP // 2
STEP_LANES = HEADS_PER_STEP * HEAD_DIM
N_Q_TILES = SEQ // Q_TILE
N_KINDS = 3
PADDED_PITCH = 24
SCRATCH_ROWS = SEQ // 16 * PADDED_PITCH
TILE_UNROLL = 8


def _bias_row(rel_ref, g, head):
    r = DILATIONS[g]
    nb = N_BUCKETS // 2
    max_exact = nb // 2
    delta = lax.broadcasted_iota(jnp.int32, (8, K_TILE), 1) - N_SIDE
    rel = delta * r
    n = jnp.abs(rel)
    nf = jnp.maximum(n, 1).astype(F32)
    large = max_exact + (jnp.log(nf / max_exact) / math.log(MAX_DISTANCE / max_exact)
                         * (nb - max_exact)).astype(jnp.int32)
    large = jnp.minimum(large, nb - 1)
    bucket = jnp.where(rel > 0, nb, 0) + jnp.where(n < max_exact, n, large)
    u = jnp.zeros((8, K_TILE), F32)
    for b in range(N_BUCKETS):
        u = jnp.where(bucket == b, rel_ref[b, g * N_HEADS + head], u)
    return jnp.where(jnp.abs(delta) <= N_SIDE, u, NEG_INF)


def _build_bias_tiles(rel_ref, bias_ref, head0):
    qi = lax.broadcasted_iota(jnp.int32, (Q_TILE, K_TILE), 0)
    kj = lax.broadcasted_iota(jnp.int32, (Q_TILE, K_TILE), 1)
    for g in range(N_GROUPS):
        for h in range(HEADS_PER_STEP):
            u = _bias_row(rel_ref, g, head0 + h)
            ub = jnp.broadcast_to(u[0:1, :], (Q_TILE, K_TILE))
            for kind in range(N_KINDS):
                shift = (N_SIDE * kind - N_SIDE) % K_TILE
                t = pltpu.roll(ub, shift, 1, stride=1, stride_axis=0)
                in_band = jnp.abs(kj - qi - N_SIDE * kind) <= N_SIDE
                bias_ref[g, h, kind] = jnp.where(in_band, t, NEG_INF)


def _token_rows(r, c, l0):
    if r == 1:
        return pl.ds(l0, Q_TILE)
    pitch = PADDED_PITCH if r % 16 == 0 else r
    return pl.ds(l0 * pitch + c, Q_TILE, stride=pitch)


def _load_tokens(ref, g, t):
    r = DILATIONS[g]
    if r % 16:
        return ref[g, pl.ds(pl.multiple_of(t * Q_TILE, Q_TILE), Q_TILE), :]
    n = Q_TILE // r
    base = pl.multiple_of(t * n * PADDED_PITCH, 8)
    return jnp.concatenate([ref[g, pl.ds(base + i * PADDED_PITCH, r), :] for i in range(n)], axis=0)


def _attn_unit(q_ref, k_ref, v_ref, lanes, q0, k0, n_keys, bias0, bias1):
    q = q_ref[0, pl.ds(q0, Q_TILE), lanes]
    k = k_ref[0, pl.ds(k0, n_keys), lanes]
    v = v_ref[0, pl.ds(k0, n_keys), lanes]
    lane = lax.broadcasted_iota(jnp.int32, (1, LANES), 1)
    first = lane < HEAD_DIM
    scale = HEAD_DIM ** -0.5
    m0 = jnp.where(first, scale, 0.0).astype(BF16)
    m1 = jnp.where(first, 0.0, scale).astype(BF16)
    q2 = jnp.concatenate([q * m0, q * m1], axis=0)
    s = lax.dot_general(q2, k, (((1,), (1,)), ((), ())), preferred_element_type=F32)
    s = s + jnp.concatenate([bias0, bias1], axis=0)
    m = jnp.max(s, axis=-1, keepdims=True)
    p = jnp.exp(s - m).astype(BF16)
    v1 = jnp.concatenate([v, jnp.ones((n_keys, LANES), BF16)], axis=1)
    o2 = jnp.dot(p, v1, preferred_element_type=F32)
    acc = jnp.where(first, o2[:Q_TILE, :LANES], o2[Q_TILE:, :LANES])
    l = jnp.where(first, o2[:Q_TILE, LANES:], o2[Q_TILE:, LANES:])
    m_t = jnp.where(first, m[:Q_TILE], m[Q_TILE:])
    return acc, m_t, l


def _attn_kernel(rel_ref, q0_ref, k0_ref, v0_ref, q1_ref, k1_ref, v1_ref, q2_ref, k2_ref, v2_ref,
                 o_ref, bias_ref, acc_ref, m_ref, l_ref):
    hp = pl.program_id(0)

    @pl.when(pl.program_id(1) == 0)
    def _():
        _build_bias_tiles(rel_ref, bias_ref, hp * HEADS_PER_STEP)

    qkv = ((q0_ref, k0_ref, v0_ref), (q1_ref, k1_ref, v1_ref), (q2_ref, k2_ref, v2_ref))
    for pair in range(PAIRS_PER_STEP):
        lanes = slice(pair * LANES, (pair + 1) * LANES)
        for g in range(N_GROUPS):
            r = DILATIONS[g]
            L = SEQ // r
            tiles_per_seq = L // Q_TILE
            q_ref, k_ref, v_ref = qkv[g]

            def tile_body(t, carry, g=g, r=r, tiles_per_seq=tiles_per_seq,
                          q_ref=q_ref, k_ref=k_ref, v_ref=v_ref, lanes=lanes, pair=pair):
                q0 = pl.multiple_of(t * Q_TILE, Q_TILE)
                if tiles_per_seq == 1:
                    k0, n_keys = q0, Q_TILE
                    bias0 = bias_ref[g, 2 * pair, 0, :, :Q_TILE]
                    bias1 = bias_ref[g, 2 * pair + 1, 0, :, :Q_TILE]
                else:
                    tl = t % tiles_per_seq
                    kind = jnp.where(tl == 0, 0, jnp.where(tl == tiles_per_seq - 1, 2, 1))
                    k0, n_keys = pl.multiple_of(q0 - N_SIDE * kind, N_SIDE), K_TILE
                    bias0 = bias_ref[g, 2 * pair, kind]
                    bias1 = bias_ref[g, 2 * pair + 1, kind]
                acc, m, l = _attn_unit(q_ref, k_ref, v_ref, lanes, q0, k0, n_keys, bias0, bias1)
                rows = _token_rows(r, t // tiles_per_seq, (t % tiles_per_seq) * Q_TILE)
                acc_ref[g, rows, :] = acc
                m_ref[g, rows, :] = m
                l_ref[g, rows, :] = l
                return carry

            lax.fori_loop(0, N_Q_TILES, tile_body, 0, unroll=TILE_UNROLL)

        def merge_body(t, carry, lanes=lanes):
            m = [_load_tokens(m_ref, g, t) for g in range(N_GROUPS)]
            top = jnp.maximum(jnp.maximum(m[0], m[1]), m[2])
            w = [jnp.exp(x - top) for x in m]
            num = sum(w[g] * _load_tokens(acc_ref, g, t) for g in range(N_GROUPS))
            den = sum(w[g] * _load_tokens(l_ref, g, t) for g in range(N_GROUPS))
            rows = pl.ds(pl.multiple_of(t * Q_TILE, Q_TILE), Q_TILE)
            o_ref[0, rows, lanes] = (num / den).astype(BF16)
            return carry

        lax.fori_loop(0, N_Q_TILES, merge_body, 0)


def _attention(qkv, rel_bias):
    n_hp = N_HEADS // HEADS_PER_STEP
    blk = (1, SEQ, STEP_LANES)
    specs = [pl.BlockSpec(memory_space=pltpu.SMEM)]
    args = [rel_bias]
    for g in range(N_GROUPS):
        for part in range(3):
            specs.append(pl.BlockSpec(blk, lambda hp, b, part=part: (b, 0, part * n_hp + hp)))
            args.append(qkv[g])
    return pl.pallas_call(
        _attn_kernel,
        grid=(n_hp, BATCH),
        in_specs=specs,
        out_specs=pl.BlockSpec(blk, lambda hp, b: (b, 0, hp)),
        out_shape=jax.ShapeDtypeStruct((BATCH, SEQ, D_MODEL), BF16),
        scratch_shapes=[pltpu.VMEM((N_GROUPS, HEADS_PER_STEP, N_KINDS, Q_TILE, K_TILE), F32),
                        pltpu.VMEM((N_GROUPS, SCRATCH_ROWS, LANES), F32),
                        pltpu.VMEM((N_GROUPS, SCRATCH_ROWS, LANES), F32),
                        pltpu.VMEM((N_GROUPS, SCRATCH_ROWS, LANES), F32)],
        compiler_params=_params(("arbitrary", "arbitrary")),
        name="dilated_attention",
    )(*args)


PROJ_TM = 1024


def _oproj_kernel(a_ref, w_ref, g_ref, x_ref, o_ref):
    m = jnp.dot(a_ref[...], w_ref[...], preferred_element_type=F32)
    o_ref[...] = x_ref[...] + _rms(m, g_ref[...])


def _out_proj(a2d, w_o, g_post, x2d):
    n = x2d.shape[0]
    return pl.pallas_call(
        _oproj_kernel,
        grid=(n // PROJ_TM,),
        in_specs=[pl.BlockSpec((PROJ_TM, D_MODEL), lambda i: (i, 0)),
                  pl.BlockSpec((D_MODEL, D_MODEL), lambda i: (0, 0)),
                  pl.BlockSpec((1, D_MODEL), lambda i: (0, 0)),
                  pl.BlockSpec((PROJ_TM, D_MODEL), lambda i: (i, 0))],
        out_specs=pl.BlockSpec((PROJ_TM, D_MODEL), lambda i: (i, 0)),
        out_shape=jax.ShapeDtypeStruct((n, D_MODEL), F32),
        compiler_params=_params(("arbitrary",)),
        name="attn_out_proj",
    )(a2d, w_o, g_post.reshape(1, D_MODEL), x2d)


MLP_TM = 512
MLP_FC = 1024


def _mlp_kernel(x_ref, gpre_ref, gpost_ref, wup_ref, wdn_ref, o_ref):
    x = x_ref[...]
    h = _rms(x, gpre_ref[...]).astype(BF16)
    acc = jnp.zeros((MLP_TM, D_MODEL), F32)
    for c in range(D_FF // MLP_FC):
        cols = slice(c * MLP_FC, (c + 1) * MLP_FC)
        u = jnp.dot(h, wup_ref[:, cols], preferred_element_type=F32)
        u = jnp.square(jnp.maximum(u, 0.0)).astype(BF16)
        acc = acc + jnp.dot(u, wdn_ref[cols, :], preferred_element_type=F32)
    o_ref[...] = x + _rms(acc, gpost_ref[...])


def _mlp(x2d, g_pre, g_post, w_up, w_down):
    n = x2d.shape[0]
    return pl.pallas_call(
        _mlp_kernel,
        grid=(n // MLP_TM,),
        in_specs=[pl.BlockSpec((MLP_TM, D_MODEL), lambda i: (i, 0)),
                  pl.BlockSpec((1, D_MODEL), lambda i: (0, 0)),
                  pl.BlockSpec((1, D_MODEL), lambda i: (0, 0)),
                  pl.BlockSpec((D_MODEL, D_FF), lambda i: (0, 0)),
                  pl.BlockSpec((D_FF, D_MODEL), lambda i: (0, 0))],
        out_specs=pl.BlockSpec((MLP_TM, D_MODEL), lambda i: (i, 0)),
        out_shape=jax.ShapeDtypeStruct((n, D_MODEL), F32),
        compiler_params=_params(("arbitrary",)),
        name="mlp",
    )(x2d, g_pre.reshape(1, D_MODEL), g_post.reshape(1, D_MODEL), w_up, w_down)


CONV_TS = 512
HALO = 16
CONV_ROWS = CONV_TS + 2 * HALO
CONV_RB = 64


def _conv_kernel(x_ref, gpre_ref, w1_ref, b1_ref, wdw_ref, bdw_ref, lng_ref, lnb_ref,
                 w2_ref, b2_ref, gpost_ref, o_ref, h_ref, u_ref, c_ref):
    j = pl.program_id(1)
    nj = pl.num_programs(1)
    s0 = pl.multiple_of(j * CONV_TS, CONV_TS)
    top = pl.multiple_of(jnp.maximum(s0 - HALO, 0), HALO)
    bot = pl.multiple_of(jnp.minimum(s0 + CONV_TS, SEQ - HALO), HALO)
    gpre = gpre_ref[...]

    h_ref[0:HALO, :] = _rms(x_ref[0, pl.ds(top, HALO), :], gpre).astype(BF16)
    h_ref[HALO:HALO + CONV_TS, :] = _rms(x_ref[0, pl.ds(s0, CONV_TS), :], gpre).astype(BF16)
    h_ref[HALO + CONV_TS:, :] = _rms(x_ref[0, pl.ds(bot, HALO), :], gpre).astype(BF16)

    h = h_ref[...]
    a = jnp.dot(h, w1_ref[:, :D_MODEL], preferred_element_type=F32) + b1_ref[:, :D_MODEL]
    gate = jnp.dot(h, w1_ref[:, D_MODEL:], preferred_element_type=F32) + b1_ref[:, D_MODEL:]
    u_ref[...] = a * jax.nn.sigmoid(gate)

    @pl.when(j == 0)
    def _():
        u_ref[0:HALO, :] = jnp.zeros((HALO, D_MODEL), F32)

    @pl.when(j == nj - 1)
    def _():
        u_ref[HALO + CONV_TS:, :] = jnp.zeros((HALO, D_MODEL), F32)

    def conv_block(rb, carry):
        r0 = pl.multiple_of(rb * CONV_RB, CONV_RB)
        for lc in range(D_MODEL // LANES):
            lanes = slice(lc * LANES, (lc + 1) * LANES)
            win = u_ref[pl.ds(r0, CONV_RB + 2 * HALO), lanes]
            acc = jnp.zeros((CONV_RB, LANES), F32)
            for t in range(CONV_WIDTH):
                off = HALO - CONV_WIDTH // 2 + t
                acc = acc + win[off:off + CONV_RB, :] * wdw_ref[t:t + 1, lanes]
            c_ref[pl.ds(r0, CONV_RB), lanes] = acc + bdw_ref[:, lanes]
        return carry

    lax.fori_loop(0, CONV_TS // CONV_RB, conv_block, 0)

    v = c_ref[...]
    mu = jnp.mean(v, axis=-1, keepdims=True)
    var = jnp.mean(jnp.square(v - mu), axis=-1, keepdims=True)
    y = (v - mu) * lax.rsqrt(var + LN_EPS) * lng_ref[...] + lnb_ref[...]
    y = y * jax.nn.sigmoid(y)
    z = jnp.dot(y.astype(BF16), w2_ref[...], preferred_element_type=F32) + b2_ref[...]
    o_ref[0] = x_ref[0, pl.ds(s0, CONV_TS), :] + _rms(z, gpost_ref[...])


def _conv_layer(x, g_pre, w1, b1, wdw, bdw, lng, lnb, w2, b2, g_post):
    row = lambda v: v.reshape(1, -1)
    const = lambda shape: pl.BlockSpec(shape, lambda b, j: (0, 0))
    return pl.pallas_call(
        _conv_kernel,
        grid=(BATCH, SEQ // CONV_TS),
        in_specs=[pl.BlockSpec((1, SEQ, D_MODEL), lambda b, j: (b, 0, 0)),
                  const((1, D_MODEL)),
                  const((D_MODEL, 2 * D_MODEL)),
                  const((1, 2 * D_MODEL)),
                  const((CONV_WIDTH, D_MODEL)),
                  const((1, D_MODEL)),
                  const((1, D_MODEL)),
                  const((1, D_MODEL)),
                  const((D_MODEL, D_MODEL)),
                  const((1, D_MODEL)),
                  const((1, D_MODEL))],
        out_specs=pl.BlockSpec((1, CONV_TS, D_MODEL), lambda b, j: (b, j, 0)),
        out_shape=jax.ShapeDtypeStruct((BATCH, SEQ, D_MODEL), F32),
        scratch_shapes=[pltpu.VMEM((CONV_ROWS, D_MODEL), BF16),
                        pltpu.VMEM((CONV_ROWS, D_MODEL), F32),
                        pltpu.VMEM((CONV_TS, D_MODEL), F32)],
        compiler_params=_params(("arbitrary", "arbitrary")),
        name="conformer_conv",
    )(x, row(g_pre), w1, row(b1), wdw, row(bdw), row(lng), row(lnb), w2, row(b2), row(g_post))


def kernel(x, rel_bias, norm_mix_pre, norm_mix_post, norm_mlp_pre, norm_mlp_post, attn_w_qkv,
           attn_w_o, conv_w_pw1, conv_b_pw1, conv_w_dw, conv_b_dw, conv_ln_g, conv_ln_b,
           conv_w_pw2, conv_b_pw2, mlp_w_up, mlp_w_down):
    n_tok = BATCH * SEQ
    x2d = x.reshape(n_tok, D_MODEL)

    h = _prenorm(x, norm_mix_pre[0])
    w_qkv = attn_w_qkv[0].astype(BF16)
    qkv = [_qkv_proj(h[g], w_qkv, g) for g in range(N_GROUPS)]
    a = _attention(qkv, rel_bias)
    x2d = _out_proj(a.reshape(n_tok, D_MODEL), attn_w_o[0].astype(BF16), norm_mix_post[0], x2d)
    x2d = _mlp(x2d, norm_mlp_pre[0], norm_mlp_post[0],
               mlp_w_up[0].astype(BF16), mlp_w_down[0].astype(BF16))

    x3d = _conv_layer(x2d.reshape(BATCH, SEQ, D_MODEL), norm_mix_pre[1],
                      conv_w_pw1[0].astype(BF16), conv_b_pw1[0], conv_w_dw[0], conv_b_dw[0],
                      conv_ln_g[0], conv_ln_b[0], conv_w_pw2[0].astype(BF16), conv_b_pw2[0],
                      norm_mix_post[1])
    x2d = _mlp(x3d.reshape(n_tok, D_MODEL), norm_mlp_pre[1], norm_mlp_post[1],
               mlp_w_up[1].astype(BF16), mlp_w_down[1].astype(BF16))
    return x2d.reshape(BATCH, SEQ, D_MODEL)
```

```python
import functools
import math

import jax
import jax.numpy as jnp
from jax import lax
from jax.experimental import pallas as pl
from jax.experimental.pallas import tpu as pltpu

D_MODEL = 1024
BATCH = 8
SEQ = 2048
HEAD_DIM = 64
N_HEADS = 16
DILATIONS = (1, 4, 16)
N_SIDE = 64
N_GROUPS = 3
GROUP_WIDTH = 3 * N_HEADS * HEAD_DIM
N_BUCKETS = 32
MAX_DISTANCE = 1024
CONV_WIDTH = 31
D_FF = 4 * D_MODEL
RMS_EPS = 1e-6
LN_EPS = 1e-5
NEG_INF = -1e30

F32 = jnp.float32
BF16 = jnp.bfloat16

LANES = 128
Q_TILE = 128
K_TILE = Q_TILE + 2 * N_SIDE
VMEM_LIMIT = 56 * 1024 * 1024


def _rms(x, g):
    return x * lax.rsqrt(jnp.mean(x * x, axis=-1, keepdims=True) + RMS_EPS) * g


def _params(semantics):
    return pltpu.CompilerParams(dimension_semantics=semantics, vmem_limit_bytes=VMEM_LIMIT)


NORM_ROWS = 256


def _prenorm_kernel(x_ref, g_ref, *refs):
    out_refs, slab_ref = refs[:-1], refs[-1]

    def chunk(i, carry):
        rows = pl.ds(pl.multiple_of(i * NORM_ROWS, NORM_ROWS), NORM_ROWS)
        hn = _rms(x_ref[0, rows, :], g_ref[...])
        for k in range(D_MODEL // LANES):
            slab_ref[k, rows, :] = hn[:, k * LANES:(k + 1) * LANES]
        for out_ref, r in zip(out_refs, DILATIONS):
            if r == 1:
                out_ref[0, rows, :] = hn.astype(BF16)
        return carry

    lax.fori_loop(0, SEQ // NORM_ROWS, chunk, 0)
    for out_ref, r in zip(out_refs, DILATIONS):
        if r == 1:
            continue
        L = SEQ // r
        for c in range(r):
            for k in range(D_MODEL // LANES):
                out_ref[0, c * L:(c + 1) * L, k * LANES:(k + 1) * LANES] = (
                    slab_ref[k, pl.ds(c, L, stride=r), :].astype(BF16))


def _prenorm(x, g):
    blk = pl.BlockSpec((1, SEQ, D_MODEL), lambda b: (b, 0, 0))
    return pl.pallas_call(
        _prenorm_kernel,
        grid=(BATCH,),
        in_specs=[blk, pl.BlockSpec((1, D_MODEL), lambda b: (0, 0))],
        out_specs=[blk] * N_GROUPS,
        out_shape=[jax.ShapeDtypeStruct((BATCH, SEQ, D_MODEL), BF16)] * N_GROUPS,
        scratch_shapes=[pltpu.VMEM((D_MODEL // LANES, SEQ, LANES), F32)],
        compiler_params=_params(("arbitrary",)),
        name="prenorm",
    )(x, g.reshape(1, D_MODEL))


QKV_TN = 1024
QKV_TM = 512


def _qkv_kernel(h_ref, w_ref, o_ref):
    for m in range(SEQ // QKV_TM):
        rows = slice(m * QKV_TM, (m + 1) * QKV_TM)
        o_ref[0, rows, :] = jnp.dot(h_ref[0, rows, :], w_ref[...],
                                    preferred_element_type=F32).astype(BF16)


def _qkv_proj(h, w_qkv, g):
    nj = GROUP_WIDTH // QKV_TN
    return pl.pallas_call(
        _qkv_kernel,
        grid=(BATCH, nj),
        in_specs=[pl.BlockSpec((1, SEQ, D_MODEL), lambda b, j: (b, 0, 0)),
                  pl.BlockSpec((D_MODEL, QKV_TN), lambda b, j: (0, g * nj + j))],
        out_specs=pl.BlockSpec((1, SEQ, QKV_TN), lambda b, j: (b, 0, j)),
        out_shape=jax.ShapeDtypeStruct((BATCH, SEQ, GROUP_WIDTH), BF16),
        compiler_params=_params(("arbitrary", "arbitrary")),
        name=f"qkv_proj_g{g}",
    )(h, w_qkv)


HEADS_PER_STEP = 4
PAIRS_PER_STEP = HEADS_PER_STEP // 2
STEP_LANES = HEADS_PER_STEP * HEAD_DIM
N_Q_TILES = SEQ // Q_TILE
N_KINDS = 3
PADDED_PITCH = 24
SCRATCH_ROWS = SEQ // 16 * PADDED_PITCH
TILE_UNROLL = 8


def _bias_row(rel_ref, g, head):
    r = DILATIONS[g]
    nb = N_BUCKETS // 2
    max_exact = nb // 2
    delta = lax.broadcasted_iota(jnp.int32, (8, K_TILE), 1) - N_SIDE
    rel = delta * r
    n = jnp.abs(rel)
    nf = jnp.maximum(n, 1).astype(F32)
    large = max_exact + (jnp.log(nf / max_exact) / math.log(MAX_DISTANCE / max_exact)
                         * (nb - max_exact)).astype(jnp.int32)
    large = jnp.minimum(large, nb - 1)
    bucket = jnp.where(rel > 0, nb, 0) + jnp.where(n < max_exact, n, large)
    u = jnp.zeros((8, K_TILE), F32)
    for b in range(N_BUCKETS):
        u = jnp.where(bucket == b, rel_ref[b, g * N_HEADS + head], u)
    return jnp.where(jnp.abs(delta) <= N_SIDE, u, NEG_INF)


def _build_bias_tiles(rel_ref, bias_ref, head0):
    qi = lax.broadcasted_iota(jnp.int32, (Q_TILE, K_TILE), 0)
    kj = lax.broadcasted_iota(jnp.int32, (Q_TILE, K_TILE), 1)
    for g in range(N_GROUPS):
        for h in range(HEADS_PER_STEP):
            u = _bias_row(rel_ref, g, head0 + h)
            ub = jnp.broadcast_to(u[0:1, :], (Q_TILE, K_TILE))
            for kind in range(N_KINDS):
                shift = (N_SIDE * kind - N_SIDE) % K_TILE
                t = pltpu.roll(ub, shift, 1, stride=1, stride_axis=0)
                in_band = jnp.abs(kj - qi - N_SIDE * kind) <= N_SIDE
                bias_ref[g, h, kind] = jnp.where(in_band, t, NEG_INF)


def _token_rows(r, c, l0):
    if r == 1:
        return pl.ds(l0, Q_TILE)
    pitch = PADDED_PITCH if r % 16 == 0 else r
    return pl.ds(l0 * pitch + c, Q_TILE, stride=pitch)


def _load_tokens(ref, g, t):
    r = DILATIONS[g]
    if r % 16:
        return ref[g, pl.ds(pl.multiple_of(t * Q_TILE, Q_TILE), Q_TILE), :]
    n = Q_TILE // r
    base = pl.multiple_of(t * n * PADDED_PITCH, 8)
    return jnp.concatenate([ref[g, pl.ds(base + i * PADDED_PITCH, r), :] for i in range(n)], axis=0)


def _attn_unit(q_ref, k_ref, v_ref, lanes, q0, k0, n_keys, bias0, bias1):
    q = q_ref[0, pl.ds(q0, Q_TILE), lanes]
    k = k_ref[0, pl.ds(k0, n_keys), lanes]
    v = v_ref[0, pl.ds(k0, n_keys), lanes]
    lane = lax.broadcasted_iota(jnp.int32, (1, LANES), 1)
    first = lane < HEAD_DIM
    scale = HEAD_DIM ** -0.5
    m0 = jnp.where(first, scale, 0.0).astype(BF16)
    m1 = jnp.where(first, 0.0, scale).astype(BF16)
    q2 = jnp.concatenate([q * m0, q * m1], axis=0)
    s = lax.dot_general(q2, k, (((1,), (1,)), ((), ())), preferred_element_type=F32)
    s = s + jnp.concatenate([bias0, bias1], axis=0)
    m = jnp.max(s, axis=-1, keepdims=True)
    p = jnp.exp(s - m).astype(BF16)
    v1 = jnp.concatenate([v, jnp.ones((n_keys, LANES), BF16)], axis=1)
    o2 = jnp.dot(p, v1, preferred_element_type=F32)
    acc = jnp.where(first, o2[:Q_TILE, :LANES], o2[Q_TILE:, :LANES])
    l = jnp.where(first, o2[:Q_TILE, LANES:], o2[Q_TILE:, LANES:])
    m_t = jnp.where(first, m[:Q_TILE], m[Q_TILE:])
    return acc, m_t, l


def _attn_kernel(rel_ref, q0_ref, k0_ref, v0_ref, q1_ref, k1_ref, v1_ref, q2_ref, k2_ref, v2_ref,
                 o_ref, bias_ref, acc_ref, m_ref, l_ref):
    hp = pl.program_id(0)

    @pl.when(pl.program_id(1) == 0)
    def _():
        _build_bias_tiles(rel_ref, bias_ref, hp * HEADS_PER_STEP)

    qkv = ((q0_ref, k0_ref, v0_ref), (q1_ref, k1_ref, v1_ref), (q2_ref, k2_ref, v2_ref))
    for pair in range(PAIRS_PER_STEP):
        lanes = slice(pair * LANES, (pair + 1) * LANES)
        for g in range(N_GROUPS):
            r = DILATIONS[g]
            L = SEQ // r
            tiles_per_seq = L // Q_TILE
            q_ref, k_ref, v_ref = qkv[g]

            def tile_body(t, carry, g=g, r=r, tiles_per_seq=tiles_per_seq,
                          q_ref=q_ref, k_ref=k_ref, v_ref=v_ref, lanes=lanes, pair=pair):
                q0 = pl.multiple_of(t * Q_TILE, Q_TILE)
                if tiles_per_seq == 1:
                    k0, n_keys = q0, Q_TILE
                    bias0 = bias_ref[g, 2 * pair, 0, :, :Q_TILE]
                    bias1 = bias_ref[g, 2 * pair + 1, 0, :, :Q_TILE]
                else:
                    tl = t % tiles_per_seq
                    kind = jnp.where(tl == 0, 0, jnp.where(tl == tiles_per_seq - 1, 2, 1))
                    k0, n_keys = pl.multiple_of(q0 - N_SIDE * kind, N_SIDE), K_TILE
                    bias0 = bias_ref[g, 2 * pair, kind]
                    bias1 = bias_ref[g, 2 * pair + 1, kind]
                acc, m, l = _attn_unit(q_ref, k_ref, v_ref, lanes, q0, k0, n_keys, bias0, bias1)
                rows = _token_rows(r, t // tiles_per_seq, (t % tiles_per_seq) * Q_TILE)
                acc_ref[g, rows, :] = acc
                m_ref[g, rows, :] = m
                l_ref[g, rows, :] = l
                return carry

            lax.fori_loop(0, N_Q_TILES, tile_body, 0, unroll=TILE_UNROLL)

        def merge_body(t, carry, lanes=lanes):
            m = [_load_tokens(m_ref, g, t) for g in range(N_GROUPS)]
            top = jnp.maximum(jnp.maximum(m[0], m[1]), m[2])
            w = [jnp.exp(x - top) for x in m]
            num = sum(w[g] * _load_tokens(acc_ref, g, t) for g in range(N_GROUPS))
            den = sum(w[g] * _load_tokens(l_ref, g, t) for g in range(N_GROUPS))
            rows = pl.ds(pl.multiple_of(t * Q_TILE, Q_TILE), Q_TILE)
            o_ref[0, rows, lanes] = (num / den).astype(BF16)
            return carry

        lax.fori_loop(0, N_Q_TILES, merge_body, 0)


def _attention(qkv, rel_bias):
    n_hp = N_HEADS // HEADS_PER_STEP
    blk = (1, SEQ, STEP_LANES)
    specs = [pl.BlockSpec(memory_space=pltpu.SMEM)]
    args = [rel_bias]
    for g in range(N_GROUPS):
        for part in range(3):
            specs.append(pl.BlockSpec(blk, lambda hp, b, part=part: (b, 0, part * n_hp + hp)))
            args.append(qkv[g])
    return pl.pallas_call(
        _attn_kernel,
        grid=(n_hp, BATCH),
        in_specs=specs,
        out_specs=pl.BlockSpec(blk, lambda hp, b: (b, 0, hp)),
        out_shape=jax.ShapeDtypeStruct((BATCH, SEQ, D_MODEL), BF16),
        scratch_shapes=[pltpu.VMEM((N_GROUPS, HEADS_PER_STEP, N_KINDS, Q_TILE, K_TILE), F32),
                        pltpu.VMEM((N_GROUPS, SCRATCH_ROWS, LANES), F32),
                        pltpu.VMEM((N_GROUPS, SCRATCH_ROWS, LANES), F32),
                        pltpu.VMEM((N_GROUPS, SCRATCH_ROWS, LANES), F32)],
        compiler_params=_params(("arbitrary", "arbitrary")),
        name="dilated_attention",
    )(*args)


PROJ_TM = 1024


def _oproj_kernel(a_ref, w_ref, g_ref, x_ref, o_ref):
    m = jnp.dot(a_ref[...], w_ref[...], preferred_element_type=F32)
    o_ref[...] = x_ref[...] + _rms(m, g_ref[...])


def _out_proj(a2d, w_o, g_post, x2d):
    n = x2d.shape[0]
    return pl.pallas_call(
        _oproj_kernel,
        grid=(n // PROJ_TM,),
        in_specs=[pl.BlockSpec((PROJ_TM, D_MODEL), lambda i: (i, 0)),
                  pl.BlockSpec((D_MODEL, D_MODEL), lambda i: (0, 0)),
                  pl.BlockSpec((1, D_MODEL), lambda i: (0, 0)),
                  pl.BlockSpec((PROJ_TM, D_MODEL), lambda i: (i, 0))],
        out_specs=pl.BlockSpec((PROJ_TM, D_MODEL), lambda i: (i, 0)),
        out_shape=jax.ShapeDtypeStruct((n, D_MODEL), F32),
        compiler_params=_params(("arbitrary",)),
        name="attn_out_proj",
    )(a2d, w_o, g_post.reshape(1, D_MODEL), x2d)


MLP_TM = 512
MLP_FC = 1024


def _mlp_kernel(x_ref, gpre_ref, gpost_ref, wup_ref, wdn_ref, o_ref):
    x = x_ref[...]
    h = _rms(x, gpre_ref[...]).astype(BF16)
    acc = jnp.zeros((MLP_TM, D_MODEL), F32)
    for c in range(D_FF // MLP_FC):
        cols = slice(c * MLP_FC, (c + 1) * MLP_FC)
        u = jnp.dot(h, wup_ref[:, cols], preferred_element_type=F32)
        u = jnp.square(jnp.maximum(u, 0.0)).astype(BF16)
        acc = acc + jnp.dot(u, wdn_ref[cols, :], preferred_element_type=F32)
    o_ref[...] = x + _rms(acc, gpost_ref[...])


def _mlp(x2d, g_pre, g_post, w_up, w_down):
    n = x2d.shape[0]
    return pl.pallas_call(
        _mlp_kernel,
        grid=(n // MLP_TM,),
        in_specs=[pl.BlockSpec((MLP_TM, D_MODEL), lambda i: (i, 0)),
                  pl.BlockSpec((1, D_MODEL), lambda i: (0, 0)),
                  pl.BlockSpec((1, D_MODEL), lambda i: (0, 0)),
                  pl.BlockSpec((D_MODEL, D_FF), lambda i: (0, 0)),
                  pl.BlockSpec((D_FF, D_MODEL), lambda i: (0, 0))],
        out_specs=pl.BlockSpec((MLP_TM, D_MODEL), lambda i: (i, 0)),
        out_shape=jax.ShapeDtypeStruct((n, D_MODEL), F32),
        compiler_params=_params(("arbitrary",)),
        name="mlp",
    )(x2d, g_pre.reshape(1, D_MODEL), g_post.reshape(1, D_MODEL), w_up, w_down)


CONV_TS = 512
HALO = 16
CONV_ROWS = CONV_TS + 2 * HALO
CONV_RB = 64


def _conv_kernel(x_ref, gpre_ref, w1_ref, b1_ref, wdw_ref, bdw_ref, lng_ref, lnb_ref,
                 w2_ref, b2_ref, gpost_ref, o_ref, h_ref, u_ref, c_ref):
    j = pl.program_id(1)
    nj = pl.num_programs(1)
    s0 = pl.multiple_of(j * CONV_TS, CONV_TS)
    top = pl.multiple_of(jnp.maximum(s0 - HALO, 0), HALO)
    bot = pl.multiple_of(jnp.minimum(s0 + CONV_TS, SEQ - HALO), HALO)
    gpre = gpre_ref[...]

    h_ref[0:HALO, :] = _rms(x_ref[0, pl.ds(top, HALO), :], gpre).astype(BF16)
    h_ref[HALO:HALO + CONV_TS, :] = _rms(x_ref[0, pl.ds(s0, CONV_TS), :], gpre).astype(BF16)
    h_ref[HALO + CONV_TS:, :] = _rms(x_ref[0, pl.ds(bot, HALO), :], gpre).astype(BF16)

    h = h_ref[...]
    a = jnp.dot(h, w1_ref[:, :D_MODEL], preferred_element_type=F32) + b1_ref[:, :D_MODEL]
    gate = jnp.dot(h, w1_ref[:, D_MODEL:], preferred_element_type=F32) + b1_ref[:, D_MODEL:]
    u_ref[...] = a * jax.nn.sigmoid(gate)

    @pl.when(j == 0)
    def _():
        u_ref[0:HALO, :] = jnp.zeros((HALO, D_MODEL), F32)

    @pl.when(j == nj - 1)
    def _():
        u_ref[HALO + CONV_TS:, :] = jnp.zeros((HALO, D_MODEL), F32)

    def conv_block(rb, carry):
        r0 = pl.multiple_of(rb * CONV_RB, CONV_RB)
        for lc in range(D_MODEL // LANES):
            lanes = slice(lc * LANES, (lc + 1) * LANES)
            win = u_ref[pl.ds(r0, CONV_RB + 2 * HALO), lanes]
            acc = bdw_ref[:, lanes]
            for shift in range(8):
                part = None
                for base in range(0, 2 * HALO, 8):
                    t = base + shift - (HALO - CONV_WIDTH // 2)
                    if 0 <= t < CONV_WIDTH:
                        term = win[base:base + CONV_RB + 8, :] * wdw_ref[t:t + 1, lanes]
                        part = term if part is None else part + term
                acc = acc + part[shift:shift + CONV_RB, :]
            c_ref[pl.ds(r0, CONV_RB), lanes] = acc
        return carry

    lax.fori_loop(0, CONV_TS // CONV_RB, conv_block, 0)

    v = c_ref[...]
    mu = jnp.mean(v, axis=-1, keepdims=True)
    var = jnp.mean(jnp.square(v - mu), axis=-1, keepdims=True)
    y = (v - mu) * lax.rsqrt(var + LN_EPS) * lng_ref[...] + lnb_ref[...]
    y = y * jax.nn.sigmoid(y)
    z = jnp.dot(y.astype(BF16), w2_ref[...], preferred_element_type=F32) + b2_ref[...]
    o_ref[0] = x_ref[0, pl.ds(s0, CONV_TS), :] + _rms(z, gpost_ref[...])


def _conv_layer(x, g_pre, w1, b1, wdw, bdw, lng, lnb, w2, b2, g_post):
    row = lambda v: v.reshape(1, -1)
    const = lambda shape: pl.BlockSpec(shape, lambda b, j: (0, 0))
    return pl.pallas_call(
        _conv_kernel,
        grid=(BATCH, SEQ // CONV_TS),
        in_specs=[pl.BlockSpec((1, SEQ, D_MODEL), lambda b, j: (b, 0, 0)),
                  const((1, D_MODEL)),
                  const((D_MODEL, 2 * D_MODEL)),
                  const((1, 2 * D_MODEL)),
                  const((CONV_WIDTH, D_MODEL)),
                  const((1, D_MODEL)),
                  const((1, D_MODEL)),
                  const((1, D_MODEL)),
                  const((D_MODEL, D_MODEL)),
                  const((1, D_MODEL)),
                  const((1, D_MODEL))],
        out_specs=pl.BlockSpec((1, CONV_TS, D_MODEL), lambda b, j: (b, j, 0)),
        out_shape=jax.ShapeDtypeStruct((BATCH, SEQ, D_MODEL), F32),
        scratch_shapes=[pltpu.VMEM((CONV_ROWS, D_MODEL), BF16),
                        pltpu.VMEM((CONV_ROWS, D_MODEL), F32),
                        pltpu.VMEM((CONV_TS, D_MODEL), F32)],
        compiler_params=_params(("arbitrary", "arbitrary")),
        name="conformer_conv",
    )(x, row(g_pre), w1, row(b1), wdw, row(bdw), row(lng), row(lnb), w2, row(b2), row(g_post))


def kernel(x, rel_bias, norm_mix_pre, norm_mix_post, norm_mlp_pre, norm_mlp_post, attn_w_qkv,
           attn_w_o, conv_w_pw1, conv_b_pw1, conv_w_dw, conv_b_dw, conv_ln_g, conv_ln_b,
           conv_w_pw2, conv_b_pw2, mlp_w_up, mlp_w_down):
    n_tok = BATCH * SEQ
    x2d = x.reshape(n_tok, D_MODEL)

    h = _prenorm(x, norm_mix_pre[0])
    w_qkv = attn_w_qkv[0].astype(BF16)
    qkv = [_qkv_proj(h[g], w_qkv, g) for g in range(N_GROUPS)]
    a = _attention(qkv, rel_bias)
    x2d = _out_proj(a.reshape(n_tok, D_MODEL), attn_w_o[0].astype(BF16), norm_mix_post[0], x2d)
    x2d = _mlp(x2d, norm_mlp_pre[0], norm_mlp_post[0],
               mlp_w_up[0].astype(BF16), mlp_w_down[0].astype(BF16))

    x3d = _conv_layer(x2d.reshape(BATCH, SEQ, D_MODEL), norm_mix_pre[1],
                      conv_w_pw1[0].astype(BF16), conv_b_pw1[0], conv_w_dw[0], conv_b_dw[0],
                      conv_ln_g[0], conv_ln_b[0], conv_w_pw2[0].astype(BF16), conv_b_pw2[0],
                      norm_mix_post[1])
    x2d = _mlp(x3d.reshape(n_tok, D_MODEL), norm_mlp_pre[1], norm_mlp_post[1],
               mlp_w_up[1].astype(BF16), mlp_w_down[1].astype(BF16))
    return x2d.reshape(BATCH, SEQ, D_MODEL)
```

```python
import functools
import math

import jax
import jax.numpy as jnp
from jax import lax
from jax.experimental import pallas as pl
from jax.experimental.pallas import tpu as pltpu

D_MODEL = 1024
BATCH = 8
SEQ = 2048
HEAD_DIM = 64
N_HEADS = 16
DILATIONS = (1, 4, 16)
N_SIDE = 64
N_GROUPS = 3
GROUP_WIDTH = 3 * N_HEADS * HEAD_DIM
N_BUCKETS = 32
MAX_DISTANCE = 1024
CONV_WIDTH = 31
D_FF = 4 * D_MODEL
RMS_EPS = 1e-6
LN_EPS = 1e-5
NEG_INF = -1e30

F32 = jnp.float32
BF16 = jnp.bfloat16

LANES = 128
Q_TILE = 128
K_TILE = Q_TILE + 2 * N_SIDE
VMEM_LIMIT = 56 * 1024 * 1024


def _rms(x, g):
    return x * lax.rsqrt(jnp.mean(x * x, axis=-1, keepdims=True) + RMS_EPS) * g


def _params(semantics):
    return pltpu.CompilerParams(dimension_semantics=semantics, vmem_limit_bytes=VMEM_LIMIT)


NORM_ROWS = 256


def _prenorm_kernel(x_ref, g_ref, *refs):
    out_refs, slab_ref = refs[:-1], refs[-1]

    def chunk(i, carry):
        rows = pl.ds(pl.multiple_of(i * NORM_ROWS, NORM_ROWS), NORM_ROWS)
        hn = _rms(x_ref[0, rows, :], g_ref[...])
        for k in range(D_MODEL // LANES):
            slab_ref[k, rows, :] = hn[:, k * LANES:(k + 1) * LANES]
        for out_ref, r in zip(out_refs, DILATIONS):
            if r == 1:
                out_ref[0, rows, :] = hn.astype(BF16)
        return carry

    lax.fori_loop(0, SEQ // NORM_ROWS, chunk, 0)
    for out_ref, r in zip(out_refs, DILATIONS):
        if r == 1:
            continue
        L = SEQ // r
        for c in range(r):
            for k in range(D_MODEL // LANES):
                out_ref[0, c * L:(c + 1) * L, k * LANES:(k + 1) * LANES] = (
                    slab_ref[k, pl.ds(c, L, stride=r), :].astype(BF16))


def _prenorm(x, g):
    blk = pl.BlockSpec((1, SEQ, D_MODEL), lambda b: (b, 0, 0))
    return pl.pallas_call(
        _prenorm_kernel,
        grid=(BATCH,),
        in_specs=[blk, pl.BlockSpec((1, D_MODEL), lambda b: (0, 0))],
        out_specs=[blk] * N_GROUPS,
        out_shape=[jax.ShapeDtypeStruct((BATCH, SEQ, D_MODEL), BF16)] * N_GROUPS,
        scratch_shapes=[pltpu.VMEM((D_MODEL // LANES, SEQ, LANES), F32)],
        compiler_params=_params(("arbitrary",)),
        name="prenorm",
    )(x, g.reshape(1, D_MODEL))


QKV_TN = 1024
QKV_TM = 512


def _qkv_kernel(h_ref, w_ref, o_ref, wb_ref):
    @pl.when(pl.program_id(1) == 0)
    def _():
        wb_ref[...] = w_ref[...].astype(BF16)

    for m in range(SEQ // QKV_TM):
        rows = slice(m * QKV_TM, (m + 1) * QKV_TM)
        o_ref[0, rows, :] = jnp.dot(h_ref[0, rows, :], wb_ref[...],
                                    preferred_element_type=F32).astype(BF16)


def _qkv_proj(h, w_qkv, g):
    nj = GROUP_WIDTH // QKV_TN
    return pl.pallas_call(
        _qkv_kernel,
        grid=(nj, BATCH),
        in_specs=[pl.BlockSpec((1, SEQ, D_MODEL), lambda j, b: (b, 0, 0)),
                  pl.BlockSpec((D_MODEL, QKV_TN), lambda j, b: (0, g * nj + j))],
        out_specs=pl.BlockSpec((1, SEQ, QKV_TN), lambda j, b: (b, 0, j)),
        out_shape=jax.ShapeDtypeStruct((BATCH, SEQ, GROUP_WIDTH), BF16),
        scratch_shapes=[pltpu.VMEM((D_MODEL, QKV_TN), BF16)],
        compiler_params=_params(("arbitrary", "arbitrary")),
        name=f"qkv_proj_g{g}",
    )(h, w_qkv)


HEADS_PER_STEP = 4
PAIRS_PER_STEP = HEADS_PER_STEP // 2
STEP_LANES = HEADS_PER_STEP * HEAD_DIM
N_Q_TILES = SEQ // Q_TILE
N_KINDS = 3
PADDED_PITCH = 24
SCRATCH_ROWS = SEQ // 16 * PADDED_PITCH
TILE_UNROLL = 16


def _bias_row(rel_ref, g, head):
    r = DILATIONS[g]
    nb = N_BUCKETS // 2
    max_exact = nb // 2
    delta = lax.broadcasted_iota(jnp.int32, (8, K_TILE), 1) - N_SIDE
    rel = delta * r
    n = jnp.abs(rel)
    nf = jnp.maximum(n, 1).astype(F32)
    large = max_exact + (jnp.log(nf / max_exact) / math.log(MAX_DISTANCE / max_exact)
                         * (nb - max_exact)).astype(jnp.int32)
    large = jnp.minimum(large, nb - 1)
    bucket = jnp.where(rel > 0, nb, 0) + jnp.where(n < max_exact, n, large)
    u = jnp.zeros((8, K_TILE), F32)
    for b in range(N_BUCKETS):
        u = jnp.where(bucket == b, rel_ref[b, g * N_HEADS + head], u)
    return jnp.where(jnp.abs(delta) <= N_SIDE, u, NEG_INF)


def _build_bias_tiles(rel_ref, bias_ref, head0):
    qi = lax.broadcasted_iota(jnp.int32, (Q_TILE, K_TILE), 0)
    kj = lax.broadcasted_iota(jnp.int32, (Q_TILE, K_TILE), 1)
    for g in range(N_GROUPS):
        for h in range(HEADS_PER_STEP):
            u = _bias_row(rel_ref, g, head0 + h)
            ub = jnp.broadcast_to(u[0:1, :], (Q_TILE, K_TILE))
            for kind in range(N_KINDS):
                shift = (N_SIDE * kind - N_SIDE) % K_TILE
                t = pltpu.roll(ub, shift, 1, stride=1, stride_axis=0)
                in_band = jnp.abs(kj - qi - N_SIDE * kind) <= N_SIDE
                bias_ref[g, h, kind] = jnp.where(in_band, t, NEG_INF)


def _token_rows(r, c, l0):
    if r == 1:
        return pl.ds(l0, Q_TILE)
    pitch = PADDED_PITCH if r % 16 == 0 else r
    return pl.ds(l0 * pitch + c, Q_TILE, stride=pitch)


def _load_tokens(ref, g, t):
    r = DILATIONS[g]
    if r % 16:
        return ref[g, pl.ds(pl.multiple_of(t * Q_TILE, Q_TILE), Q_TILE), :]
    n = Q_TILE // r
    base = pl.multiple_of(t * n * PADDED_PITCH, 8)
    return jnp.concatenate([ref[g, pl.ds(base + i * PADDED_PITCH, r), :] for i in range(n)], axis=0)


def _split_heads(q):
    first = lax.broadcasted_iota(jnp.int32, (1, LANES), 1) < HEAD_DIM
    scale = HEAD_DIM ** -0.5
    m0 = jnp.where(first, scale, 0.0).astype(BF16)
    m1 = jnp.where(first, 0.0, scale).astype(BF16)
    return jnp.concatenate([q * m0, q * m1], axis=0)


def _softmax_pv(s, v):
    first = lax.broadcasted_iota(jnp.int32, (1, LANES), 1) < HEAD_DIM
    m = jnp.max(s, axis=-1, keepdims=True)
    p = jnp.exp(s - m).astype(BF16)
    v1 = jnp.concatenate([v, jnp.ones((v.shape[0], LANES), BF16)], axis=1)
    o2 = jnp.dot(p, v1, preferred_element_type=F32)
    acc = jnp.where(first, o2[:Q_TILE, :LANES], o2[Q_TILE:, :LANES])
    l = jnp.where(first, o2[:Q_TILE, LANES:], o2[Q_TILE:, LANES:])
    m_t = jnp.where(first, m[:Q_TILE], m[Q_TILE:])
    return acc, m_t, l


_NT = (((1,), (1,)), ((), ()))


def _attn_kernel(rel_ref, q0_ref, k0_ref, v0_ref, q1_ref, k1_ref, v1_ref, q2_ref, k2_ref, v2_ref,
                 o_ref, bias_ref, acc_ref, m_ref, l_ref):
    hp = pl.program_id(0)

    @pl.when(pl.program_id(1) == 0)
    def _():
        _build_bias_tiles(rel_ref, bias_ref, hp * HEADS_PER_STEP)

    qkv = ((q0_ref, k0_ref, v0_ref), (q1_ref, k1_ref, v1_ref), (q2_ref, k2_ref, v2_ref))
    for pair in range(PAIRS_PER_STEP):
        lanes = slice(pair * LANES, (pair + 1) * LANES)
        for g in range(N_GROUPS):
            r = DILATIONS[g]
            tiles_per_seq = SEQ // r // Q_TILE
            q_ref, k_ref, v_ref = qkv[g]

            def put(t, res, g=g, r=r, tiles_per_seq=tiles_per_seq):
                rows = _token_rows(r, t // tiles_per_seq, (t % tiles_per_seq) * Q_TILE)
                for ref, val in zip((acc_ref, m_ref, l_ref), res):
                    ref[g, rows, :] = val

            if tiles_per_seq > 1:
                def tile_body(t, carry, g=g, tiles_per_seq=tiles_per_seq, put=put,
                              q_ref=q_ref, k_ref=k_ref, v_ref=v_ref, lanes=lanes, pair=pair):
                    q0 = pl.multiple_of(t * Q_TILE, Q_TILE)
                    tl = t % tiles_per_seq
                    kind = jnp.where(tl == 0, 0, jnp.where(tl == tiles_per_seq - 1, 2, 1))
                    k0 = pl.multiple_of(q0 - N_SIDE * kind, N_SIDE)
                    q2 = _split_heads(q_ref[0, pl.ds(q0, Q_TILE), lanes])
                    s = lax.dot_general(q2, k_ref[0, pl.ds(k0, K_TILE), lanes], _NT,
                                        preferred_element_type=F32)
                    s = s + jnp.concatenate([bias_ref[g, 2 * pair, kind],
                                             bias_ref[g, 2 * pair + 1, kind]], axis=0)
                    put(t, _softmax_pv(s, v_ref[0, pl.ds(k0, K_TILE), lanes]))
                    return carry

                lax.fori_loop(0, N_Q_TILES, tile_body, 0, unroll=TILE_UNROLL)
            else:
                def tile_body(t2, carry, g=g, put=put,
                              q_ref=q_ref, k_ref=k_ref, v_ref=v_ref, lanes=lanes, pair=pair):
                    q0 = pl.multiple_of(t2 * 2 * Q_TILE, 2 * Q_TILE)
                    q = q_ref[0, pl.ds(q0, 2 * Q_TILE), lanes]
                    q2 = jnp.concatenate([_split_heads(q[:Q_TILE]), _split_heads(q[Q_TILE:])], axis=0)
                    s = lax.dot_general(q2, k_ref[0, pl.ds(q0, 2 * Q_TILE), lanes], _NT,
                                        preferred_element_type=F32)
                    bias = jnp.concatenate([bias_ref[g, 2 * pair, 0, :, :Q_TILE],
                                            bias_ref[g, 2 * pair + 1, 0, :, :Q_TILE]], axis=0)
                    for i in range(2):
                        si = s[2 * i * Q_TILE:2 * (i + 1) * Q_TILE, i * Q_TILE:(i + 1) * Q_TILE]
                        v = v_ref[0, pl.ds(q0 + i * Q_TILE, Q_TILE), lanes]
                        put(2 * t2 + i, _softmax_pv(si + bias, v))
                    return carry

                lax.fori_loop(0, N_Q_TILES // 2, tile_body, 0, unroll=TILE_UNROLL // 2)

        def merge_body(t, carry, lanes=lanes):
            m = [_load_tokens(m_ref, g, t) for g in range(N_GROUPS)]
            top = jnp.maximum(jnp.maximum(m[0], m[1]), m[2])
            w = [jnp.exp(x - top) for x in m]
            num = sum(w[g] * _load_tokens(acc_ref, g, t) for g in range(N_GROUPS))
            den = sum(w[g] * _load_tokens(l_ref, g, t) for g in range(N_GROUPS))
            rows = pl.ds(pl.multiple_of(t * Q_TILE, Q_TILE), Q_TILE)
            o_ref[0, rows, lanes] = (num / den).astype(BF16)
            return carry

        lax.fori_loop(0, N_Q_TILES, merge_body, 0)


def _attention(qkv, rel_bias):
    n_hp = N_HEADS // HEADS_PER_STEP
    blk = (1, SEQ, STEP_LANES)
    specs = [pl.BlockSpec(memory_space=pltpu.SMEM)]
    args = [rel_bias]
    for g in range(N_GROUPS):
        for part in range(3):
            specs.append(pl.BlockSpec(blk, lambda hp, b, part=part: (b, 0, part * n_hp + hp)))
            args.append(qkv[g])
    return pl.pallas_call(
        _attn_kernel,
        grid=(n_hp, BATCH),
        in_specs=specs,
        out_specs=pl.BlockSpec(blk, lambda hp, b: (b, 0, hp)),
        out_shape=jax.ShapeDtypeStruct((BATCH, SEQ, D_MODEL), BF16),
        scratch_shapes=[pltpu.VMEM((N_GROUPS, HEADS_PER_STEP, N_KINDS, Q_TILE, K_TILE), F32),
                        pltpu.VMEM((N_GROUPS, SCRATCH_ROWS, LANES), F32),
                        pltpu.VMEM((N_GROUPS, SCRATCH_ROWS, LANES), F32),
                        pltpu.VMEM((N_GROUPS, SCRATCH_ROWS, LANES), F32)],
        compiler_params=_params(("arbitrary", "arbitrary")),
        name="dilated_attention",
    )(*args)


PROJ_TM = 1024


def _oproj_kernel(a_ref, w_ref, g_ref, x_ref, o_ref):
    m = jnp.dot(a_ref[...], w_ref[...], preferred_element_type=F32)
    o_ref[...] = x_ref[...] + _rms(m, g_ref[...])


def _out_proj(a2d, w_o, g_post, x2d):
    n = x2d.shape[0]
    return pl.pallas_call(
        _oproj_kernel,
        grid=(n // PROJ_TM,),
        in_specs=[pl.BlockSpec((PROJ_TM, D_MODEL), lambda i: (i, 0)),
                  pl.BlockSpec((D_MODEL, D_MODEL), lambda i: (0, 0)),
                  pl.BlockSpec((1, D_MODEL), lambda i: (0, 0)),
                  pl.BlockSpec((PROJ_TM, D_MODEL), lambda i: (i, 0))],
        out_specs=pl.BlockSpec((PROJ_TM, D_MODEL), lambda i: (i, 0)),
        out_shape=jax.ShapeDtypeStruct((n, D_MODEL), F32),
        compiler_params=_params(("arbitrary",)),
        name="attn_out_proj",
    )(a2d, w_o, g_post.reshape(1, D_MODEL), x2d)


MLP_TM = 512
MLP_FC = 1024


def _mlp_kernel(x_ref, gpre_ref, gpost_ref, wup_ref, wdn_ref, o_ref):
    x = x_ref[...]
    h = _rms(x, gpre_ref[...]).astype(BF16)
    acc = jnp.zeros((MLP_TM, D_MODEL), F32)
    for c in range(D_FF // MLP_FC):
        cols = slice(c * MLP_FC, (c + 1) * MLP_FC)
        u = jnp.dot(h, wup_ref[:, cols], preferred_element_type=F32)
        u = jnp.square(jnp.maximum(u, 0.0)).astype(BF16)
        acc = acc + jnp.dot(u, wdn_ref[cols, :], preferred_element_type=F32)
    o_ref[...] = x + _rms(acc, gpost_ref[...])


def _mlp(x2d, g_pre, g_post, w_up, w_down):
    n = x2d.shape[0]
    return pl.pallas_call(
        _mlp_kernel,
        grid=(n // MLP_TM,),
        in_specs=[pl.BlockSpec((MLP_TM, D_MODEL), lambda i: (i, 0)),
                  pl.BlockSpec((1, D_MODEL), lambda i: (0, 0)),
                  pl.BlockSpec((1, D_MODEL), lambda i: (0, 0)),
                  pl.BlockSpec((D_MODEL, D_FF), lambda i: (0, 0)),
                  pl.BlockSpec((D_FF, D_MODEL), lambda i: (0, 0))],
        out_specs=pl.BlockSpec((MLP_TM, D_MODEL), lambda i: (i, 0)),
        out_shape=jax.ShapeDtypeStruct((n, D_MODEL), F32),
        compiler_params=_params(("arbitrary",)),
        name="mlp",
    )(x2d, g_pre.reshape(1, D_MODEL), g_post.reshape(1, D_MODEL), w_up, w_down)


CONV_TS = 512
HALO = 16
CONV_ROWS = CONV_TS + 2 * HALO
CONV_RB = 64


def _conv_kernel(x_ref, gpre_ref, w1_ref, b1_ref, wdw_ref, bdw_ref, lng_ref, lnb_ref,
                 w2_ref, b2_ref, gpost_ref, o_ref, h_ref, u_ref, c_ref):
    j = pl.program_id(1)
    nj = pl.num_programs(1)
    s0 = pl.multiple_of(j * CONV_TS, CONV_TS)
    top = pl.multiple_of(jnp.maximum(s0 - HALO, 0), HALO)
    bot = pl.multiple_of(jnp.minimum(s0 + CONV_TS, SEQ - HALO), HALO)
    gpre = gpre_ref[...]

    h_ref[0:HALO, :] = _rms(x_ref[0, pl.ds(top, HALO), :], gpre).astype(BF16)
    h_ref[HALO:HALO + CONV_TS, :] = _rms(x_ref[0, pl.ds(s0, CONV_TS), :], gpre).astype(BF16)
    h_ref[HALO + CONV_TS:, :] = _rms(x_ref[0, pl.ds(bot, HALO), :], gpre).astype(BF16)

    h = h_ref[...]
    a = jnp.dot(h, w1_ref[:, :D_MODEL], preferred_element_type=F32) + b1_ref[:, :D_MODEL]
    gate = jnp.dot(h, w1_ref[:, D_MODEL:], preferred_element_type=F32) + b1_ref[:, D_MODEL:]
    u_ref[...] = a * jax.nn.sigmoid(gate)

    @pl.when(j == 0)
    def _():
        u_ref[0:HALO, :] = jnp.zeros((HALO, D_MODEL), F32)

    @pl.when(j == nj - 1)
    def _():
        u_ref[HALO + CONV_TS:, :] = jnp.zeros((HALO, D_MODEL), F32)

    def conv_block(rb, carry):
        r0 = pl.multiple_of(rb * CONV_RB, CONV_RB)
        for lc in range(D_MODEL // LANES):
            lanes = slice(lc * LANES, (lc + 1) * LANES)
            win = u_ref[pl.ds(r0, CONV_RB + 2 * HALO), lanes]
            acc = bdw_ref[:, lanes]
            for shift in range(8):
                part = None
                for base in range(0, 2 * HALO, 8):
                    t = base + shift - (HALO - CONV_WIDTH // 2)
                    if 0 <= t < CONV_WIDTH:
                        term = win[base:base + CONV_RB + 8, :] * wdw_ref[t:t + 1, lanes]
                        part = term if part is None else part + term
                acc = acc + part[shift:shift + CONV_RB, :]
            c_ref[pl.ds(r0, CONV_RB), lanes] = acc
        return carry

    lax.fori_loop(0, CONV_TS // CONV_RB, conv_block, 0)

    v = c_ref[...]
    mu = jnp.mean(v, axis=-1, keepdims=True)
    var = jnp.mean(jnp.square(v - mu), axis=-1, keepdims=True)
    y = (v - mu) * lax.rsqrt(var + LN_EPS) * lng_ref[...] + lnb_ref[...]
    y = y * jax.nn.sigmoid(y)
    z = jnp.dot(y.astype(BF16), w2_ref[...], preferred_element_type=F32) + b2_ref[...]
    o_ref[0] = x_ref[0, pl.ds(s0, CONV_TS), :] + _rms(z, gpost_ref[...])


def _conv_layer(x, g_pre, w1, b1, wdw, bdw, lng, lnb, w2, b2, g_post):
    row = lambda v: v.reshape(1, -1)
    const = lambda shape: pl.BlockSpec(shape, lambda b, j: (0, 0))
    return pl.pallas_call(
        _conv_kernel,
        grid=(BATCH, SEQ // CONV_TS),
        in_specs=[pl.BlockSpec((1, SEQ, D_MODEL), lambda b, j: (b, 0, 0)),
                  const((1, D_MODEL)),
                  const((D_MODEL, 2 * D_MODEL)),
                  const((1, 2 * D_MODEL)),
                  const((CONV_WIDTH, D_MODEL)),
                  const((1, D_MODEL)),
                  const((1, D_MODEL)),
                  const((1, D_MODEL)),
                  const((D_MODEL, D_MODEL)),
                  const((1, D_MODEL)),
                  const((1, D_MODEL))],
        out_specs=pl.BlockSpec((1, CONV_TS, D_MODEL), lambda b, j: (b, j, 0)),
        out_shape=jax.ShapeDtypeStruct((BATCH, SEQ, D_MODEL), F32),
        scratch_shapes=[pltpu.VMEM((CONV_ROWS, D_MODEL), BF16),
                        pltpu.VMEM((CONV_ROWS, D_MODEL), F32),
                        pltpu.VMEM((CONV_TS, D_MODEL), F32)],
        compiler_params=_params(("arbitrary", "arbitrary")),
        name="conformer_conv",
    )(x, row(g_pre), w1, row(b1), wdw, row(bdw), row(lng), row(lnb), w2, row(b2), row(g_post))


def kernel(x, rel_bias, norm_mix_pre, norm_mix_post, norm_mlp_pre, norm_mlp_post, attn_w_qkv,
           attn_w_o, conv_w_pw1, conv_b_pw1, conv_w_dw, conv_b_dw, conv_ln_g, conv_ln_b,
           conv_w_pw2, conv_b_pw2, mlp_w_up, mlp_w_down):
    n_tok = BATCH * SEQ
    x2d = x.reshape(n_tok, D_MODEL)

    h = _prenorm(x, norm_mix_pre[0])
    w_qkv = attn_w_qkv.reshape(D_MODEL, N_GROUPS * GROUP_WIDTH)
    qkv = [_qkv_proj(h[g], w_qkv, g) for g in range(N_GROUPS)]
    a = _attention(qkv, rel_bias)
    x2d = _out_proj(a.reshape(n_tok, D_MODEL), attn_w_o[0].astype(BF16), norm_mix_post[0], x2d)
    x2d = _mlp(x2d, norm_mlp_pre[0], norm_mlp_post[0],
               mlp_w_up[0].astype(BF16), mlp_w_down[0].astype(BF16))

    x3d = _conv_layer(x2d.reshape(BATCH, SEQ, D_MODEL), norm_mix_pre[1],
                      conv_w_pw1[0].astype(BF16), conv_b_pw1[0], conv_w_dw[0], conv_b_dw[0],
                      conv_ln_g[0], conv_ln_b[0], conv_w_pw2[0].astype(BF16), conv_b_pw2[0],
                      norm_mix_post[1])
    x2d = _mlp(x3d.reshape(n_tok, D_MODEL), norm_mlp_pre[1], norm_mlp_post[1],
               mlp_w_up[1].astype(BF16), mlp_w_down[1].astype(BF16))
    return x2d.reshape(BATCH, SEQ, D_MODEL)
```

```python
import functools
import math

import jax
import jax.numpy as jnp
from jax import lax
from jax.experimental import pallas as pl
from jax.experimental.pallas import tpu as pltpu

D_MODEL = 1024
BATCH = 8
SEQ = 2048
HEAD_DIM = 64
N_HEADS = 16
DILATIONS = (1, 4, 16)
N_SIDE = 64
N_GROUPS = 3
GROUP_WIDTH = 3 * N_HEADS * HEAD_DIM
N_BUCKETS = 32
MAX_DISTANCE = 1024
CONV_WIDTH = 31
D_FF = 4 * D_MODEL
RMS_EPS = 1e-6
LN_EPS = 1e-5
NEG_INF = -1e30

F32 = jnp.float32
BF16 = jnp.bfloat16

LANES = 128
Q_TILE = 128
K_TILE = Q_TILE + 2 * N_SIDE
VMEM_LIMIT = 56 * 1024 * 1024


def _rms(x, g):
    return x * lax.rsqrt(jnp.mean(x * x, axis=-1, keepdims=True) + RMS_EPS) * g


def _params(semantics):
    return pltpu.CompilerParams(dimension_semantics=semantics, vmem_limit_bytes=VMEM_LIMIT)


NORM_ROWS = 256


def _prenorm_kernel(x_ref, g_ref, *refs):
    out_refs, slab_ref = refs[:-1], refs[-1]

    def chunk(i, carry):
        rows = pl.ds(pl.multiple_of(i * NORM_ROWS, NORM_ROWS), NORM_ROWS)
        hn = _rms(x_ref[0, rows, :], g_ref[...])
        for k in range(D_MODEL // LANES):
            slab_ref[k, rows, :] = hn[:, k * LANES:(k + 1) * LANES]
        for out_ref, r in zip(out_refs, DILATIONS):
            if r == 1:
                out_ref[0, rows, :] = hn.astype(BF16)
        return carry

    lax.fori_loop(0, SEQ // NORM_ROWS, chunk, 0)
    for out_ref, r in zip(out_refs, DILATIONS):
        if r == 1:
            continue
        L = SEQ // r
        for c in range(r):
            for k in range(D_MODEL // LANES):
                out_ref[0, c * L:(c + 1) * L, k * LANES:(k + 1) * LANES] = (
                    slab_ref[k, pl.ds(c, L, stride=r), :].astype(BF16))


def _prenorm(x, g):
    blk = pl.BlockSpec((1, SEQ, D_MODEL), lambda b: (b, 0, 0))
    return pl.pallas_call(
        _prenorm_kernel,
        grid=(BATCH,),
        in_specs=[blk, pl.BlockSpec((1, D_MODEL), lambda b: (0, 0))],
        out_specs=[blk] * N_GROUPS,
        out_shape=[jax.ShapeDtypeStruct((BATCH, SEQ, D_MODEL), BF16)] * N_GROUPS,
        scratch_shapes=[pltpu.VMEM((D_MODEL // LANES, SEQ, LANES), F32)],
        compiler_params=_params(("arbitrary",)),
        name="prenorm",
    )(x, g.reshape(1, D_MODEL))


QKV_TN = 1024
QKV_TM = 512


def _qkv_kernel(h_ref, w_ref, o_ref, wb_ref):
    @pl.when(pl.program_id(1) == 0)
    def _():
        wb_ref[...] = w_ref[...].astype(BF16)

    for m in range(SEQ // QKV_TM):
        rows = slice(m * QKV_TM, (m + 1) * QKV_TM)
        o_ref[0, rows, :] = jnp.dot(h_ref[0, rows, :], wb_ref[...],
                                    preferred_element_type=F32).astype(BF16)


def _qkv_proj(h, w_qkv, g):
    nj = GROUP_WIDTH // QKV_TN
    return pl.pallas_call(
        _qkv_kernel,
        grid=(nj, BATCH),
        in_specs=[pl.BlockSpec((1, SEQ, D_MODEL), lambda j, b: (b, 0, 0)),
                  pl.BlockSpec((D_MODEL, QKV_TN), lambda j, b: (0, g * nj + j))],
        out_specs=pl.BlockSpec((1, SEQ, QKV_TN), lambda j, b: (b, 0, j)),
        out_shape=jax.ShapeDtypeStruct((BATCH, SEQ, GROUP_WIDTH), BF16),
        scratch_shapes=[pltpu.VMEM((D_MODEL, QKV_TN), BF16)],
        compiler_params=_params(("arbitrary", "arbitrary")),
        name=f"qkv_proj_g{g}",
    )(h, w_qkv)


HEADS_PER_STEP = 4
PAIRS_PER_STEP = HEADS_PER_STEP // 2
STEP_LANES = HEADS_PER_STEP * HEAD_DIM
N_Q_TILES = SEQ // Q_TILE
N_KINDS = 3
PADDED_PITCH = 24
GROUP_ROWS = tuple(SEQ // 16 * PADDED_PITCH if r % 16 == 0 else SEQ for r in DILATIONS)
GROUP_BASE = tuple(sum(GROUP_ROWS[:g]) for g in range(N_GROUPS))


def _bias_row(rel_ref, g, head):
    r = DILATIONS[g]
    nb = N_BUCKETS // 2
    max_exact = nb // 2
    delta = lax.broadcasted_iota(jnp.int32, (8, K_TILE), 1) - N_SIDE
    rel = delta * r
    n = jnp.abs(rel)
    nf = jnp.maximum(n, 1).astype(F32)
    large = max_exact + (jnp.log(nf / max_exact) / math.log(MAX_DISTANCE / max_exact)
                         * (nb - max_exact)).astype(jnp.int32)
    large = jnp.minimum(large, nb - 1)
    bucket = jnp.where(rel > 0, nb, 0) + jnp.where(n < max_exact, n, large)
    u = jnp.zeros((8, K_TILE), F32)
    for b in range(N_BUCKETS):
        u = jnp.where(bucket == b, rel_ref[b, g * N_HEADS + head], u)
    return jnp.where(jnp.abs(delta) <= N_SIDE, u, NEG_INF)


def _build_bias_tiles(rel_ref, bias_ref, head0):
    qi = lax.broadcasted_iota(jnp.int32, (Q_TILE, K_TILE), 0)
    kj = lax.broadcasted_iota(jnp.int32, (Q_TILE, K_TILE), 1)
    for g in range(N_GROUPS):
        for h in range(HEADS_PER_STEP):
            u = _bias_row(rel_ref, g, head0 + h)
            ub = jnp.broadcast_to(u[0:1, :], (Q_TILE, K_TILE))
            for kind in range(N_KINDS):
                shift = (N_SIDE * kind - N_SIDE) % K_TILE
                t = pltpu.roll(ub, shift, 1, stride=1, stride_axis=0)
                in_band = jnp.abs(kj - qi - N_SIDE * kind) <= N_SIDE
                bias_ref[g, h, kind] = jnp.where(in_band, t, NEG_INF)


def _token_rows(g, c, l0):
    r = DILATIONS[g]
    if r == 1:
        return pl.ds(GROUP_BASE[g] + l0, Q_TILE)
    pitch = PADDED_PITCH if r % 16 == 0 else r
    return pl.ds(GROUP_BASE[g] + l0 * pitch + c, Q_TILE, stride=pitch)


def _load_tokens(ref, pair, g, t):
    r = DILATIONS[g]
    if r % 16:
        return ref[pair, GROUP_BASE[g] + t * Q_TILE:GROUP_BASE[g] + (t + 1) * Q_TILE, :]
    n = Q_TILE // r
    base = GROUP_BASE[g] + t * n * PADDED_PITCH
    return jnp.concatenate([ref[pair, base + i * PADDED_PITCH:base + i * PADDED_PITCH + r, :]
                            for i in range(n)], axis=0)


def _split_heads(q):
    first = lax.broadcasted_iota(jnp.int32, (1, LANES), 1) < HEAD_DIM
    scale = HEAD_DIM ** -0.5
    m0 = jnp.where(first, scale, 0.0).astype(BF16)
    m1 = jnp.where(first, 0.0, scale).astype(BF16)
    return jnp.concatenate([q * m0, q * m1], axis=0)


def _softmax_pv(s, v):
    first = lax.broadcasted_iota(jnp.int32, (1, LANES), 1) < HEAD_DIM
    m = jnp.max(s, axis=-1, keepdims=True)
    p = jnp.exp(s - m).astype(BF16)
    v1 = jnp.concatenate([v, jnp.ones((v.shape[0], LANES), BF16)], axis=1)
    o2 = jnp.dot(p, v1, preferred_element_type=F32)
    acc = jnp.where(first, o2[:Q_TILE, :LANES], o2[Q_TILE:, :LANES])
    l = jnp.where(first, o2[:Q_TILE, LANES:], o2[Q_TILE:, LANES:])
    m_t = jnp.where(first, m[:Q_TILE], m[Q_TILE:])
    return acc, m_t, l


_NT = (((1,), (1,)), ((), ()))


def _attn_kernel(rel_ref, q0_ref, k0_ref, v0_ref, q1_ref, k1_ref, v1_ref, q2_ref, k2_ref, v2_ref,
                 o_ref, bias_ref, acc_ref, m_ref, l_ref):
    hp = pl.program_id(0)

    @pl.when(pl.program_id(1) == 0)
    def _():
        _build_bias_tiles(rel_ref, bias_ref, hp * HEADS_PER_STEP)

    qkv = ((q0_ref, k0_ref, v0_ref), (q1_ref, k1_ref, v1_ref), (q2_ref, k2_ref, v2_ref))
    for pair in range(PAIRS_PER_STEP):
        lanes = slice(pair * LANES, (pair + 1) * LANES)
        for g in range(N_GROUPS):
            r = DILATIONS[g]
            tiles_per_seq = SEQ // r // Q_TILE
            q_ref, k_ref, v_ref = qkv[g]

            def put(t, res, g=g, tiles_per_seq=tiles_per_seq, pair=pair):
                rows = _token_rows(g, t // tiles_per_seq, (t % tiles_per_seq) * Q_TILE)
                for ref, val in zip((acc_ref, m_ref, l_ref), res):
                    ref[pair, rows, :] = val

            if tiles_per_seq > 1:
                for t in range(N_Q_TILES):
                    tl = t % tiles_per_seq
                    kind = 0 if tl == 0 else (2 if tl == tiles_per_seq - 1 else 1)
                    q0 = t * Q_TILE
                    k0 = q0 - N_SIDE * kind
                    q2 = _split_heads(q_ref[0, q0:q0 + Q_TILE, lanes])
                    s = lax.dot_general(q2, k_ref[0, k0:k0 + K_TILE, lanes], _NT,
                                        preferred_element_type=F32)
                    s = s + jnp.concatenate([bias_ref[g, 2 * pair, kind],
                                             bias_ref[g, 2 * pair + 1, kind]], axis=0)
                    put(t, _softmax_pv(s, v_ref[0, k0:k0 + K_TILE, lanes]))
            else:
                bias = jnp.concatenate([bias_ref[g, 2 * pair, 0, :, :Q_TILE],
                                        bias_ref[g, 2 * pair + 1, 0, :, :Q_TILE]], axis=0)
                for t in range(0, N_Q_TILES, 2):
                    q0 = t * Q_TILE
                    q = q_ref[0, q0:q0 + 2 * Q_TILE, lanes]
                    q2 = jnp.concatenate([_split_heads(q[:Q_TILE]), _split_heads(q[Q_TILE:])], axis=0)
                    s = lax.dot_general(q2, k_ref[0, q0:q0 + 2 * Q_TILE, lanes], _NT,
                                        preferred_element_type=F32)
                    for i in range(2):
                        si = s[2 * i * Q_TILE:2 * (i + 1) * Q_TILE, i * Q_TILE:(i + 1) * Q_TILE]
                        v = v_ref[0, q0 + i * Q_TILE:q0 + (i + 1) * Q_TILE, lanes]
                        put(t + i, _softmax_pv(si + bias, v))

    for pair in range(PAIRS_PER_STEP):
        lanes = slice(pair * LANES, (pair + 1) * LANES)
        for t in range(N_Q_TILES):
            m = [_load_tokens(m_ref, pair, g, t) for g in range(N_GROUPS)]
            top = jnp.maximum(jnp.maximum(m[0], m[1]), m[2])
            w = [jnp.exp(x - top) for x in m]
            num = sum(w[g] * _load_tokens(acc_ref, pair, g, t) for g in range(N_GROUPS))
            den = sum(w[g] * _load_tokens(l_ref, pair, g, t) for g in range(N_GROUPS))
            o_ref[0, t * Q_TILE:(t + 1) * Q_TILE, lanes] = (num / den).astype(BF16)


def _attention(qkv, rel_bias):
    n_hp = N_HEADS // HEADS_PER_STEP
    blk = (1, SEQ, STEP_LANES)
    specs = [pl.BlockSpec(memory_space=pltpu.SMEM)]
    args = [rel_bias]
    for g in range(N_GROUPS):
        for part in range(3):
            specs.append(pl.BlockSpec(blk, lambda hp, b, part=part: (b, 0, part * n_hp + hp)))
            args.append(qkv[g])
    return pl.pallas_call(
        _attn_kernel,
        grid=(n_hp, BATCH),
        in_specs=specs,
        out_specs=pl.BlockSpec(blk, lambda hp, b: (b, 0, hp)),
        out_shape=jax.ShapeDtypeStruct((BATCH, SEQ, D_MODEL), BF16),
        scratch_shapes=[pltpu.VMEM((N_GROUPS, HEADS_PER_STEP, N_KINDS, Q_TILE, K_TILE), F32),
                        pltpu.VMEM((PAIRS_PER_STEP, sum(GROUP_ROWS), LANES), F32),
                        pltpu.VMEM((PAIRS_PER_STEP, sum(GROUP_ROWS), LANES), F32),
                        pltpu.VMEM((PAIRS_PER_STEP, sum(GROUP_ROWS), LANES), F32)],
        compiler_params=_params(("arbitrary", "arbitrary")),
        name="dilated_attention",
    )(*args)


PROJ_TM = 1024


def _oproj_kernel(a_ref, w_ref, g_ref, x_ref, o_ref):
    m = jnp.dot(a_ref[...], w_ref[...], preferred_element_type=F32)
    o_ref[...] = x_ref[...] + _rms(m, g_ref[...])


def _out_proj(a2d, w_o, g_post, x2d):
    n = x2d.shape[0]
    return pl.pallas_call(
        _oproj_kernel,
        grid=(n // PROJ_TM,),
        in_specs=[pl.BlockSpec((PROJ_TM, D_MODEL), lambda i: (i, 0)),
                  pl.BlockSpec((D_MODEL, D_MODEL), lambda i: (0, 0)),
                  pl.BlockSpec((1, D_MODEL), lambda i: (0, 0)),
                  pl.BlockSpec((PROJ_TM, D_MODEL), lambda i: (i, 0))],
        out_specs=pl.BlockSpec((PROJ_TM, D_MODEL), lambda i: (i, 0)),
        out_shape=jax.ShapeDtypeStruct((n, D_MODEL), F32),
        compiler_params=_params(("arbitrary",)),
        name="attn_out_proj",
    )(a2d, w_o, g_post.reshape(1, D_MODEL), x2d)


MLP_TM = 512
MLP_FC = 1024


def _mlp_kernel(x_ref, gpre_ref, gpost_ref, wup_ref, wdn_ref, o_ref):
    x = x_ref[...]
    h = _rms(x, gpre_ref[...]).astype(BF16)
    acc = jnp.zeros((MLP_TM, D_MODEL), F32)
    for c in range(D_FF // MLP_FC):
        cols = slice(c * MLP_FC, (c + 1) * MLP_FC)
        u = jnp.dot(h, wup_ref[:, cols], preferred_element_type=F32)
        u = jnp.square(jnp.maximum(u, 0.0)).astype(BF16)
        acc = acc + jnp.dot(u, wdn_ref[cols, :], preferred_element_type=F32)
    o_ref[...] = x + _rms(acc, gpost_ref[...])


def _mlp(x2d, g_pre, g_post, w_up, w_down):
    n = x2d.shape[0]
    return pl.pallas_call(
        _mlp_kernel,
        grid=(n // MLP_TM,),
        in_specs=[pl.BlockSpec((MLP_TM, D_MODEL), lambda i: (i, 0)),
                  pl.BlockSpec((1, D_MODEL), lambda i: (0, 0)),
                  pl.BlockSpec((1, D_MODEL), lambda i: (0, 0)),
                  pl.BlockSpec((D_MODEL, D_FF), lambda i: (0, 0)),
                  pl.BlockSpec((D_FF, D_MODEL), lambda i: (0, 0))],
        out_specs=pl.BlockSpec((MLP_TM, D_MODEL), lambda i: (i, 0)),
        out_shape=jax.ShapeDtypeStruct((n, D_MODEL), F32),
        compiler_params=_params(("arbitrary",)),
        name="mlp",
    )(x2d, g_pre.reshape(1, D_MODEL), g_post.reshape(1, D_MODEL), w_up, w_down)


CONV_TS = 512
HALO = 16
CONV_ROWS = CONV_TS + 2 * HALO
CONV_RB = 64
CONV_NC = 256


def _conv_kernel(x_ref, gpre_ref, w1_ref, b1_ref, wdw_ref, bdw_ref, lng_ref, lnb_ref,
                 w2_ref, b2_ref, gpost_ref, o_ref, h_ref, u_ref, c_ref):
    j = pl.program_id(1)
    nj = pl.num_programs(1)
    s0 = pl.multiple_of(j * CONV_TS, CONV_TS)
    top = pl.multiple_of(jnp.maximum(s0 - HALO, 0), HALO)
    bot = pl.multiple_of(jnp.minimum(s0 + CONV_TS, SEQ - HALO), HALO)
    gpre = gpre_ref[...]

    h_ref[0:HALO, :] = _rms(x_ref[0, pl.ds(top, HALO), :], gpre).astype(BF16)
    h_ref[HALO:HALO + CONV_TS, :] = _rms(x_ref[0, pl.ds(s0, CONV_TS), :], gpre).astype(BF16)
    h_ref[HALO + CONV_TS:, :] = _rms(x_ref[0, pl.ds(bot, HALO), :], gpre).astype(BF16)

    h = h_ref[...]
    for nc in range(D_MODEL // CONV_NC):
        cols = slice(nc * CONV_NC, (nc + 1) * CONV_NC)
        gcols = slice(D_MODEL + nc * CONV_NC, D_MODEL + (nc + 1) * CONV_NC)
        a = jnp.dot(h, w1_ref[:, cols], preferred_element_type=F32) + b1_ref[:, cols]
        gate = jnp.dot(h, w1_ref[:, gcols], preferred_element_type=F32) + b1_ref[:, gcols]
        u = a * jax.nn.sigmoid(gate)
        u_ref[0:HALO, cols] = jnp.where(j > 0, u[0:HALO], 0.0)
        u_ref[HALO:HALO + CONV_TS, cols] = u[HALO:HALO + CONV_TS]
        u_ref[HALO + CONV_TS:, cols] = jnp.where(j < nj - 1, u[HALO + CONV_TS:], 0.0)
        for rb in range(CONV_TS // CONV_RB):
            r0 = rb * CONV_RB
            for lc in range(nc * CONV_NC // LANES, (nc + 1) * CONV_NC // LANES):
                lanes = slice(lc * LANES, (lc + 1) * LANES)
                win = u_ref[r0:r0 + CONV_RB + 2 * HALO, lanes]
                acc = bdw_ref[:, lanes]
                for shift in range(8):
                    part = None
                    for base in range(0, 2 * HALO, 8):
                        t = base + shift - (HALO - CONV_WIDTH // 2)
                        if 0 <= t < CONV_WIDTH:
                            term = win[base:base + CONV_RB + 8, :] * wdw_ref[t:t + 1, lanes]
                            part = term if part is None else part + term
                    acc = acc + part[shift:shift + CONV_RB, :]
                c_ref[r0:r0 + CONV_RB, lanes] = acc

    v = c_ref[...]
    mu = jnp.mean(v, axis=-1, keepdims=True)
    var = jnp.mean(jnp.square(v - mu), axis=-1, keepdims=True)
    y = (v - mu) * lax.rsqrt(var + LN_EPS) * lng_ref[...] + lnb_ref[...]
    y = y * jax.nn.sigmoid(y)
    z = jnp.dot(y.astype(BF16), w2_ref[...], preferred_element_type=F32) + b2_ref[...]
    o_ref[0] = x_ref[0, pl.ds(s0, CONV_TS), :] + _rms(z, gpost_ref[...])


def _conv_layer(x, g_pre, w1, b1, wdw, bdw, lng, lnb, w2, b2, g_post):
    row = lambda v: v.reshape(1, -1)
    const = lambda shape: pl.BlockSpec(shape, lambda b, j: (0, 0))
    return pl.pallas_call(
        _conv_kernel,
        grid=(BATCH, SEQ // CONV_TS),
        in_specs=[pl.BlockSpec((1, SEQ, D_MODEL), lambda b, j: (b, 0, 0)),
                  const((1, D_MODEL)),
                  const((D_MODEL, 2 * D_MODEL)),
                  const((1, 2 * D_MODEL)),
                  const((CONV_WIDTH, D_MODEL)),
                  const((1, D_MODEL)),
                  const((1, D_MODEL)),
                  const((1, D_MODEL)),
                  const((D_MODEL, D_MODEL)),
                  const((1, D_MODEL)),
                  const((1, D_MODEL))],
        out_specs=pl.BlockSpec((1, CONV_TS, D_MODEL), lambda b, j: (b, j, 0)),
        out_shape=jax.ShapeDtypeStruct((BATCH, SEQ, D_MODEL), F32),
        scratch_shapes=[pltpu.VMEM((CONV_ROWS, D_MODEL), BF16),
                        pltpu.VMEM((CONV_ROWS, D_MODEL), F32),
                        pltpu.VMEM((CONV_TS, D_MODEL), F32)],
        compiler_params=_params(("arbitrary", "arbitrary")),
        name="conformer_conv",
    )(x, row(g_pre), w1, row(b1), wdw, row(bdw), row(lng), row(lnb), w2, row(b2), row(g_post))


def kernel(x, rel_bias, norm_mix_pre, norm_mix_post, norm_mlp_pre, norm_mlp_post, attn_w_qkv,
           attn_w_o, conv_w_pw1, conv_b_pw1, conv_w_dw, conv_b_dw, conv_ln_g, conv_ln_b,
           conv_w_pw2, conv_b_pw2, mlp_w_up, mlp_w_down):
    n_tok = BATCH * SEQ
    x2d = x.reshape(n_tok, D_MODEL)

    h = _prenorm(x, norm_mix_pre[0])
    w_qkv = attn_w_qkv.reshape(D_MODEL, N_GROUPS * GROUP_WIDTH)
    qkv = [_qkv_proj(h[g], w_qkv, g) for g in range(N_GROUPS)]
    a = _attention(qkv, rel_bias)
    x2d = _out_proj(a.reshape(n_tok, D_MODEL), attn_w_o[0].astype(BF16), norm_mix_post[0], x2d)
    x2d = _mlp(x2d, norm_mlp_pre[0], norm_mlp_post[0],
               mlp_w_up[0].astype(BF16), mlp_w_down[0].astype(BF16))

    x3d = _conv_layer(x2d.reshape(BATCH, SEQ, D_MODEL), norm_mix_pre[1],
                      conv_w_pw1[0].astype(BF16), conv_b_pw1[0], conv_w_dw[0], conv_b_dw[0],
                      conv_ln_g[0], conv_ln_b[0], conv_w_pw2[0].astype(BF16), conv_b_pw2[0],
                      norm_mix_post[1])
    x2d = _mlp(x3d.reshape(n_tok, D_MODEL), norm_mlp_pre[1], norm_mlp_post[1],
               mlp_w_up[1].astype(BF16), mlp_w_down[1].astype(BF16))
    return x2d.reshape(BATCH, SEQ, D_MODEL)
```

```python
import functools
import math

import jax
import jax.numpy as jnp
from jax import lax
from jax.experimental import pallas as pl
from jax.experimental.pallas import tpu as pltpu

D_MODEL = 1024
BATCH = 8
SEQ = 2048
HEAD_DIM = 64
N_HEADS = 16
DILATIONS = (1, 4, 16)
N_SIDE = 64
N_GROUPS = 3
GROUP_WIDTH = 3 * N_HEADS * HEAD_DIM
N_BUCKETS = 32
MAX_DISTANCE = 1024
CONV_WIDTH = 31
D_FF = 4 * D_MODEL
RMS_EPS = 1e-6
LN_EPS = 1e-5
NEG_INF = -1e30

F32 = jnp.float32
BF16 = jnp.bfloat16

LANES = 128
Q_TILE = 128
K_TILE = Q_TILE + 2 * N_SIDE
VMEM_LIMIT = 56 * 1024 * 1024


def _rms(x, g):
    return x * lax.rsqrt(jnp.mean(x * x, axis=-1, keepdims=True) + RMS_EPS) * g


def _params(semantics):
    return pltpu.CompilerParams(dimension_semantics=semantics, vmem_limit_bytes=VMEM_LIMIT)


NORM_ROWS = 256


def _prenorm_kernel(x_ref, g_ref, *refs):
    out_refs, slab_ref = refs[:-1], refs[-1]

    def chunk(i, carry):
        rows = pl.ds(pl.multiple_of(i * NORM_ROWS, NORM_ROWS), NORM_ROWS)
        hn = _rms(x_ref[0, rows, :], g_ref[...])
        for k in range(D_MODEL // LANES):
            slab_ref[k, rows, :] = hn[:, k * LANES:(k + 1) * LANES]
        for out_ref, r in zip(out_refs, DILATIONS):
            if r == 1:
                out_ref[0, rows, :] = hn.astype(BF16)
        return carry

    lax.fori_loop(0, SEQ // NORM_ROWS, chunk, 0)
    for out_ref, r in zip(out_refs, DILATIONS):
        if r == 1:
            continue
        L = SEQ // r
        for c in range(r):
            for k in range(D_MODEL // LANES):
                out_ref[0, c * L:(c + 1) * L, k * LANES:(k + 1) * LANES] = (
                    slab_ref[k, pl.ds(c, L, stride=r), :].astype(BF16))


def _prenorm(x, g):
    blk = pl.BlockSpec((1, SEQ, D_MODEL), lambda b: (b, 0, 0))
    return pl.pallas_call(
        _prenorm_kernel,
        grid=(BATCH,),
        in_specs=[blk, pl.BlockSpec((1, D_MODEL), lambda b: (0, 0))],
        out_specs=[blk] * N_GROUPS,
        out_shape=[jax.ShapeDtypeStruct((BATCH, SEQ, D_MODEL), BF16)] * N_GROUPS,
        scratch_shapes=[pltpu.VMEM((D_MODEL // LANES, SEQ, LANES), F32)],
        compiler_params=_params(("arbitrary",)),
        name="prenorm",
    )(x, g.reshape(1, D_MODEL))


QKV_TN = 1024
QKV_TM = 512


def _qkv_kernel(h_ref, w_ref, o_ref, wb_ref):
    @pl.when(pl.program_id(1) == 0)
    def _():
        wb_ref[...] = w_ref[...].astype(BF16)

    for m in range(SEQ // QKV_TM):
        rows = slice(m * QKV_TM, (m + 1) * QKV_TM)
        o_ref[0, rows, :] = jnp.dot(h_ref[0, rows, :], wb_ref[...],
                                    preferred_element_type=F32).astype(BF16)


def _qkv_proj(h, w_qkv, g):
    nj = GROUP_WIDTH // QKV_TN
    return pl.pallas_call(
        _qkv_kernel,
        grid=(nj, BATCH),
        in_specs=[pl.BlockSpec((1, SEQ, D_MODEL), lambda j, b: (b, 0, 0)),
                  pl.BlockSpec((D_MODEL, QKV_TN), lambda j, b: (0, g * nj + j))],
        out_specs=pl.BlockSpec((1, SEQ, QKV_TN), lambda j, b: (b, 0, j)),
        out_shape=jax.ShapeDtypeStruct((BATCH, SEQ, GROUP_WIDTH), BF16),
        scratch_shapes=[pltpu.VMEM((D_MODEL, QKV_TN), BF16)],
        compiler_params=_params(("arbitrary", "arbitrary")),
        name=f"qkv_proj_g{g}",
    )(h, w_qkv)


HEADS_PER_STEP = 4
PAIRS_PER_STEP = HEADS_PER_STEP // 2
STEP_LANES = HEADS_PER_STEP * HEAD_DIM
N_Q_TILES = SEQ // Q_TILE
N_KINDS = 3
PADDED_PITCH = 24
GROUP_ROWS = tuple(SEQ // 16 * PADDED_PITCH if r % 16 == 0 else SEQ for r in DILATIONS)
GROUP_BASE = tuple(sum(GROUP_ROWS[:g]) for g in range(N_GROUPS))


def _bias_row(rel_ref, g, head):
    r = DILATIONS[g]
    nb = N_BUCKETS // 2
    max_exact = nb // 2
    delta = lax.broadcasted_iota(jnp.int32, (8, K_TILE), 1) - N_SIDE
    rel = delta * r
    n = jnp.abs(rel)
    nf = jnp.maximum(n, 1).astype(F32)
    large = max_exact + (jnp.log(nf / max_exact) / math.log(MAX_DISTANCE / max_exact)
                         * (nb - max_exact)).astype(jnp.int32)
    large = jnp.minimum(large, nb - 1)
    bucket = jnp.where(rel > 0, nb, 0) + jnp.where(n < max_exact, n, large)
    u = jnp.zeros((8, K_TILE), F32)
    for b in range(N_BUCKETS):
        u = jnp.where(bucket == b, rel_ref[b, g * N_HEADS + head], u)
    return jnp.where(jnp.abs(delta) <= N_SIDE, u, NEG_INF)


def _build_bias_tiles(rel_ref, bias_ref, head0):
    qi = lax.broadcasted_iota(jnp.int32, (Q_TILE, K_TILE), 0)
    kj = lax.broadcasted_iota(jnp.int32, (Q_TILE, K_TILE), 1)
    for g in range(N_GROUPS):
        for h in range(HEADS_PER_STEP):
            u = _bias_row(rel_ref, g, head0 + h)
            ub = jnp.broadcast_to(u[0:1, :], (Q_TILE, K_TILE))
            for kind in range(N_KINDS):
                shift = (N_SIDE * kind - N_SIDE) % K_TILE
                t = pltpu.roll(ub, shift, 1, stride=1, stride_axis=0)
                in_band = jnp.abs(kj - qi - N_SIDE * kind) <= N_SIDE
                bias_ref[g, h, kind] = jnp.where(in_band, t, NEG_INF)


def _token_rows(g, c, l0):
    r = DILATIONS[g]
    if r == 1:
        return pl.ds(GROUP_BASE[g] + l0, Q_TILE)
    pitch = PADDED_PITCH if r % 16 == 0 else r
    return pl.ds(GROUP_BASE[g] + l0 * pitch + c, Q_TILE, stride=pitch)


def _load_tokens(ref, g, t):
    r = DILATIONS[g]
    if r % 16:
        return ref[pl.ds(pl.multiple_of(GROUP_BASE[g] + t * Q_TILE, Q_TILE), Q_TILE), :]
    n = Q_TILE // r
    base = pl.multiple_of(GROUP_BASE[g] + t * n * PADDED_PITCH, 8)
    return jnp.concatenate([ref[pl.ds(base + i * PADDED_PITCH, r), :] for i in range(n)], axis=0)


def _split_heads(q):
    first = lax.broadcasted_iota(jnp.int32, (1, LANES), 1) < HEAD_DIM
    scale = HEAD_DIM ** -0.5
    m0 = jnp.where(first, scale, 0.0).astype(BF16)
    m1 = jnp.where(first, 0.0, scale).astype(BF16)
    return jnp.concatenate([q * m0, q * m1], axis=0)


def _softmax_pv(s, v):
    first = lax.broadcasted_iota(jnp.int32, (1, LANES), 1) < HEAD_DIM
    m = jnp.max(s, axis=-1, keepdims=True)
    p = jnp.exp(s - m).astype(BF16)
    v1 = jnp.concatenate([v, jnp.ones((v.shape[0], LANES), BF16)], axis=1)
    o2 = jnp.dot(p, v1, preferred_element_type=F32)
    acc = jnp.where(first, o2[:Q_TILE, :LANES], o2[Q_TILE:, :LANES])
    l = jnp.where(first, o2[:Q_TILE, LANES:], o2[Q_TILE:, LANES:])
    m_t = jnp.where(first, m[:Q_TILE], m[Q_TILE:])
    return acc, m_t, l


_NT = (((1,), (1,)), ((), ()))


def _attn_kernel(rel_ref, q0_ref, k0_ref, v0_ref, q1_ref, k1_ref, v1_ref, q2_ref, k2_ref, v2_ref,
                 o_ref, bias_ref, acc_ref, m_ref, l_ref):
    hp = pl.program_id(0)

    @pl.when(pl.program_id(1) == 0)
    def _():
        _build_bias_tiles(rel_ref, bias_ref, hp * HEADS_PER_STEP)

    qkv = ((q0_ref, k0_ref, v0_ref), (q1_ref, k1_ref, v1_ref), (q2_ref, k2_ref, v2_ref))
    for pair in range(PAIRS_PER_STEP):
        lanes = slice(pair * LANES, (pair + 1) * LANES)
        for g in range(N_GROUPS):
            r = DILATIONS[g]
            tiles_per_seq = SEQ // r // Q_TILE
            q_ref, k_ref, v_ref = qkv[g]

            def put(t, res, g=g, tiles_per_seq=tiles_per_seq):
                rows = _token_rows(g, t // tiles_per_seq, (t % tiles_per_seq) * Q_TILE)
                for ref, val in zip((acc_ref, m_ref, l_ref), res):
                    ref[rows, :] = val

            if tiles_per_seq > 1:
                for t in range(N_Q_TILES):
                    tl = t % tiles_per_seq
                    kind = 0 if tl == 0 else (2 if tl == tiles_per_seq - 1 else 1)
                    q0 = t * Q_TILE
                    k0 = q0 - N_SIDE * kind
                    q2 = _split_heads(q_ref[0, q0:q0 + Q_TILE, lanes])
                    s = lax.dot_general(q2, k_ref[0, k0:k0 + K_TILE, lanes], _NT,
                                        preferred_element_type=F32)
                    s = s + jnp.concatenate([bias_ref[g, 2 * pair, kind],
                                             bias_ref[g, 2 * pair + 1, kind]], axis=0)
                    put(t, _softmax_pv(s, v_ref[0, k0:k0 + K_TILE, lanes]))
            else:
                bias = jnp.concatenate([bias_ref[g, 2 * pair, 0, :, :Q_TILE],
                                        bias_ref[g, 2 * pair + 1, 0, :, :Q_TILE]], axis=0)
                for t in range(0, N_Q_TILES, 2):
                    q0 = t * Q_TILE
                    q = q_ref[0, q0:q0 + 2 * Q_TILE, lanes]
                    q2 = jnp.concatenate([_split_heads(q[:Q_TILE]), _split_heads(q[Q_TILE:])], axis=0)
                    s = lax.dot_general(q2, k_ref[0, q0:q0 + 2 * Q_TILE, lanes], _NT,
                                        preferred_element_type=F32)
                    for i in range(2):
                        si = s[2 * i * Q_TILE:2 * (i + 1) * Q_TILE, i * Q_TILE:(i + 1) * Q_TILE]
                        v = v_ref[0, q0 + i * Q_TILE:q0 + (i + 1) * Q_TILE, lanes]
                        put(t + i, _softmax_pv(si + bias, v))

        def merge_body(t, carry, lanes=lanes):
            m = [_load_tokens(m_ref, g, t) for g in range(N_GROUPS)]
            top = jnp.maximum(jnp.maximum(m[0], m[1]), m[2])
            w = [jnp.exp(x - top) for x in m]
            num = sum(w[g] * _load_tokens(acc_ref, g, t) for g in range(N_GROUPS))
            den = sum(w[g] * _load_tokens(l_ref, g, t) for g in range(N_GROUPS))
            rows = pl.ds(pl.multiple_of(t * Q_TILE, Q_TILE), Q_TILE)
            o_ref[0, rows, lanes] = (num / den).astype(BF16)
            return carry

        lax.fori_loop(0, N_Q_TILES, merge_body, 0)


def _attention(qkv, rel_bias):
    n_hp = N_HEADS // HEADS_PER_STEP
    blk = (1, SEQ, STEP_LANES)
    specs = [pl.BlockSpec(memory_space=pltpu.SMEM)]
    args = [rel_bias]
    for g in range(N_GROUPS):
        for part in range(3):
            specs.append(pl.BlockSpec(blk, lambda hp, b, part=part: (b, 0, part * n_hp + hp)))
            args.append(qkv[g])
    return pl.pallas_call(
        _attn_kernel,
        grid=(n_hp, BATCH),
        in_specs=specs,
        out_specs=pl.BlockSpec(blk, lambda hp, b: (b, 0, hp)),
        out_shape=jax.ShapeDtypeStruct((BATCH, SEQ, D_MODEL), BF16),
        scratch_shapes=[pltpu.VMEM((N_GROUPS, HEADS_PER_STEP, N_KINDS, Q_TILE, K_TILE), F32),
                        pltpu.VMEM((sum(GROUP_ROWS), LANES), F32),
                        pltpu.VMEM((sum(GROUP_ROWS), LANES), F32),
                        pltpu.VMEM((sum(GROUP_ROWS), LANES), F32)],
        compiler_params=_params(("arbitrary", "arbitrary")),
        name="dilated_attention",
    )(*args)


MLP_TM = 512
MLP_FC = 1024


def _mlp_kernel(*refs, with_proj):
    if with_proj:
        a_ref, wo_ref, gmix_ref, *refs = refs
    x_ref, gpre_ref, gpost_ref, wup_ref, wdn_ref, o_ref = refs
    x = x_ref[...]
    if with_proj:
        m = jnp.dot(a_ref[...], wo_ref[...].astype(BF16), preferred_element_type=F32)
        x = x + _rms(m, gmix_ref[...])
    h = _rms(x, gpre_ref[...]).astype(BF16)
    acc = jnp.zeros((MLP_TM, D_MODEL), F32)
    for c in range(D_FF // MLP_FC):
        cols = slice(c * MLP_FC, (c + 1) * MLP_FC)
        u = jnp.dot(h, wup_ref[:, cols].astype(BF16), preferred_element_type=F32)
        u = jnp.square(jnp.maximum(u, 0.0)).astype(BF16)
        acc = acc + jnp.dot(u, wdn_ref[cols, :].astype(BF16), preferred_element_type=F32)
    o_ref[...] = x + _rms(acc, gpost_ref[...])


def _mlp(x2d, g_pre, g_post, w_up, w_down, layer, proj=None):
    n = x2d.shape[0]
    resident = pl.Buffered(1)
    tile = pl.BlockSpec((MLP_TM, D_MODEL), lambda i: (i, 0))
    gain = pl.BlockSpec((1, D_MODEL), lambda i: (0, 0))
    specs, args = [], []
    if proj is not None:
        a2d, w_o, g_mix = proj
        specs += [tile, pl.BlockSpec((None, D_MODEL, D_MODEL), lambda i: (0, 0, 0),
                                     pipeline_mode=resident), gain]
        args += [a2d, w_o, g_mix.reshape(1, D_MODEL)]
    specs += [tile, gain, gain,
              pl.BlockSpec((None, D_MODEL, D_FF), lambda i: (layer, 0, 0), pipeline_mode=resident),
              pl.BlockSpec((None, D_FF, D_MODEL), lambda i: (layer, 0, 0), pipeline_mode=resident)]
    args += [x2d, g_pre.reshape(1, D_MODEL), g_post.reshape(1, D_MODEL), w_up, w_down]
    return pl.pallas_call(
        functools.partial(_mlp_kernel, with_proj=proj is not None),
        grid=(n // MLP_TM,),
        in_specs=specs,
        out_specs=tile,
        out_shape=jax.ShapeDtypeStruct((n, D_MODEL), F32),
        compiler_params=_params(("arbitrary",)),
        name="mlp",
    )(*args)


CONV_TS = 512
HALO = 16
CONV_ROWS = CONV_TS + 2 * HALO
CONV_RB = 64
CONV_NC = 256


def _conv_kernel(x_ref, gpre_ref, w1_ref, b1_ref, wdw_ref, bdw_ref, lng_ref, lnb_ref,
                 w2_ref, b2_ref, gpost_ref, o_ref, h_ref, u_ref, c_ref):
    j = pl.program_id(1)
    nj = pl.num_programs(1)
    s0 = pl.multiple_of(j * CONV_TS, CONV_TS)
    top = pl.multiple_of(jnp.maximum(s0 - HALO, 0), HALO)
    bot = pl.multiple_of(jnp.minimum(s0 + CONV_TS, SEQ - HALO), HALO)
    gpre = gpre_ref[...]

    h_ref[0:HALO, :] = _rms(x_ref[0, pl.ds(top, HALO), :], gpre).astype(BF16)
    h_ref[HALO:HALO + CONV_TS, :] = _rms(x_ref[0, pl.ds(s0, CONV_TS), :], gpre).astype(BF16)
    h_ref[HALO + CONV_TS:, :] = _rms(x_ref[0, pl.ds(bot, HALO), :], gpre).astype(BF16)

    h = h_ref[...]
    for nc in range(D_MODEL // CONV_NC):
        cols = slice(nc * CONV_NC, (nc + 1) * CONV_NC)
        gcols = slice(D_MODEL + nc * CONV_NC, D_MODEL + (nc + 1) * CONV_NC)
        a = jnp.dot(h, w1_ref[:, cols], preferred_element_type=F32) + b1_ref[:, cols]
        gate = jnp.dot(h, w1_ref[:, gcols], preferred_element_type=F32) + b1_ref[:, gcols]
        u = a * jax.nn.sigmoid(gate)
        u_ref[0:HALO, cols] = jnp.where(j > 0, u[0:HALO], 0.0)
        u_ref[HALO:HALO + CONV_TS, cols] = u[HALO:HALO + CONV_TS]
        u_ref[HALO + CONV_TS:, cols] = jnp.where(j < nj - 1, u[HALO + CONV_TS:], 0.0)
        for rb in range(CONV_TS // CONV_RB):
            r0 = rb * CONV_RB
            for lc in range(nc * CONV_NC // LANES, (nc + 1) * CONV_NC // LANES):
                lanes = slice(lc * LANES, (lc + 1) * LANES)
                win = u_ref[r0:r0 + CONV_RB + 2 * HALO, lanes]
                acc = bdw_ref[:, lanes]
                for shift in range(8):
                    part = None
                    for base in range(0, 2 * HALO, 8):
                        t = base + shift - (HALO - CONV_WIDTH // 2)
                        if 0 <= t < CONV_WIDTH:
                            term = win[base:base + CONV_RB + 8, :] * wdw_ref[t:t + 1, lanes]
                            part = term if part is None else part + term
                    acc = acc + part[shift:shift + CONV_RB, :]
                c_ref[r0:r0 + CONV_RB, lanes] = acc

    v = c_ref[...]
    mu = jnp.mean(v, axis=-1, keepdims=True)
    var = jnp.mean(jnp.square(v - mu), axis=-1, keepdims=True)
    y = (v - mu) * lax.rsqrt(var + LN_EPS) * lng_ref[...] + lnb_ref[...]
    y = y * jax.nn.sigmoid(y)
    z = jnp.dot(y.astype(BF16), w2_ref[...], preferred_element_type=F32) + b2_ref[...]
    o_ref[0] = x_ref[0, pl.ds(s0, CONV_TS), :] + _rms(z, gpost_ref[...])


def _conv_layer(x, g_pre, w1, b1, wdw, bdw, lng, lnb, w2, b2, g_post):
    row = lambda v: v.reshape(1, -1)
    const = lambda shape: pl.BlockSpec(shape, lambda b, j: (0, 0))
    return pl.pallas_call(
        _conv_kernel,
        grid=(BATCH, SEQ // CONV_TS),
        in_specs=[pl.BlockSpec((1, SEQ, D_MODEL), lambda b, j: (b, 0, 0)),
                  const((1, D_MODEL)),
                  const((D_MODEL, 2 * D_MODEL)),
                  const((1, 2 * D_MODEL)),
                  const((CONV_WIDTH, D_MODEL)),
                  const((1, D_MODEL)),
                  const((1, D_MODEL)),
                  const((1, D_MODEL)),
                  const((D_MODEL, D_MODEL)),
                  const((1, D_MODEL)),
                  const((1, D_MODEL))],
        out_specs=pl.BlockSpec((1, CONV_TS, D_MODEL), lambda b, j: (b, j, 0)),
        out_shape=jax.ShapeDtypeStruct((BATCH, SEQ, D_MODEL), F32),
        scratch_shapes=[pltpu.VMEM((CONV_ROWS, D_MODEL), BF16),
                        pltpu.VMEM((CONV_ROWS, D_MODEL), F32),
                        pltpu.VMEM((CONV_TS, D_MODEL), F32)],
        compiler_params=_params(("arbitrary", "arbitrary")),
        name="conformer_conv",
    )(x, row(g_pre), w1, row(b1), wdw, row(bdw), row(lng), row(lnb), w2, row(b2), row(g_post))


def kernel(x, rel_bias, norm_mix_pre, norm_mix_post, norm_mlp_pre, norm_mlp_post, attn_w_qkv,
           attn_w_o, conv_w_pw1, conv_b_pw1, conv_w_dw, conv_b_dw, conv_ln_g, conv_ln_b,
           conv_w_pw2, conv_b_pw2, mlp_w_up, mlp_w_down):
    n_tok = BATCH * SEQ
    x2d = x.reshape(n_tok, D_MODEL)

    h = _prenorm(x, norm_mix_pre[0])
    w_qkv = attn_w_qkv.reshape(D_MODEL, N_GROUPS * GROUP_WIDTH)
    qkv = [_qkv_proj(h[g], w_qkv, g) for g in range(N_GROUPS)]
    a = _attention(qkv, rel_bias)
    x2d = _mlp(x2d, norm_mlp_pre[0], norm_mlp_post[0], mlp_w_up, mlp_w_down, 0,
               proj=(a.reshape(n_tok, D_MODEL), attn_w_o, norm_mix_post[0]))

    x3d = _conv_layer(x2d.reshape(BATCH, SEQ, D_MODEL), norm_mix_pre[1],
                      conv_w_pw1[0].astype(BF16), conv_b_pw1[0], conv_w_dw[0], conv_b_dw[0],
                      conv_ln_g[0], conv_ln_b[0], conv_w_pw2[0].astype(BF16), conv_b_pw2[0],
                      norm_mix_post[1])
    x2d = _mlp(x3d.reshape(n_tok, D_MODEL), norm_mlp_pre[1], norm_mlp_post[1],
               mlp_w_up, mlp_w_down, 1)
    return x2d.reshape(BATCH, SEQ, D_MODEL)
```

```python
import functools
import math

import jax
import jax.numpy as jnp
from jax import lax
from jax.experimental import pallas as pl
from jax.experimental.pallas import tpu as pltpu

D_MODEL = 1024
BATCH = 8
SEQ = 2048
HEAD_DIM = 64
N_HEADS = 16
DILATIONS = (1, 4, 16)
N_SIDE = 64
N_GROUPS = 3
GROUP_WIDTH = 3 * N_HEADS * HEAD_DIM
N_BUCKETS = 32
MAX_DISTANCE = 1024
CONV_WIDTH = 31
D_FF = 4 * D_MODEL
RMS_EPS = 1e-6
LN_EPS = 1e-5
NEG_INF = -1e30

F32 = jnp.float32
BF16 = jnp.bfloat16

LANES = 128
Q_TILE = 128
K_TILE = Q_TILE + 2 * N_SIDE
VMEM_LIMIT = 56 * 1024 * 1024


def _rms(x, g):
    return x * lax.rsqrt(jnp.mean(x * x, axis=-1, keepdims=True) + RMS_EPS) * g


def _params(semantics):
    return pltpu.CompilerParams(dimension_semantics=semantics, vmem_limit_bytes=VMEM_LIMIT)


NORM_ROWS = 256


def _prenorm_kernel(x_ref, g_ref, *refs):
    out_refs, slab_ref = refs[:-1], refs[-1]

    def chunk(i, carry):
        rows = pl.ds(pl.multiple_of(i * NORM_ROWS, NORM_ROWS), NORM_ROWS)
        hn = _rms(x_ref[0, rows, :], g_ref[...])
        for k in range(D_MODEL // LANES):
            slab_ref[k, rows, :] = hn[:, k * LANES:(k + 1) * LANES]
        for out_ref, r in zip(out_refs, DILATIONS):
            if r == 1:
                out_ref[0, rows, :] = hn.astype(BF16)
        return carry

    lax.fori_loop(0, SEQ // NORM_ROWS, chunk, 0)
    for out_ref, r in zip(out_refs, DILATIONS):
        if r == 1:
            continue
        L = SEQ // r
        for c in range(r):
            for k in range(D_MODEL // LANES):
                out_ref[0, c * L:(c + 1) * L, k * LANES:(k + 1) * LANES] = (
                    slab_ref[k, pl.ds(c, L, stride=r), :].astype(BF16))


def _prenorm(x, g):
    blk = pl.BlockSpec((1, SEQ, D_MODEL), lambda b: (b, 0, 0))
    return pl.pallas_call(
        _prenorm_kernel,
        grid=(BATCH,),
        in_specs=[blk, pl.BlockSpec((1, D_MODEL), lambda b: (0, 0))],
        out_specs=[blk] * N_GROUPS,
        out_shape=[jax.ShapeDtypeStruct((BATCH, SEQ, D_MODEL), BF16)] * N_GROUPS,
        scratch_shapes=[pltpu.VMEM((D_MODEL // LANES, SEQ, LANES), F32)],
        compiler_params=_params(("arbitrary",)),
        name="prenorm",
    )(x, g.reshape(1, D_MODEL))


QKV_TN = 1024
QKV_TM = 512


def _qkv_kernel(h_ref, w_ref, o_ref, wb_ref):
    @pl.when(pl.program_id(1) == 0)
    def _():
        wb_ref[...] = w_ref[...].astype(BF16)

    for m in range(SEQ // QKV_TM):
        rows = slice(m * QKV_TM, (m + 1) * QKV_TM)
        o_ref[0, rows, :] = jnp.dot(h_ref[0, rows, :], wb_ref[...],
                                    preferred_element_type=F32).astype(BF16)


def _qkv_proj(h, w_qkv, g):
    nj = GROUP_WIDTH // QKV_TN
    return pl.pallas_call(
        _qkv_kernel,
        grid=(nj, BATCH),
        in_specs=[pl.BlockSpec((1, SEQ, D_MODEL), lambda j, b: (b, 0, 0)),
                  pl.BlockSpec((D_MODEL, QKV_TN), lambda j, b: (0, g * nj + j))],
        out_specs=pl.BlockSpec((1, SEQ, QKV_TN), lambda j, b: (b, 0, j)),
        out_shape=jax.ShapeDtypeStruct((BATCH, SEQ, GROUP_WIDTH), BF16),
        scratch_shapes=[pltpu.VMEM((D_MODEL, QKV_TN), BF16)],
        compiler_params=_params(("arbitrary", "arbitrary")),
        name=f"qkv_proj_g{g}",
    )(h, w_qkv)


HEADS_PER_STEP = 4
PAIRS_PER_STEP = HEADS_PER_STEP // 2
STEP_LANES = HEADS_PER_STEP * HEAD_DIM
N_Q_TILES = SEQ // Q_TILE
N_KINDS = 3
PADDED_PITCH = 24
GROUP_ROWS = tuple(SEQ // 16 * PADDED_PITCH if r % 16 == 0 else SEQ for r in DILATIONS)
GROUP_BASE = tuple(sum(GROUP_ROWS[:g]) for g in range(N_GROUPS))


def _bias_row(rel_ref, g, head):
    r = DILATIONS[g]
    nb = N_BUCKETS // 2
    max_exact = nb // 2
    delta = lax.broadcasted_iota(jnp.int32, (8, K_TILE), 1) - N_SIDE
    rel = delta * r
    n = jnp.abs(rel)
    nf = jnp.maximum(n, 1).astype(F32)
    large = max_exact + (jnp.log(nf / max_exact) / math.log(MAX_DISTANCE / max_exact)
                         * (nb - max_exact)).astype(jnp.int32)
    large = jnp.minimum(large, nb - 1)
    bucket = jnp.where(rel > 0, nb, 0) + jnp.where(n < max_exact, n, large)
    u = jnp.zeros((8, K_TILE), F32)
    for b in range(N_BUCKETS):
        u = jnp.where(bucket == b, rel_ref[b, g * N_HEADS + head], u)
    return jnp.where(jnp.abs(delta) <= N_SIDE, u, NEG_INF)


def _build_bias_tiles(rel_ref, bias_ref, head0):
    qi = lax.broadcasted_iota(jnp.int32, (Q_TILE, K_TILE), 0)
    kj = lax.broadcasted_iota(jnp.int32, (Q_TILE, K_TILE), 1)
    for g in range(N_GROUPS):
        for h in range(HEADS_PER_STEP):
            u = _bias_row(rel_ref, g, head0 + h)
            ub = jnp.broadcast_to(u[0:1, :], (Q_TILE, K_TILE))
            for kind in range(N_KINDS):
                shift = (N_SIDE * kind - N_SIDE) % K_TILE
                t = pltpu.roll(ub, shift, 1, stride=1, stride_axis=0)
                in_band = jnp.abs(kj - qi - N_SIDE * kind) <= N_SIDE
                bias_ref[g, h, kind] = jnp.where(in_band, t, NEG_INF)


def _token_rows(g, c, l0):
    r = DILATIONS[g]
    if r == 1:
        return pl.ds(GROUP_BASE[g] + l0, Q_TILE)
    pitch = PADDED_PITCH if r % 16 == 0 else r
    return pl.ds(GROUP_BASE[g] + l0 * pitch + c, Q_TILE, stride=pitch)


def _load_tokens(ref, g, t):
    r = DILATIONS[g]
    if r % 16:
        return ref[pl.ds(pl.multiple_of(GROUP_BASE[g] + t * Q_TILE, Q_TILE), Q_TILE), :]
    n = Q_TILE // r
    base = pl.multiple_of(GROUP_BASE[g] + t * n * PADDED_PITCH, 8)
    return jnp.concatenate([ref[pl.ds(base + i * PADDED_PITCH, r), :] for i in range(n)], axis=0)


def _split_heads(q):
    first = lax.broadcasted_iota(jnp.int32, (1, LANES), 1) < HEAD_DIM
    scale = HEAD_DIM ** -0.5
    m0 = jnp.where(first, scale, 0.0).astype(BF16)
    m1 = jnp.where(first, 0.0, scale).astype(BF16)
    return jnp.concatenate([q * m0, q * m1], axis=0)


def _softmax_pv(s, v):
    first = lax.broadcasted_iota(jnp.int32, (1, LANES), 1) < HEAD_DIM
    m = jnp.max(s, axis=-1, keepdims=True)
    p = jnp.exp(s - m).astype(BF16)
    v1 = jnp.concatenate([v, jnp.ones((v.shape[0], LANES), BF16)], axis=1)
    o2 = jnp.dot(p, v1, preferred_element_type=F32)
    acc = jnp.where(first, o2[:Q_TILE, :LANES], o2[Q_TILE:, :LANES])
    l = jnp.where(first, o2[:Q_TILE, LANES:], o2[Q_TILE:, LANES:])
    m_t = jnp.where(first, m[:Q_TILE], m[Q_TILE:])
    return acc, m_t, l


_NT = (((1,), (1,)), ((), ()))


def _attn_kernel(rel_ref, q0_ref, k0_ref, v0_ref, q1_ref, k1_ref, v1_ref, q2_ref, k2_ref, v2_ref,
                 o_ref, bias_ref, acc_ref, m_ref, l_ref):
    hp = pl.program_id(0)

    @pl.when(pl.program_id(1) == 0)
    def _():
        _build_bias_tiles(rel_ref, bias_ref, hp * HEADS_PER_STEP)

    qkv = ((q0_ref, k0_ref, v0_ref), (q1_ref, k1_ref, v1_ref), (q2_ref, k2_ref, v2_ref))
    for pair in range(PAIRS_PER_STEP):
        lanes = slice(pair * LANES, (pair + 1) * LANES)
        for g in range(N_GROUPS):
            r = DILATIONS[g]
            tiles_per_seq = SEQ // r // Q_TILE
            q_ref, k_ref, v_ref = qkv[g]

            def put(t, res, g=g, tiles_per_seq=tiles_per_seq):
                rows = _token_rows(g, t // tiles_per_seq, (t % tiles_per_seq) * Q_TILE)
                for ref, val in zip((acc_ref, m_ref, l_ref), res):
                    ref[rows, :] = val

            if tiles_per_seq > 1:
                for t in range(N_Q_TILES):
                    tl = t % tiles_per_seq
                    kind = 0 if tl == 0 else (2 if tl == tiles_per_seq - 1 else 1)
                    q0 = t * Q_TILE
                    k0 = q0 - N_SIDE * kind
                    q2 = _split_heads(q_ref[0, q0:q0 + Q_TILE, lanes])
                    s = lax.dot_general(q2, k_ref[0, k0:k0 + K_TILE, lanes], _NT,
                                        preferred_element_type=F32)
                    s = s + jnp.concatenate([bias_ref[g, 2 * pair, kind],
                                             bias_ref[g, 2 * pair + 1, kind]], axis=0)
                    put(t, _softmax_pv(s, v_ref[0, k0:k0 + K_TILE, lanes]))
            else:
                bias = jnp.concatenate([bias_ref[g, 2 * pair, 0, :, :Q_TILE],
                                        bias_ref[g, 2 * pair + 1, 0, :, :Q_TILE]], axis=0)
                for t in range(0, N_Q_TILES, 2):
                    q0 = t * Q_TILE
                    q = q_ref[0, q0:q0 + 2 * Q_TILE, lanes]
                    q2 = jnp.concatenate([_split_heads(q[:Q_TILE]), _split_heads(q[Q_TILE:])], axis=0)
                    s = lax.dot_general(q2, k_ref[0, q0:q0 + 2 * Q_TILE, lanes], _NT,
                                        preferred_element_type=F32)
                    for i in range(2):
                        si = s[2 * i * Q_TILE:2 * (i + 1) * Q_TILE, i * Q_TILE:(i + 1) * Q_TILE]
                        v = v_ref[0, q0 + i * Q_TILE:q0 + (i + 1) * Q_TILE, lanes]
                        put(t + i, _softmax_pv(si + bias, v))

        def merge_body(t, carry, lanes=lanes):
            m = [_load_tokens(m_ref, g, t) for g in range(N_GROUPS)]
            top = jnp.maximum(jnp.maximum(m[0], m[1]), m[2])
            w = [jnp.exp(x - top) for x in m]
            num = sum(w[g] * _load_tokens(acc_ref, g, t) for g in range(N_GROUPS))
            den = sum(w[g] * _load_tokens(l_ref, g, t) for g in range(N_GROUPS))
            rows = pl.ds(pl.multiple_of(t * Q_TILE, Q_TILE), Q_TILE)
            o_ref[0, rows, lanes] = (num / den).astype(BF16)
            return carry

        lax.fori_loop(0, N_Q_TILES, merge_body, 0)


def _attention(qkv, rel_bias):
    n_hp = N_HEADS // HEADS_PER_STEP
    blk = (1, SEQ, STEP_LANES)
    specs = [pl.BlockSpec(memory_space=pltpu.SMEM)]
    args = [rel_bias]
    for g in range(N_GROUPS):
        for part in range(3):
            specs.append(pl.BlockSpec(blk, lambda hp, b, part=part: (b, 0, part * n_hp + hp)))
            args.append(qkv[g])
    return pl.pallas_call(
        _attn_kernel,
        grid=(n_hp, BATCH),
        in_specs=specs,
        out_specs=pl.BlockSpec(blk, lambda hp, b: (b, 0, hp)),
        out_shape=jax.ShapeDtypeStruct((BATCH, SEQ, D_MODEL), BF16),
        scratch_shapes=[pltpu.VMEM((N_GROUPS, HEADS_PER_STEP, N_KINDS, Q_TILE, K_TILE), F32),
                        pltpu.VMEM((sum(GROUP_ROWS), LANES), F32),
                        pltpu.VMEM((sum(GROUP_ROWS), LANES), F32),
                        pltpu.VMEM((sum(GROUP_ROWS), LANES), F32)],
        compiler_params=_params(("arbitrary", "arbitrary")),
        name="dilated_attention",
    )(*args)


MLP_TM = 512
MLP_FC = 1024


def _mlp_kernel(*refs, with_proj):
    if with_proj:
        a_ref, wo_ref, gmix_ref, *refs = refs
    x_ref, gpre_ref, gpost_ref, wup_ref, wdn_ref, o_ref = refs
    x = x_ref[...]
    if with_proj:
        m = jnp.dot(a_ref[...], wo_ref[...].astype(BF16), preferred_element_type=F32)
        x = x + _rms(m, gmix_ref[...])
    h = _rms(x, gpre_ref[...]).astype(BF16)
    acc = jnp.zeros((MLP_TM, D_MODEL), F32)
    for c in range(D_FF // MLP_FC):
        cols = slice(c * MLP_FC, (c + 1) * MLP_FC)
        u = jnp.dot(h, wup_ref[:, cols].astype(BF16), preferred_element_type=F32)
        u = jnp.square(jnp.maximum(u, 0.0)).astype(BF16)
        acc = acc + jnp.dot(u, wdn_ref[cols, :].astype(BF16), preferred_element_type=F32)
    o_ref[...] = x + _rms(acc, gpost_ref[...])


def _mlp(x2d, g_pre, g_post, w_up, w_down, layer, proj=None):
    n = x2d.shape[0]
    resident = pl.Buffered(1)
    tile = pl.BlockSpec((MLP_TM, D_MODEL), lambda i: (i, 0))
    gain = pl.BlockSpec((1, D_MODEL), lambda i: (0, 0))
    specs, args = [], []
    if proj is not None:
        a2d, w_o, g_mix = proj
        specs += [tile, pl.BlockSpec((None, D_MODEL, D_MODEL), lambda i: (0, 0, 0),
                                     pipeline_mode=resident), gain]
        args += [a2d, w_o, g_mix.reshape(1, D_MODEL)]
    specs += [tile, gain, gain,
              pl.BlockSpec((None, D_MODEL, D_FF), lambda i: (layer, 0, 0), pipeline_mode=resident),
              pl.BlockSpec((None, D_FF, D_MODEL), lambda i: (layer, 0, 0), pipeline_mode=resident)]
    args += [x2d, g_pre.reshape(1, D_MODEL), g_post.reshape(1, D_MODEL), w_up, w_down]
    return pl.pallas_call(
        functools.partial(_mlp_kernel, with_proj=proj is not None),
        grid=(n // MLP_TM,),
        in_specs=specs,
        out_specs=tile,
        out_shape=jax.ShapeDtypeStruct((n, D_MODEL), F32),
        compiler_params=_params(("arbitrary",)),
        name="mlp",
    )(*args)


CONV_TS = 512
HALO = 16
CONV_ROWS = CONV_TS + 2 * HALO
CONV_RB = 128
CONV_NC = 256


def _conv_kernel(x_ref, gpre_ref, w1_ref, b1_ref, wdw_ref, bdw_ref, lng_ref, lnb_ref,
                 w2_ref, b2_ref, gpost_ref, o_ref, h_ref, u_ref, c_ref):
    j = pl.program_id(1)
    nj = pl.num_programs(1)
    s0 = pl.multiple_of(j * CONV_TS, CONV_TS)
    top = pl.multiple_of(jnp.maximum(s0 - HALO, 0), HALO)
    bot = pl.multiple_of(jnp.minimum(s0 + CONV_TS, SEQ - HALO), HALO)
    gpre = gpre_ref[...]

    h_ref[0:HALO, :] = _rms(x_ref[0, pl.ds(top, HALO), :], gpre).astype(BF16)
    h_ref[HALO:HALO + CONV_TS, :] = _rms(x_ref[0, pl.ds(s0, CONV_TS), :], gpre).astype(BF16)
    h_ref[HALO + CONV_TS:, :] = _rms(x_ref[0, pl.ds(bot, HALO), :], gpre).astype(BF16)

    h = h_ref[...]
    half = CONV_RB // 2
    for nc in range(D_MODEL // CONV_NC):
        cols = slice(nc * CONV_NC, (nc + 1) * CONV_NC)
        gcols = slice(D_MODEL + nc * CONV_NC, D_MODEL + (nc + 1) * CONV_NC)
        a = jnp.dot(h, w1_ref[:, cols], preferred_element_type=F32) + b1_ref[:, cols]
        gate = jnp.dot(h, w1_ref[:, gcols], preferred_element_type=F32) + b1_ref[:, gcols]
        u = a * jax.nn.sigmoid(gate)
        for k in range(CONV_NC // LANES):
            lc = nc * CONV_NC // LANES + k
            lanes = slice(lc * LANES, (lc + 1) * LANES)
            uk = u[:, k * LANES:(k + 1) * LANES]
            u_ref[lc, 0:HALO, :] = jnp.where(j > 0, uk[0:HALO], 0.0)
            u_ref[lc, HALO:HALO + CONV_TS, :] = uk[HALO:HALO + CONV_TS]
            u_ref[lc, HALO + CONV_TS:, :] = jnp.where(j < nj - 1, uk[HALO + CONV_TS:], 0.0)
            for r0 in range(0, CONV_TS, CONV_RB):
                for phase in range(2):
                    acc = bdw_ref[:, lanes]
                    for t in range(CONV_WIDTH):
                        start = r0 + phase + t + HALO - CONV_WIDTH // 2
                        acc = acc + (u_ref[lc, pl.ds(start, half, stride=2), :]
                                     * wdw_ref[t:t + 1, lanes])
                    c_ref[lc, pl.ds(r0 + phase, half, stride=2), :] = acc

    v = jnp.concatenate([c_ref[lc] for lc in range(D_MODEL // LANES)], axis=1)
    mu = jnp.mean(v, axis=-1, keepdims=True)
    var = jnp.mean(jnp.square(v - mu), axis=-1, keepdims=True)
    y = (v - mu) * lax.rsqrt(var + LN_EPS) * lng_ref[...] + lnb_ref[...]
    y = y * jax.nn.sigmoid(y)
    z = jnp.dot(y.astype(BF16), w2_ref[...], preferred_element_type=F32) + b2_ref[...]
    o_ref[0] = x_ref[0, pl.ds(s0, CONV_TS), :] + _rms(z, gpost_ref[...])


def _conv_layer(x, g_pre, w1, b1, wdw, bdw, lng, lnb, w2, b2, g_post):
    row = lambda v: v.reshape(1, -1)
    const = lambda shape: pl.BlockSpec(shape, lambda b, j: (0, 0))
    return pl.pallas_call(
        _conv_kernel,
        grid=(BATCH, SEQ // CONV_TS),
        in_specs=[pl.BlockSpec((1, SEQ, D_MODEL), lambda b, j: (b, 0, 0)),
                  const((1, D_MODEL)),
                  const((D_MODEL, 2 * D_MODEL)),
                  const((1, 2 * D_MODEL)),
                  const((CONV_WIDTH, D_MODEL)),
                  const((1, D_MODEL)),
                  const((1, D_MODEL)),
                  const((1, D_MODEL)),
                  const((D_MODEL, D_MODEL)),
                  const((1, D_MODEL)),
                  const((1, D_MODEL))],
        out_specs=pl.BlockSpec((1, CONV_TS, D_MODEL), lambda b, j: (b, j, 0)),
        out_shape=jax.ShapeDtypeStruct((BATCH, SEQ, D_MODEL), F32),
        scratch_shapes=[pltpu.VMEM((CONV_ROWS, D_MODEL), BF16),
                        pltpu.VMEM((D_MODEL // LANES, CONV_ROWS, LANES), F32),
                        pltpu.VMEM((D_MODEL // LANES, CONV_TS, LANES), F32)],
        compiler_params=_params(("arbitrary", "arbitrary")),
        name="conformer_conv",
    )(x, row(g_pre), w1, row(b1), wdw, row(bdw), row(lng), row(lnb), w2, row(b2), row(g_post))


def kernel(x, rel_bias, norm_mix_pre, norm_mix_post, norm_mlp_pre, norm_mlp_post, attn_w_qkv,
           attn_w_o, conv_w_pw1, conv_b_pw1, conv_w_dw, conv_b_dw, conv_ln_g, conv_ln_b,
           conv_w_pw2, conv_b_pw2, mlp_w_up, mlp_w_down):
    n_tok = BATCH * SEQ
    x2d = x.reshape(n_tok, D_MODEL)

    h = _prenorm(x, norm_mix_pre[0])
    w_qkv = attn_w_qkv.reshape(D_MODEL, N_GROUPS * GROUP_WIDTH)
    qkv = [_qkv_proj(h[g], w_qkv, g) for g in range(N_GROUPS)]
    a = _attention(qkv, rel_bias)
    x2d = _mlp(x2d, norm_mlp_pre[0], norm_mlp_post[0], mlp_w_up, mlp_w_down, 0,
               proj=(a.reshape(n_tok, D_MODEL), attn_w_o, norm_mix_post[0]))

    x3d = _conv_layer(x2d.reshape(BATCH, SEQ, D_MODEL), norm_mix_pre[1],
                      conv_w_pw1[0].astype(BF16), conv_b_pw1[0], conv_w_dw[0], conv_b_dw[0],
                      conv_ln_g[0], conv_ln_b[0], conv_w_pw2[0].astype(BF16), conv_b_pw2[0],
                      norm_mix_post[1])
    x2d = _mlp(x3d.reshape(n_tok, D_MODEL), norm_mlp_pre[1], norm_mlp_post[1],
               mlp_w_up, mlp_w_down, 1)
    return x2d.reshape(BATCH, SEQ, D_MODEL)
```

```python
import functools
import math

import jax
import jax.numpy as jnp
from jax import lax
from jax.experimental import pallas as pl
from jax.experimental.pallas import tpu as pltpu

D_MODEL = 1024
BATCH = 8
SEQ = 2048
HEAD_DIM = 64
N_HEADS = 16
DILATIONS = (1, 4, 16)
N_SIDE = 64
N_GROUPS = 3
GROUP_WIDTH = 3 * N_HEADS * HEAD_DIM
N_BUCKETS = 32
MAX_DISTANCE = 1024
CONV_WIDTH = 31
D_FF = 4 * D_MODEL
RMS_EPS = 1e-6
LN_EPS = 1e-5
NEG_INF = -1e30

F32 = jnp.float32
BF16 = jnp.bfloat16

LANES = 128
Q_TILE = 128
K_TILE = Q_TILE + 2 * N_SIDE
VMEM_LIMIT = 56 * 1024 * 1024


def _rms(x, g):
    return x * lax.rsqrt(jnp.mean(x * x, axis=-1, keepdims=True) + RMS_EPS) * g


def _params(semantics):
    return pltpu.CompilerParams(dimension_semantics=semantics, vmem_limit_bytes=VMEM_LIMIT)


NORM_ROWS = 256


def _prenorm_kernel(x_ref, g_ref, *refs):
    out_refs, slab_ref = refs[:-1], refs[-1]

    def chunk(i, carry):
        rows = pl.ds(pl.multiple_of(i * NORM_ROWS, NORM_ROWS), NORM_ROWS)
        hn = _rms(x_ref[0, rows, :], g_ref[...])
        for k in range(D_MODEL // LANES):
            slab_ref[k, rows, :] = hn[:, k * LANES:(k + 1) * LANES]
        for out_ref, r in zip(out_refs, DILATIONS):
            if r == 1:
                out_ref[0, rows, :] = hn.astype(BF16)
        return carry

    lax.fori_loop(0, SEQ // NORM_ROWS, chunk, 0)
    for out_ref, r in zip(out_refs, DILATIONS):
        if r == 1:
            continue
        L = SEQ // r
        for c in range(r):
            for k in range(D_MODEL // LANES):
                out_ref[0, c * L:(c + 1) * L, k * LANES:(k + 1) * LANES] = (
                    slab_ref[k, pl.ds(c, L, stride=r), :].astype(BF16))


def _prenorm(x, g):
    blk = pl.BlockSpec((1, SEQ, D_MODEL), lambda b: (b, 0, 0))
    return pl.pallas_call(
        _prenorm_kernel,
        grid=(BATCH,),
        in_specs=[blk, pl.BlockSpec((1, D_MODEL), lambda b: (0, 0))],
        out_specs=[blk] * N_GROUPS,
        out_shape=[jax.ShapeDtypeStruct((BATCH, SEQ, D_MODEL), BF16)] * N_GROUPS,
        scratch_shapes=[pltpu.VMEM((D_MODEL // LANES, SEQ, LANES), F32)],
        compiler_params=_params(("arbitrary",)),
        name="prenorm",
    )(x, g.reshape(1, D_MODEL))


QKV_TN = 1024
QKV_TM = 512


def _qkv_kernel(h_ref, w_ref, o_ref, wb_ref):
    @pl.when(pl.program_id(1) == 0)
    def _():
        wb_ref[...] = w_ref[...].astype(BF16)

    for m in range(SEQ // QKV_TM):
        rows = slice(m * QKV_TM, (m + 1) * QKV_TM)
        o_ref[0, rows, :] = jnp.dot(h_ref[0, rows, :], wb_ref[...],
                                    preferred_element_type=F32).astype(BF16)


def _qkv_proj(h, w_qkv, g):
    nj = GROUP_WIDTH // QKV_TN
    return pl.pallas_call(
        _qkv_kernel,
        grid=(nj, BATCH),
        in_specs=[pl.BlockSpec((1, SEQ, D_MODEL), lambda j, b: (b, 0, 0)),
                  pl.BlockSpec((D_MODEL, QKV_TN), lambda j, b: (0, g * nj + j))],
        out_specs=pl.BlockSpec((1, SEQ, QKV_TN), lambda j, b: (b, 0, j)),
        out_shape=jax.ShapeDtypeStruct((BATCH, SEQ, GROUP_WIDTH), BF16),
        scratch_shapes=[pltpu.VMEM((D_MODEL, QKV_TN), BF16)],
        compiler_params=_params(("arbitrary", "arbitrary")),
        name=f"qkv_proj_g{g}",
    )(h, w_qkv)


HEADS_PER_STEP = 4
PAIRS_PER_STEP = HEADS_PER_STEP // 2
STEP_LANES = HEADS_PER_STEP * HEAD_DIM
N_Q_TILES = SEQ // Q_TILE
N_KINDS = 3
PADDED_PITCH = 24
GROUP_ROWS = tuple(SEQ // 16 * PADDED_PITCH if r % 16 == 0 else SEQ for r in DILATIONS)
GROUP_BASE = tuple(sum(GROUP_ROWS[:g]) for g in range(N_GROUPS))


def _bias_row(rel_ref, g, head):
    r = DILATIONS[g]
    nb = N_BUCKETS // 2
    max_exact = nb // 2
    delta = lax.broadcasted_iota(jnp.int32, (8, K_TILE), 1) - N_SIDE
    rel = delta * r
    n = jnp.abs(rel)
    nf = jnp.maximum(n, 1).astype(F32)
    large = max_exact + (jnp.log(nf / max_exact) / math.log(MAX_DISTANCE / max_exact)
                         * (nb - max_exact)).astype(jnp.int32)
    large = jnp.minimum(large, nb - 1)
    bucket = jnp.where(rel > 0, nb, 0) + jnp.where(n < max_exact, n, large)
    u = jnp.zeros((8, K_TILE), F32)
    for b in range(N_BUCKETS):
        u = jnp.where(bucket == b, rel_ref[b, g * N_HEADS + head], u)
    return jnp.where(jnp.abs(delta) <= N_SIDE, u, NEG_INF)


def _build_bias_tiles(rel_ref, bias_ref, head0):
    qi = lax.broadcasted_iota(jnp.int32, (Q_TILE, K_TILE), 0)
    kj = lax.broadcasted_iota(jnp.int32, (Q_TILE, K_TILE), 1)
    for g in range(N_GROUPS):
        for h in range(HEADS_PER_STEP):
            u = _bias_row(rel_ref, g, head0 + h)
            ub = jnp.broadcast_to(u[0:1, :], (Q_TILE, K_TILE))
            for kind in range(N_KINDS):
                shift = (N_SIDE * kind - N_SIDE) % K_TILE
                t = pltpu.roll(ub, shift, 1, stride=1, stride_axis=0)
                in_band = jnp.abs(kj - qi - N_SIDE * kind) <= N_SIDE
                bias_ref[g, h, kind] = jnp.where(in_band, t, NEG_INF)


def _token_rows(g, c, l0):
    r = DILATIONS[g]
    if r == 1:
        return pl.ds(GROUP_BASE[g] + l0, Q_TILE)
    pitch = PADDED_PITCH if r % 16 == 0 else r
    return pl.ds(GROUP_BASE[g] + l0 * pitch + c, Q_TILE, stride=pitch)


def _load_tokens(ref, g, t):
    r = DILATIONS[g]
    if r % 16:
        return ref[pl.ds(pl.multiple_of(GROUP_BASE[g] + t * Q_TILE, Q_TILE), Q_TILE), :]
    n = Q_TILE // r
    base = pl.multiple_of(GROUP_BASE[g] + t * n * PADDED_PITCH, 8)
    return jnp.concatenate([ref[pl.ds(base + i * PADDED_PITCH, r), :] for i in range(n)], axis=0)


def _split_heads(q):
    first = lax.broadcasted_iota(jnp.int32, (1, LANES), 1) < HEAD_DIM
    scale = HEAD_DIM ** -0.5
    m0 = jnp.where(first, scale, 0.0).astype(BF16)
    m1 = jnp.where(first, 0.0, scale).astype(BF16)
    return jnp.concatenate([q * m0, q * m1], axis=0)


def _softmax_pv(s, v):
    first = lax.broadcasted_iota(jnp.int32, (1, LANES), 1) < HEAD_DIM
    m = jnp.max(s, axis=-1, keepdims=True)
    p = jnp.exp(s - m).astype(BF16)
    v1 = jnp.concatenate([v, jnp.ones((v.shape[0], LANES), BF16)], axis=1)
    o2 = jnp.dot(p, v1, preferred_element_type=F32)
    acc = jnp.where(first, o2[:Q_TILE, :LANES], o2[Q_TILE:, :LANES])
    l = jnp.where(first, o2[:Q_TILE, LANES:], o2[Q_TILE:, LANES:])
    m_t = jnp.where(first, m[:Q_TILE], m[Q_TILE:])
    return acc, m_t, l


_NT = (((1,), (1,)), ((), ()))


def _attn_kernel(rel_ref, q0_ref, k0_ref, v0_ref, q1_ref, k1_ref, v1_ref, q2_ref, k2_ref, v2_ref,
                 o_ref, bias_ref, acc_ref, m_ref, l_ref):
    hp = pl.program_id(0)

    @pl.when(pl.program_id(1) == 0)
    def _():
        _build_bias_tiles(rel_ref, bias_ref, hp * HEADS_PER_STEP)

    qkv = ((q0_ref, k0_ref, v0_ref), (q1_ref, k1_ref, v1_ref), (q2_ref, k2_ref, v2_ref))
    for pair in range(PAIRS_PER_STEP):
        lanes = slice(pair * LANES, (pair + 1) * LANES)
        for g in range(N_GROUPS):
            r = DILATIONS[g]
            tiles_per_seq = SEQ // r // Q_TILE
            q_ref, k_ref, v_ref = qkv[g]

            def put(t, res, g=g, tiles_per_seq=tiles_per_seq):
                rows = _token_rows(g, t // tiles_per_seq, (t % tiles_per_seq) * Q_TILE)
                for ref, val in zip((acc_ref, m_ref, l_ref), res):
                    ref[rows, :] = val

            if tiles_per_seq > 1:
                for t in range(N_Q_TILES):
                    tl = t % tiles_per_seq
                    kind = 0 if tl == 0 else (2 if tl == tiles_per_seq - 1 else 1)
                    q0 = t * Q_TILE
                    k0 = q0 - N_SIDE * kind
                    q2 = _split_heads(q_ref[0, q0:q0 + Q_TILE, lanes])
                    s = lax.dot_general(q2, k_ref[0, k0:k0 + K_TILE, lanes], _NT,
                                        preferred_element_type=F32)
                    s = s + jnp.concatenate([bias_ref[g, 2 * pair, kind],
                                             bias_ref[g, 2 * pair + 1, kind]], axis=0)
                    put(t, _softmax_pv(s, v_ref[0, k0:k0 + K_TILE, lanes]))
            else:
                bias = jnp.concatenate([bias_ref[g, 2 * pair, 0, :, :Q_TILE],
                                        bias_ref[g, 2 * pair + 1, 0, :, :Q_TILE]], axis=0)
                for t in range(0, N_Q_TILES, 2):
                    q0 = t * Q_TILE
                    q = q_ref[0, q0:q0 + 2 * Q_TILE, lanes]
                    q2 = jnp.concatenate([_split_heads(q[:Q_TILE]), _split_heads(q[Q_TILE:])], axis=0)
                    s = lax.dot_general(q2, k_ref[0, q0:q0 + 2 * Q_TILE, lanes], _NT,
                                        preferred_element_type=F32)
                    for i in range(2):
                        si = s[2 * i * Q_TILE:2 * (i + 1) * Q_TILE, i * Q_TILE:(i + 1) * Q_TILE]
                        v = v_ref[0, q0 + i * Q_TILE:q0 + (i + 1) * Q_TILE, lanes]
                        put(t + i, _softmax_pv(si + bias, v))

        def merge_body(t, carry, lanes=lanes):
            m = [_load_tokens(m_ref, g, t) for g in range(N_GROUPS)]
            top = jnp.maximum(jnp.maximum(m[0], m[1]), m[2])
            w = [jnp.exp(x - top) for x in m]
            num = sum(w[g] * _load_tokens(acc_ref, g, t) for g in range(N_GROUPS))
            den = sum(w[g] * _load_tokens(l_ref, g, t) for g in range(N_GROUPS))
            rows = pl.ds(pl.multiple_of(t * Q_TILE, Q_TILE), Q_TILE)
            o_ref[0, rows, lanes] = (num / den).astype(BF16)
            return carry

        lax.fori_loop(0, N_Q_TILES, merge_body, 0)


def _attention(qkv, rel_bias):
    n_hp = N_HEADS // HEADS_PER_STEP
    blk = (1, SEQ, STEP_LANES)
    specs = [pl.BlockSpec(memory_space=pltpu.SMEM)]
    args = [rel_bias]
    for g in range(N_GROUPS):
        for part in range(3):
            specs.append(pl.BlockSpec(blk, lambda hp, b, part=part: (b, 0, part * n_hp + hp)))
            args.append(qkv[g])
    return pl.pallas_call(
        _attn_kernel,
        grid=(n_hp, BATCH),
        in_specs=specs,
        out_specs=pl.BlockSpec(blk, lambda hp, b: (b, 0, hp)),
        out_shape=jax.ShapeDtypeStruct((BATCH, SEQ, D_MODEL), BF16),
        scratch_shapes=[pltpu.VMEM((N_GROUPS, HEADS_PER_STEP, N_KINDS, Q_TILE, K_TILE), F32),
                        pltpu.VMEM((sum(GROUP_ROWS), LANES), F32),
                        pltpu.VMEM((sum(GROUP_ROWS), LANES), F32),
                        pltpu.VMEM((sum(GROUP_ROWS), LANES), F32)],
        compiler_params=_params(("arbitrary", "arbitrary")),
        name="dilated_attention",
    )(*args)


MLP_TM = 512
MLP_FC = 1024
MLP_STAGES = 2 + 2 * (D_FF // MLP_FC)


_DONE = object()


def _interleave(order, **stage_generators):
    for key in order:
        next(stage_generators[key], _DONE)
    for gen in stage_generators.values():
        assert next(gen, _DONE) is _DONE, "order does not cover every stage"


def _mlp_stages(read_x, gpre_ref, gpost_ref, wup_ref, wdn_ref, o_ref):
    h = _rms(read_x(), gpre_ref[...]).astype(BF16)
    acc = jnp.zeros((MLP_TM, D_MODEL), F32)
    yield
    for c in range(D_FF // MLP_FC):
        cols = slice(c * MLP_FC, (c + 1) * MLP_FC)
        u = jnp.dot(h, wup_ref[:, cols].astype(BF16), preferred_element_type=F32)
        u = jnp.square(jnp.maximum(u, 0.0)).astype(BF16)
        yield
        acc = acc + jnp.dot(u, wdn_ref[cols, :].astype(BF16), preferred_element_type=F32)
        yield
    o_ref[...] = read_x() + _rms(acc, gpost_ref[...])


def _mlp_kernel(*refs, with_proj):
    if with_proj:
        a_ref, wo_ref, gmix_ref, *refs = refs
    x_ref, gpre_ref, gpost_ref, wup_ref, wdn_ref, o_ref = refs
    x = x_ref[...]
    if with_proj:
        m = jnp.dot(a_ref[...], wo_ref[...].astype(BF16), preferred_element_type=F32)
        x = x + _rms(m, gmix_ref[...])
    _interleave("m" * MLP_STAGES,
                m=_mlp_stages(lambda: x, gpre_ref, gpost_ref, wup_ref, wdn_ref, o_ref))


def _mlp(x2d, g_pre, g_post, w_up, w_down, layer, proj=None):
    n = x2d.shape[0]
    resident = pl.Buffered(1)
    tile = pl.BlockSpec((MLP_TM, D_MODEL), lambda i: (i, 0))
    gain = pl.BlockSpec((1, D_MODEL), lambda i: (0, 0))
    specs, args = [], []
    if proj is not None:
        a2d, w_o, g_mix = proj
        specs += [tile, pl.BlockSpec((None, D_MODEL, D_MODEL), lambda i: (0, 0, 0),
                                     pipeline_mode=resident), gain]
        args += [a2d, w_o, g_mix.reshape(1, D_MODEL)]
    specs += [tile, gain, gain,
              pl.BlockSpec((None, D_MODEL, D_FF), lambda i: (layer, 0, 0), pipeline_mode=resident),
              pl.BlockSpec((None, D_FF, D_MODEL), lambda i: (layer, 0, 0), pipeline_mode=resident)]
    args += [x2d, g_pre.reshape(1, D_MODEL), g_post.reshape(1, D_MODEL), w_up, w_down]
    return pl.pallas_call(
        functools.partial(_mlp_kernel, with_proj=proj is not None),
        grid=(n // MLP_TM,),
        in_specs=specs,
        out_specs=tile,
        out_shape=jax.ShapeDtypeStruct((n, D_MODEL), F32),
        compiler_params=_params(("arbitrary",)),
        name="mlp",
    )(*args)


CONV_TS = 512
HALO = 16
CONV_ROWS = CONV_TS + 2 * HALO
CONV_RB = 128
CONV_NC = 256


N_TILES = BATCH * SEQ // CONV_TS
TILES_PER_SEQ = SEQ // CONV_TS
assert CONV_TS == MLP_TM


def _conv_stages(xm_ref, xt_ref, xb_ref, first, last, gpre_ref, w1_ref, b1_ref, wdw_ref, bdw_ref,
                 lng_ref, lnb_ref, w2_ref, b2_ref, gpost_ref, o_ref, h_ref, u_ref, c_ref):
    gpre = gpre_ref[...]
    h_ref[0:HALO, :] = _rms(xt_ref[0], gpre).astype(BF16)
    h_ref[HALO:HALO + CONV_TS, :] = _rms(xm_ref[0], gpre).astype(BF16)
    h_ref[HALO + CONV_TS:, :] = _rms(xb_ref[0], gpre).astype(BF16)
    yield

    h = h_ref[...]
    half = CONV_RB // 2
    for nc in range(D_MODEL // CONV_NC):
        cols = slice(nc * CONV_NC, (nc + 1) * CONV_NC)
        gcols = slice(D_MODEL + nc * CONV_NC, D_MODEL + (nc + 1) * CONV_NC)
        a = jnp.dot(h, w1_ref[:, cols], preferred_element_type=F32) + b1_ref[:, cols]
        gate = jnp.dot(h, w1_ref[:, gcols], preferred_element_type=F32) + b1_ref[:, gcols]
        u = a * jax.nn.sigmoid(gate)
        for k in range(CONV_NC // LANES):
            lc = nc * CONV_NC // LANES + k
            lanes = slice(lc * LANES, (lc + 1) * LANES)
            uk = u[:, k * LANES:(k + 1) * LANES]
            u_ref[lc, 0:HALO, :] = jnp.where(first, 0.0, uk[0:HALO])
            u_ref[lc, HALO:HALO + CONV_TS, :] = uk[HALO:HALO + CONV_TS]
            u_ref[lc, HALO + CONV_TS:, :] = jnp.where(last, 0.0, uk[HALO + CONV_TS:])
        yield
        for k in range(CONV_NC // LANES):
            lc = nc * CONV_NC // LANES + k
            lanes = slice(lc * LANES, (lc + 1) * LANES)
            for r0 in range(0, CONV_TS, CONV_RB):
                for phase in range(2):
                    acc = bdw_ref[:, lanes]
                    for t in range(CONV_WIDTH):
                        start = r0 + phase + t + HALO - CONV_WIDTH // 2
                        acc = acc + (u_ref[lc, pl.ds(start, half, stride=2), :]
                                     * wdw_ref[t:t + 1, lanes])
                    c_ref[lc, pl.ds(r0 + phase, half, stride=2), :] = acc
            yield

    v = jnp.concatenate([c_ref[lc] for lc in range(D_MODEL // LANES)], axis=1)
    mu = jnp.mean(v, axis=-1, keepdims=True)
    var = jnp.mean(jnp.square(v - mu), axis=-1, keepdims=True)
    y = (v - mu) * lax.rsqrt(var + LN_EPS) * lng_ref[...] + lnb_ref[...]
    y = (y * jax.nn.sigmoid(y)).astype(BF16)
    yield
    z = jnp.dot(y, w2_ref[...], preferred_element_type=F32) + b2_ref[...]
    o_ref[...] = xm_ref[0] + _rms(z, gpost_ref[...])


def _conv_mlp_kernel(xm_ref, xt_ref, xb_ref, gpre_ref, w1_ref, b1_ref, wdw_ref, bdw_ref, lng_ref,
                     lnb_ref, w2_ref, b2_ref, gpost_ref, g2pre_ref, g2post_ref, wup_ref, wdn_ref,
                     o_ref, h_ref, u_ref, c_ref, x_ref, xnew_ref):
    s = pl.program_id(0)

    @pl.when(s == 0)
    def _():
        x_ref[...] = jnp.zeros((CONV_TS, D_MODEL), F32)

    j = jnp.minimum(s, N_TILES - 1) % TILES_PER_SEQ
    order = "ccccc" + "mm" + "cmccm" * 2 + "cmcm" * 2
    _interleave(
        order,
        c=_conv_stages(xm_ref, xt_ref, xb_ref, j == 0, j == TILES_PER_SEQ - 1, gpre_ref, w1_ref,
                       b1_ref, wdw_ref, bdw_ref, lng_ref, lnb_ref, w2_ref, b2_ref, gpost_ref,
                       xnew_ref, h_ref, u_ref, c_ref),
        m=_mlp_stages(lambda: x_ref[...], g2pre_ref, g2post_ref, wup_ref, wdn_ref, o_ref))
    x_ref[...] = xnew_ref[...]


def _conv_mlp_layer(x, g_pre, w1, b1, wdw, bdw, lng, lnb, w2, b2, g_post, g2_pre, g2_post,
                    w_up, w_down):
    row = lambda v: v.reshape(1, -1)
    resident = pl.Buffered(1)
    const = lambda shape, **kw: pl.BlockSpec(shape, lambda s: (0,) * len(shape), **kw)
    halo_blocks = CONV_TS // HALO

    def tile(s):
        t = jnp.minimum(s, N_TILES - 1)
        return t // TILES_PER_SEQ, t % TILES_PER_SEQ

    def main_map(s):
        b, j = tile(s)
        return b, j, 0

    def top_map(s):
        b, j = tile(s)
        return b, jnp.maximum(j * halo_blocks - 1, 0), 0

    def bot_map(s):
        b, j = tile(s)
        return b, jnp.minimum((j + 1) * halo_blocks, SEQ // HALO - 1), 0

    gain = const((1, D_MODEL))
    return pl.pallas_call(
        _conv_mlp_kernel,
        grid=(N_TILES + 1,),
        in_specs=[pl.BlockSpec((1, CONV_TS, D_MODEL), main_map),
                  pl.BlockSpec((1, HALO, D_MODEL), top_map),
                  pl.BlockSpec((1, HALO, D_MODEL), bot_map),
                  gain,
                  const((D_MODEL, 2 * D_MODEL), pipeline_mode=resident),
                  const((1, 2 * D_MODEL)),
                  const((CONV_WIDTH, D_MODEL)),
                  gain, gain, gain,
                  const((D_MODEL, D_MODEL), pipeline_mode=resident),
                  gain, gain, gain, gain,
                  const((D_MODEL, D_FF), pipeline_mode=resident),
                  const((D_FF, D_MODEL), pipeline_mode=resident)],
        out_specs=pl.BlockSpec((CONV_TS, D_MODEL), lambda s: (jnp.maximum(s - 1, 0), 0)),
        out_shape=jax.ShapeDtypeStruct((BATCH * SEQ, D_MODEL), F32),
        scratch_shapes=[pltpu.VMEM((CONV_ROWS, D_MODEL), BF16),
                        pltpu.VMEM((D_MODEL // LANES, CONV_ROWS, LANES), F32),
                        pltpu.VMEM((D_MODEL // LANES, CONV_TS, LANES), F32),
                        pltpu.VMEM((CONV_TS, D_MODEL), F32),
                        pltpu.VMEM((CONV_TS, D_MODEL), F32)],
        compiler_params=_params(("arbitrary",)),
        name="conformer_conv_mlp",
    )(x, x, x, row(g_pre), w1, row(b1), wdw, row(bdw), row(lng), row(lnb), w2, row(b2),
      row(g_post), row(g2_pre), row(g2_post), w_up, w_down)


def kernel(x, rel_bias, norm_mix_pre, norm_mix_post, norm_mlp_pre, norm_mlp_post, attn_w_qkv,
           attn_w_o, conv_w_pw1, conv_b_pw1, conv_w_dw, conv_b_dw, conv_ln_g, conv_ln_b,
           conv_w_pw2, conv_b_pw2, mlp_w_up, mlp_w_down):
    n_tok = BATCH * SEQ
    x2d = x.reshape(n_tok, D_MODEL)

    h = _prenorm(x, norm_mix_pre[0])
    w_qkv = attn_w_qkv.reshape(D_MODEL, N_GROUPS * GROUP_WIDTH)
    qkv = [_qkv_proj(h[g], w_qkv, g) for g in range(N_GROUPS)]
    a = _attention(qkv, rel_bias)
    x2d = _mlp(x2d, norm_mlp_pre[0], norm_mlp_post[0], mlp_w_up, mlp_w_down, 0,
               proj=(a.reshape(n_tok, D_MODEL), attn_w_o, norm_mix_post[0]))

    x2d = _conv_mlp_layer(x2d.reshape(BATCH, SEQ, D_MODEL), norm_mix_pre[1],
                          conv_w_pw1[0].astype(BF16), conv_b_pw1[0], conv_w_dw[0], conv_b_dw[0],
                          conv_ln_g[0], conv_ln_b[0], conv_w_pw2[0].astype(BF16), conv_b_pw2[0],
                          norm_mix_post[1], norm_mlp_pre[1], norm_mlp_post[1],
                          mlp_w_up[1].astype(BF16), mlp_w_down[1].astype(BF16))
    return x2d.reshape(BATCH, SEQ, D_MODEL)
```

```python
import functools
import math

import jax
import jax.numpy as jnp
from jax import lax
from jax.experimental import pallas as pl
from jax.experimental.pallas import tpu as pltpu

D_MODEL = 1024
BATCH = 8
SEQ = 2048
HEAD_DIM = 64
N_HEADS = 16
DILATIONS = (1, 4, 16)
N_SIDE = 64
N_GROUPS = 3
GROUP_WIDTH = 3 * N_HEADS * HEAD_DIM
N_BUCKETS = 32
MAX_DISTANCE = 1024
CONV_WIDTH = 31
D_FF = 4 * D_MODEL
RMS_EPS = 1e-6
LN_EPS = 1e-5
NEG_INF = -1e30

F32 = jnp.float32
BF16 = jnp.bfloat16

LANES = 128
Q_TILE = 128
K_TILE = Q_TILE + 2 * N_SIDE
VMEM_LIMIT = 56 * 1024 * 1024


def _rms(x, g):
    return x * lax.rsqrt(jnp.mean(x * x, axis=-1, keepdims=True) + RMS_EPS) * g


def _params(semantics):
    return pltpu.CompilerParams(dimension_semantics=semantics, vmem_limit_bytes=VMEM_LIMIT)


NORM_ROWS = 256


def _prenorm_kernel(x_ref, g_ref, *refs):
    out_refs, slab_ref = refs[:-1], refs[-1]

    def chunk(i, carry):
        rows = pl.ds(pl.multiple_of(i * NORM_ROWS, NORM_ROWS), NORM_ROWS)
        hn = _rms(x_ref[0, rows, :], g_ref[...])
        for k in range(D_MODEL // LANES):
            slab_ref[k, rows, :] = hn[:, k * LANES:(k + 1) * LANES]
        for out_ref, r in zip(out_refs, DILATIONS):
            if r == 1:
                out_ref[0, rows, :] = hn.astype(BF16)
        return carry

    lax.fori_loop(0, SEQ // NORM_ROWS, chunk, 0)
    for out_ref, r in zip(out_refs, DILATIONS):
        if r == 1:
            continue
        L = SEQ // r
        for c in range(r):
            for k in range(D_MODEL // LANES):
                out_ref[0, c * L:(c + 1) * L, k * LANES:(k + 1) * LANES] = (
                    slab_ref[k, pl.ds(c, L, stride=r), :].astype(BF16))


def _prenorm(x, g):
    blk = pl.BlockSpec((1, SEQ, D_MODEL), lambda b: (b, 0, 0))
    return pl.pallas_call(
        _prenorm_kernel,
        grid=(BATCH,),
        in_specs=[blk, pl.BlockSpec((1, D_MODEL), lambda b: (0, 0))],
        out_specs=[blk] * N_GROUPS,
        out_shape=[jax.ShapeDtypeStruct((BATCH, SEQ, D_MODEL), BF16)] * N_GROUPS,
        scratch_shapes=[pltpu.VMEM((D_MODEL // LANES, SEQ, LANES), F32)],
        compiler_params=_params(("arbitrary",)),
        name="prenorm",
    )(x, g.reshape(1, D_MODEL))


QKV_TN = 1024
QKV_TM = 512


def _qkv_kernel(h_ref, w_ref, o_ref, wb_ref):
    @pl.when(pl.program_id(1) == 0)
    def _():
        wb_ref[...] = w_ref[...].astype(BF16)

    for m in range(SEQ // QKV_TM):
        rows = slice(m * QKV_TM, (m + 1) * QKV_TM)
        o_ref[0, rows, :] = jnp.dot(h_ref[0, rows, :], wb_ref[...],
                                    preferred_element_type=F32).astype(BF16)


def _qkv_proj(h, w_qkv, g):
    nj = GROUP_WIDTH // QKV_TN
    return pl.pallas_call(
        _qkv_kernel,
        grid=(nj, BATCH),
        in_specs=[pl.BlockSpec((1, SEQ, D_MODEL), lambda j, b: (b, 0, 0)),
                  pl.BlockSpec((D_MODEL, QKV_TN), lambda j, b: (0, g * nj + j))],
        out_specs=pl.BlockSpec((1, SEQ, QKV_TN), lambda j, b: (b, 0, j)),
        out_shape=jax.ShapeDtypeStruct((BATCH, SEQ, GROUP_WIDTH), BF16),
        scratch_shapes=[pltpu.VMEM((D_MODEL, QKV_TN), BF16)],
        compiler_params=_params(("arbitrary", "arbitrary")),
        name=f"qkv_proj_g{g}",
    )(h, w_qkv)


HEADS_PER_STEP = 4
PAIRS_PER_STEP = HEADS_PER_STEP // 2
STEP_LANES = HEADS_PER_STEP * HEAD_DIM
N_Q_TILES = SEQ // Q_TILE
N_KINDS = 3
PADDED_PITCH = 24
GROUP_ROWS = tuple(SEQ // 16 * PADDED_PITCH if r % 16 == 0 else SEQ for r in DILATIONS)
GROUP_BASE = tuple(sum(GROUP_ROWS[:g]) for g in range(N_GROUPS))


def _bias_row(rel_ref, g, head):
    r = DILATIONS[g]
    nb = N_BUCKETS // 2
    max_exact = nb // 2
    delta = lax.broadcasted_iota(jnp.int32, (8, K_TILE), 1) - N_SIDE
    rel = delta * r
    n = jnp.abs(rel)
    nf = jnp.maximum(n, 1).astype(F32)
    large = max_exact + (jnp.log(nf / max_exact) / math.log(MAX_DISTANCE / max_exact)
                         * (nb - max_exact)).astype(jnp.int32)
    large = jnp.minimum(large, nb - 1)
    bucket = jnp.where(rel > 0, nb, 0) + jnp.where(n < max_exact, n, large)
    u = jnp.zeros((8, K_TILE), F32)
    for b in range(N_BUCKETS):
        u = jnp.where(bucket == b, rel_ref[b, g * N_HEADS + head], u)
    return jnp.where(jnp.abs(delta) <= N_SIDE, u, NEG_INF)


def _build_bias_tiles(rel_ref, bias_ref, head0):
    qi = lax.broadcasted_iota(jnp.int32, (Q_TILE, K_TILE), 0)
    kj = lax.broadcasted_iota(jnp.int32, (Q_TILE, K_TILE), 1)
    for g in range(N_GROUPS):
        for h in range(HEADS_PER_STEP):
            u = _bias_row(rel_ref, g, head0 + h)
            ub = jnp.broadcast_to(u[0:1, :], (Q_TILE, K_TILE))
            for kind in range(N_KINDS):
                shift = (N_SIDE * kind - N_SIDE) % K_TILE
                t = pltpu.roll(ub, shift, 1, stride=1, stride_axis=0)
                in_band = jnp.abs(kj - qi - N_SIDE * kind) <= N_SIDE
                bias_ref[g, h, kind] = jnp.where(in_band, t, NEG_INF)


def _token_rows(g, c, l0):
    r = DILATIONS[g]
    if r == 1:
        return pl.ds(GROUP_BASE[g] + l0, Q_TILE)
    pitch = PADDED_PITCH if r % 16 == 0 else r
    return pl.ds(GROUP_BASE[g] + l0 * pitch + c, Q_TILE, stride=pitch)


def _load_tokens(ref, g, t):
    r = DILATIONS[g]
    if r % 16:
        return ref[pl.ds(pl.multiple_of(GROUP_BASE[g] + t * Q_TILE, Q_TILE), Q_TILE), :]
    n = Q_TILE // r
    base = pl.multiple_of(GROUP_BASE[g] + t * n * PADDED_PITCH, 8)
    return jnp.concatenate([ref[pl.ds(base + i * PADDED_PITCH, r), :] for i in range(n)], axis=0)


def _split_heads(q):
    first = lax.broadcasted_iota(jnp.int32, (1, LANES), 1) < HEAD_DIM
    scale = HEAD_DIM ** -0.5
    m0 = jnp.where(first, scale, 0.0).astype(BF16)
    m1 = jnp.where(first, 0.0, scale).astype(BF16)
    return jnp.concatenate([q * m0, q * m1], axis=0)


def _softmax_pv(s, v):
    first = lax.broadcasted_iota(jnp.int32, (1, LANES), 1) < HEAD_DIM
    m = jnp.max(s, axis=-1, keepdims=True)
    p = jnp.exp(s - m).astype(BF16)
    v1 = jnp.concatenate([v, jnp.ones((v.shape[0], LANES), BF16)], axis=1)
    o2 = jnp.dot(p, v1, preferred_element_type=F32)
    acc = jnp.where(first, o2[:Q_TILE, :LANES], o2[Q_TILE:, :LANES])
    l = jnp.where(first, o2[:Q_TILE, LANES:], o2[Q_TILE:, LANES:])
    m_t = jnp.where(first, m[:Q_TILE], m[Q_TILE:])
    return acc, m_t, l


_NT = (((1,), (1,)), ((), ()))


def _attn_kernel(rel_ref, q0_ref, k0_ref, v0_ref, q1_ref, k1_ref, v1_ref, q2_ref, k2_ref, v2_ref,
                 o_ref, bias_ref, acc_ref, m_ref, l_ref):
    hp = pl.program_id(0)

    @pl.when(pl.program_id(1) == 0)
    def _():
        _build_bias_tiles(rel_ref, bias_ref, hp * HEADS_PER_STEP)

    qkv = ((q0_ref, k0_ref, v0_ref), (q1_ref, k1_ref, v1_ref), (q2_ref, k2_ref, v2_ref))
    for pair in range(PAIRS_PER_STEP):
        lanes = slice(pair * LANES, (pair + 1) * LANES)
        for g in range(N_GROUPS):
            r = DILATIONS[g]
            tiles_per_seq = SEQ // r // Q_TILE
            q_ref, k_ref, v_ref = qkv[g]

            def put(t, res, g=g, tiles_per_seq=tiles_per_seq):
                rows = _token_rows(g, t // tiles_per_seq, (t % tiles_per_seq) * Q_TILE)
                for ref, val in zip((acc_ref, m_ref, l_ref), res):
                    ref[rows, :] = val

            if tiles_per_seq > 1:
                for t in range(N_Q_TILES):
                    tl = t % tiles_per_seq
                    kind = 0 if tl == 0 else (2 if tl == tiles_per_seq - 1 else 1)
                    q0 = t * Q_TILE
                    k0 = q0 - N_SIDE * kind
                    q2 = _split_heads(q_ref[0, q0:q0 + Q_TILE, lanes])
                    s = lax.dot_general(q2, k_ref[0, k0:k0 + K_TILE, lanes], _NT,
                                        preferred_element_type=F32)
                    s = s + jnp.concatenate([bias_ref[g, 2 * pair, kind],
                                             bias_ref[g, 2 * pair + 1, kind]], axis=0)
                    put(t, _softmax_pv(s, v_ref[0, k0:k0 + K_TILE, lanes]))
            else:
                bias = jnp.concatenate([bias_ref[g, 2 * pair, 0, :, :Q_TILE],
                                        bias_ref[g, 2 * pair + 1, 0, :, :Q_TILE]], axis=0)
                for t in range(0, N_Q_TILES, 2):
                    q0 = t * Q_TILE
                    q = q_ref[0, q0:q0 + 2 * Q_TILE, lanes]
                    q2 = jnp.concatenate([_split_heads(q[:Q_TILE]), _split_heads(q[Q_TILE:])], axis=0)
                    s = lax.dot_general(q2, k_ref[0, q0:q0 + 2 * Q_TILE, lanes], _NT,
                                        preferred_element_type=F32)
                    for i in range(2):
                        si = s[2 * i * Q_TILE:2 * (i + 1) * Q_TILE, i * Q_TILE:(i + 1) * Q_TILE]
                        v = v_ref[0, q0 + i * Q_TILE:q0 + (i + 1) * Q_TILE, lanes]
                        put(t + i, _softmax_pv(si + bias, v))

        def merge_body(t, carry, lanes=lanes):
            m = [_load_tokens(m_ref, g, t) for g in range(N_GROUPS)]
            top = jnp.maximum(jnp.maximum(m[0], m[1]), m[2])
            w = [jnp.exp(x - top) for x in m]
            num = sum(w[g] * _load_tokens(acc_ref, g, t) for g in range(N_GROUPS))
            den = sum(w[g] * _load_tokens(l_ref, g, t) for g in range(N_GROUPS))
            rows = pl.ds(pl.multiple_of(t * Q_TILE, Q_TILE), Q_TILE)
            o_ref[0, rows, lanes] = (num / den).astype(BF16)
            return carry

        lax.fori_loop(0, N_Q_TILES, merge_body, 0)


def _attention(qkv, rel_bias):
    n_hp = N_HEADS // HEADS_PER_STEP
    blk = (1, SEQ, STEP_LANES)
    specs = [pl.BlockSpec(memory_space=pltpu.SMEM)]
    args = [rel_bias]
    for g in range(N_GROUPS):
        for part in range(3):
            specs.append(pl.BlockSpec(blk, lambda hp, b, part=part: (b, 0, part * n_hp + hp)))
            args.append(qkv[g])
    return pl.pallas_call(
        _attn_kernel,
        grid=(n_hp, BATCH),
        in_specs=specs,
        out_specs=pl.BlockSpec(blk, lambda hp, b: (b, 0, hp)),
        out_shape=jax.ShapeDtypeStruct((BATCH, SEQ, D_MODEL), BF16),
        scratch_shapes=[pltpu.VMEM((N_GROUPS, HEADS_PER_STEP, N_KINDS, Q_TILE, K_TILE), F32),
                        pltpu.VMEM((sum(GROUP_ROWS), LANES), F32),
                        pltpu.VMEM((sum(GROUP_ROWS), LANES), F32),
                        pltpu.VMEM((sum(GROUP_ROWS), LANES), F32)],
        compiler_params=_params(("arbitrary", "arbitrary")),
        name="dilated_attention",
    )(*args)


MLP_TM = 512
MLP_FC = 1024
MLP_STAGES = 2 + 2 * (D_FF // MLP_FC)


_DONE = object()


def _interleave(order, **stage_generators):
    for key in order:
        next(stage_generators[key], _DONE)
    for gen in stage_generators.values():
        assert next(gen, _DONE) is _DONE, "order does not cover every stage"


def _mlp_stages(read_x, gpre_ref, gpost_ref, wup_ref, wdn_ref, o_ref):
    h = _rms(read_x(), gpre_ref[...]).astype(BF16)
    acc = jnp.zeros((MLP_TM, D_MODEL), F32)
    yield
    for c in range(D_FF // MLP_FC):
        cols = slice(c * MLP_FC, (c + 1) * MLP_FC)
        u = jnp.dot(h, wup_ref[:, cols].astype(BF16), preferred_element_type=F32)
        u = jnp.square(jnp.maximum(u, 0.0)).astype(BF16)
        yield
        acc = acc + jnp.dot(u, wdn_ref[cols, :].astype(BF16), preferred_element_type=F32)
        yield
    o_ref[...] = read_x() + _rms(acc, gpost_ref[...])


def _proj_mlp_kernel(a_ref, wo_ref, gmix_ref, x_ref, gpre_ref, gpost_ref, wup_ref, wdn_ref,
                     next_up_ref, next_dn_ref, o_ref, next_up_bf_ref, next_dn_bf_ref):
    next_up_bf_ref[...] = next_up_ref[...].astype(BF16)
    next_dn_bf_ref[...] = next_dn_ref[...].astype(BF16)
    m = jnp.dot(a_ref[...], wo_ref[...].astype(BF16), preferred_element_type=F32)
    x = x_ref[...] + _rms(m, gmix_ref[...])
    _interleave("m" * MLP_STAGES,
                m=_mlp_stages(lambda: x, gpre_ref, gpost_ref, wup_ref, wdn_ref, o_ref))


def _proj_mlp(x2d, a2d, w_o, g_mix, g_pre, g_post, w_up, w_down):
    n = x2d.shape[0]
    steps = n // MLP_TM
    resident = pl.Buffered(1)
    tile = pl.BlockSpec((MLP_TM, D_MODEL), lambda i: (i, 0))
    gain = pl.BlockSpec((1, D_MODEL), lambda i: (0, 0))
    up_rows, dn_rows = D_MODEL // steps, D_FF // steps
    return pl.pallas_call(
        _proj_mlp_kernel,
        grid=(steps,),
        in_specs=[tile,
                  pl.BlockSpec((None, D_MODEL, D_MODEL), lambda i: (0, 0, 0), pipeline_mode=resident),
                  gain, tile, gain, gain,
                  pl.BlockSpec((None, D_MODEL, D_FF), lambda i: (0, 0, 0), pipeline_mode=resident),
                  pl.BlockSpec((None, D_FF, D_MODEL), lambda i: (0, 0, 0), pipeline_mode=resident),
                  pl.BlockSpec((None, up_rows, D_FF), lambda i: (1, i, 0)),
                  pl.BlockSpec((None, dn_rows, D_MODEL), lambda i: (1, i, 0))],
        out_specs=[tile,
                   pl.BlockSpec((up_rows, D_FF), lambda i: (i, 0)),
                   pl.BlockSpec((dn_rows, D_MODEL), lambda i: (i, 0))],
        out_shape=[jax.ShapeDtypeStruct((n, D_MODEL), F32),
                   jax.ShapeDtypeStruct((D_MODEL, D_FF), BF16),
                   jax.ShapeDtypeStruct((D_FF, D_MODEL), BF16)],
        compiler_params=_params(("arbitrary",)),
        name="attn_proj_mlp",
    )(a2d, w_o, g_mix.reshape(1, D_MODEL), x2d, g_pre.reshape(1, D_MODEL),
      g_post.reshape(1, D_MODEL), w_up, w_down, w_up, w_down)


CONV_TS = 512
HALO = 16
CONV_ROWS = CONV_TS + 2 * HALO
CONV_RB = 128
CONV_NC = 256


N_TILES = BATCH * SEQ // CONV_TS
TILES_PER_SEQ = SEQ // CONV_TS
assert CONV_TS == MLP_TM


def _conv_stages(xm_ref, xt_ref, xb_ref, first, last, gpre_ref, w1_ref, b1_ref, wdw_ref, bdw_ref,
                 lng_ref, lnb_ref, w2_ref, b2_ref, gpost_ref, o_ref, h_ref, u_ref, c_ref):
    gpre = gpre_ref[...]
    h_ref[0:HALO, :] = _rms(xt_ref[0], gpre).astype(BF16)
    h_ref[HALO:HALO + CONV_TS, :] = _rms(xm_ref[0], gpre).astype(BF16)
    h_ref[HALO + CONV_TS:, :] = _rms(xb_ref[0], gpre).astype(BF16)
    yield

    h = h_ref[...]
    half = CONV_RB // 2
    for nc in range(D_MODEL // CONV_NC):
        cols = slice(nc * CONV_NC, (nc + 1) * CONV_NC)
        gcols = slice(D_MODEL + nc * CONV_NC, D_MODEL + (nc + 1) * CONV_NC)
        a = jnp.dot(h, w1_ref[:, cols], preferred_element_type=F32) + b1_ref[:, cols]
        gate = jnp.dot(h, w1_ref[:, gcols], preferred_element_type=F32) + b1_ref[:, gcols]
        u = a * jax.nn.sigmoid(gate)
        for k in range(CONV_NC // LANES):
            lc = nc * CONV_NC // LANES + k
            lanes = slice(lc * LANES, (lc + 1) * LANES)
            uk = u[:, k * LANES:(k + 1) * LANES]
            u_ref[lc, 0:HALO, :] = jnp.where(first, 0.0, uk[0:HALO])
            u_ref[lc, HALO:HALO + CONV_TS, :] = uk[HALO:HALO + CONV_TS]
            u_ref[lc, HALO + CONV_TS:, :] = jnp.where(last, 0.0, uk[HALO + CONV_TS:])
        yield
        for k in range(CONV_NC // LANES):
            lc = nc * CONV_NC // LANES + k
            lanes = slice(lc * LANES, (lc + 1) * LANES)
            for r0 in range(0, CONV_TS, CONV_RB):
                for phase in range(2):
                    acc = bdw_ref[:, lanes]
                    for t in range(CONV_WIDTH):
                        start = r0 + phase + t + HALO - CONV_WIDTH // 2
                        acc = acc + (u_ref[lc, pl.ds(start, half, stride=2), :]
                                     * wdw_ref[t:t + 1, lanes])
                    c_ref[lc, pl.ds(r0 + phase, half, stride=2), :] = acc
            yield

    v = jnp.concatenate([c_ref[lc] for lc in range(D_MODEL // LANES)], axis=1)
    mu = jnp.mean(v, axis=-1, keepdims=True)
    var = jnp.mean(jnp.square(v - mu), axis=-1, keepdims=True)
    y = (v - mu) * lax.rsqrt(var + LN_EPS) * lng_ref[...] + lnb_ref[...]
    y = (y * jax.nn.sigmoid(y)).astype(BF16)
    yield
    z = jnp.dot(y, w2_ref[...], preferred_element_type=F32) + b2_ref[...]
    o_ref[...] = xm_ref[0] + _rms(z, gpost_ref[...])


def _conv_mlp_kernel(xm_ref, xt_ref, xb_ref, gpre_ref, w1_ref, b1_ref, wdw_ref, bdw_ref, lng_ref,
                     lnb_ref, w2_ref, b2_ref, gpost_ref, g2pre_ref, g2post_ref, wup_ref, wdn_ref,
                     o_ref, h_ref, u_ref, c_ref, x_ref, xnew_ref):
    s = pl.program_id(0)

    @pl.when(s == 0)
    def _():
        x_ref[...] = jnp.zeros((CONV_TS, D_MODEL), F32)

    j = jnp.minimum(s, N_TILES - 1) % TILES_PER_SEQ
    order = "ccccc" + "mm" + "cmccm" * 2 + "cmcm" * 2
    _interleave(
        order,
        c=_conv_stages(xm_ref, xt_ref, xb_ref, j == 0, j == TILES_PER_SEQ - 1, gpre_ref, w1_ref,
                       b1_ref, wdw_ref, bdw_ref, lng_ref, lnb_ref, w2_ref, b2_ref, gpost_ref,
                       xnew_ref, h_ref, u_ref, c_ref),
        m=_mlp_stages(lambda: x_ref[...], g2pre_ref, g2post_ref, wup_ref, wdn_ref, o_ref))
    x_ref[...] = xnew_ref[...]


def _conv_mlp_layer(x, g_pre, w1, b1, wdw, bdw, lng, lnb, w2, b2, g_post, g2_pre, g2_post,
                    w_up, w_down):
    row = lambda v: v.reshape(1, -1)
    resident = pl.Buffered(1)
    const = lambda shape, **kw: pl.BlockSpec(shape, lambda s: (0,) * len(shape), **kw)
    halo_blocks = CONV_TS // HALO

    def tile(s):
        t = jnp.minimum(s, N_TILES - 1)
        return t // TILES_PER_SEQ, t % TILES_PER_SEQ

    def main_map(s):
        b, j = tile(s)
        return b, j, 0

    def top_map(s):
        b, j = tile(s)
        return b, jnp.maximum(j * halo_blocks - 1, 0), 0

    def bot_map(s):
        b, j = tile(s)
        return b, jnp.minimum((j + 1) * halo_blocks, SEQ // HALO - 1), 0

    gain = const((1, D_MODEL))
    return pl.pallas_call(
        _conv_mlp_kernel,
        grid=(N_TILES + 1,),
        in_specs=[pl.BlockSpec((1, CONV_TS, D_MODEL), main_map),
                  pl.BlockSpec((1, HALO, D_MODEL), top_map),
                  pl.BlockSpec((1, HALO, D_MODEL), bot_map),
                  gain,
                  const((D_MODEL, 2 * D_MODEL), pipeline_mode=resident),
                  const((1, 2 * D_MODEL)),
                  const((CONV_WIDTH, D_MODEL)),
                  gain, gain, gain,
                  const((D_MODEL, D_MODEL), pipeline_mode=resident),
                  gain, gain, gain, gain,
                  const((D_MODEL, D_FF), pipeline_mode=resident),
                  const((D_FF, D_MODEL), pipeline_mode=resident)],
        out_specs=pl.BlockSpec((CONV_TS, D_MODEL), lambda s: (jnp.maximum(s - 1, 0), 0)),
        out_shape=jax.ShapeDtypeStruct((BATCH * SEQ, D_MODEL), F32),
        scratch_shapes=[pltpu.VMEM((CONV_ROWS, D_MODEL), BF16),
                        pltpu.VMEM((D_MODEL // LANES, CONV_ROWS, LANES), F32),
                        pltpu.VMEM((D_MODEL // LANES, CONV_TS, LANES), F32),
                        pltpu.VMEM((CONV_TS, D_MODEL), F32),
                        pltpu.VMEM((CONV_TS, D_MODEL), F32)],
        compiler_params=_params(("arbitrary",)),
        name="conformer_conv_mlp",
    )(x, x, x, row(g_pre), w1, row(b1), wdw, row(bdw), row(lng), row(lnb), w2, row(b2),
      row(g_post), row(g2_pre), row(g2_post), w_up, w_down)


def kernel(x, rel_bias, norm_mix_pre, norm_mix_post, norm_mlp_pre, norm_mlp_post, attn_w_qkv,
           attn_w_o, conv_w_pw1, conv_b_pw1, conv_w_dw, conv_b_dw, conv_ln_g, conv_ln_b,
           conv_w_pw2, conv_b_pw2, mlp_w_up, mlp_w_down):
    n_tok = BATCH * SEQ
    x2d = x.reshape(n_tok, D_MODEL)

    h = _prenorm(x, norm_mix_pre[0])
    w_qkv = attn_w_qkv.reshape(D_MODEL, N_GROUPS * GROUP_WIDTH)
    qkv = [_qkv_proj(h[g], w_qkv, g) for g in range(N_GROUPS)]
    a = _attention(qkv, rel_bias)
    x2d, w_up1, w_down1 = _proj_mlp(x2d, a.reshape(n_tok, D_MODEL), attn_w_o, norm_mix_post[0],
                                    norm_mlp_pre[0], norm_mlp_post[0], mlp_w_up, mlp_w_down)

    x2d = _conv_mlp_layer(x2d.reshape(BATCH, SEQ, D_MODEL), norm_mix_pre[1],
                          conv_w_pw1[0].astype(BF16), conv_b_pw1[0], conv_w_dw[0], conv_b_dw[0],
                          conv_ln_g[0], conv_ln_b[0], conv_w_pw2[0].astype(BF16), conv_b_pw2[0],
                          norm_mix_post[1], norm_mlp_pre[1], norm_mlp_post[1], w_up1, w_down1)
    return x2d.reshape(BATCH, SEQ, D_MODEL)
```

```python
import functools
import math

import jax
import jax.numpy as jnp
from jax import lax
from jax.experimental import pallas as pl
from jax.experimental.pallas import tpu as pltpu

D_MODEL = 1024
BATCH = 8
SEQ = 2048
HEAD_DIM = 64
N_HEADS = 16
DILATIONS = (1, 4, 16)
N_SIDE = 64
N_GROUPS = 3
GROUP_WIDTH = 3 * N_HEADS * HEAD_DIM
N_BUCKETS = 32
MAX_DISTANCE = 1024
CONV_WIDTH = 31
D_FF = 4 * D_MODEL
RMS_EPS = 1e-6
LN_EPS = 1e-5
NEG_INF = -1e30

F32 = jnp.float32
BF16 = jnp.bfloat16

LANES = 128
Q_TILE = 128
K_TILE = Q_TILE + 2 * N_SIDE
VMEM_LIMIT = 56 * 1024 * 1024


def _rms(x, g):
    return x * lax.rsqrt(jnp.mean(x * x, axis=-1, keepdims=True) + RMS_EPS) * g


def _params(semantics):
    return pltpu.CompilerParams(dimension_semantics=semantics, vmem_limit_bytes=VMEM_LIMIT)


NORM_ROWS = 256


def _prenorm_kernel(x_ref, g_ref, *refs):
    out_refs, slab_ref = refs[:-1], refs[-1]

    def chunk(i, carry):
        rows = pl.ds(pl.multiple_of(i * NORM_ROWS, NORM_ROWS), NORM_ROWS)
        hn = _rms(x_ref[0, rows, :], g_ref[...])
        for k in range(D_MODEL // LANES):
            slab_ref[k, rows, :] = hn[:, k * LANES:(k + 1) * LANES]
        for out_ref, r in zip(out_refs, DILATIONS):
            if r == 1:
                out_ref[0, rows, :] = hn.astype(BF16)
        return carry

    lax.fori_loop(0, SEQ // NORM_ROWS, chunk, 0)
    for out_ref, r in zip(out_refs, DILATIONS):
        if r == 1:
            continue
        L = SEQ // r
        for c in range(r):
            for k in range(D_MODEL // LANES):
                out_ref[0, c * L:(c + 1) * L, k * LANES:(k + 1) * LANES] = (
                    slab_ref[k, pl.ds(c, L, stride=r), :].astype(BF16))


def _prenorm(x, g):
    blk = pl.BlockSpec((1, SEQ, D_MODEL), lambda b: (b, 0, 0))
    return pl.pallas_call(
        _prenorm_kernel,
        grid=(BATCH,),
        in_specs=[blk, pl.BlockSpec((1, D_MODEL), lambda b: (0, 0))],
        out_specs=[blk] * N_GROUPS,
        out_shape=[jax.ShapeDtypeStruct((BATCH, SEQ, D_MODEL), BF16)] * N_GROUPS,
        scratch_shapes=[pltpu.VMEM((D_MODEL // LANES, SEQ, LANES), F32)],
        compiler_params=_params(("arbitrary",)),
        name="prenorm",
    )(x, g.reshape(1, D_MODEL))


QKV_TN = 1024
QKV_TM = 512


def _qkv_kernel(h_ref, w_ref, o_ref, wb_ref):
    @pl.when(pl.program_id(1) == 0)
    def _():
        wb_ref[...] = w_ref[...].astype(BF16)

    for m in range(SEQ // QKV_TM):
        rows = slice(m * QKV_TM, (m + 1) * QKV_TM)
        o_ref[0, rows, :] = jnp.dot(h_ref[0, rows, :], wb_ref[...],
                                    preferred_element_type=F32).astype(BF16)


def _qkv_proj(h, w_qkv, g):
    nj = GROUP_WIDTH // QKV_TN
    return pl.pallas_call(
        _qkv_kernel,
        grid=(nj, BATCH),
        in_specs=[pl.BlockSpec((1, SEQ, D_MODEL), lambda j, b: (b, 0, 0)),
                  pl.BlockSpec((D_MODEL, QKV_TN), lambda j, b: (0, g * nj + j))],
        out_specs=pl.BlockSpec((1, SEQ, QKV_TN), lambda j, b: (b, 0, j)),
        out_shape=jax.ShapeDtypeStruct((BATCH, SEQ, GROUP_WIDTH), BF16),
        scratch_shapes=[pltpu.VMEM((D_MODEL, QKV_TN), BF16)],
        compiler_params=_params(("arbitrary", "arbitrary")),
        name=f"qkv_proj_g{g}",
    )(h, w_qkv)


HEADS_PER_STEP = 4
PAIRS_PER_STEP = HEADS_PER_STEP // 2
STEP_LANES = HEADS_PER_STEP * HEAD_DIM
N_Q_TILES = SEQ // Q_TILE
N_KINDS = 3
PADDED_PITCH = 24
GROUP_ROWS = tuple(SEQ // 16 * PADDED_PITCH if r % 16 == 0 else SEQ for r in DILATIONS)
GROUP_BASE = tuple(sum(GROUP_ROWS[:g]) for g in range(N_GROUPS))


def _bias_row(rel_ref, g, head):
    r = DILATIONS[g]
    nb = N_BUCKETS // 2
    max_exact = nb // 2
    delta = lax.broadcasted_iota(jnp.int32, (8, K_TILE), 1) - N_SIDE
    rel = delta * r
    n = jnp.abs(rel)
    nf = jnp.maximum(n, 1).astype(F32)
    large = max_exact + (jnp.log(nf / max_exact) / math.log(MAX_DISTANCE / max_exact)
                         * (nb - max_exact)).astype(jnp.int32)
    large = jnp.minimum(large, nb - 1)
    bucket = jnp.where(rel > 0, nb, 0) + jnp.where(n < max_exact, n, large)
    u = jnp.zeros((8, K_TILE), F32)
    for b in range(N_BUCKETS):
        u = jnp.where(bucket == b, rel_ref[b, g * N_HEADS + head], u)
    return jnp.where(jnp.abs(delta) <= N_SIDE, u, NEG_INF)


def _build_bias_tiles(rel_ref, bias_ref, head0):
    qi = lax.broadcasted_iota(jnp.int32, (Q_TILE, K_TILE), 0)
    kj = lax.broadcasted_iota(jnp.int32, (Q_TILE, K_TILE), 1)
    for g in range(N_GROUPS):
        for h in range(HEADS_PER_STEP):
            u = _bias_row(rel_ref, g, head0 + h)
            ub = jnp.broadcast_to(u[0:1, :], (Q_TILE, K_TILE))
            for kind in range(N_KINDS):
                shift = (N_SIDE * kind - N_SIDE) % K_TILE
                t = pltpu.roll(ub, shift, 1, stride=1, stride_axis=0)
                in_band = jnp.abs(kj - qi - N_SIDE * kind) <= N_SIDE
                bias_ref[g, h, kind] = jnp.where(in_band, t, NEG_INF)


def _token_rows(g, c, l0):
    r = DILATIONS[g]
    if r == 1:
        return pl.ds(GROUP_BASE[g] + l0, Q_TILE)
    pitch = PADDED_PITCH if r % 16 == 0 else r
    return pl.ds(GROUP_BASE[g] + l0 * pitch + c, Q_TILE, stride=pitch)


def _load_tokens(ref, g, t):
    r = DILATIONS[g]
    if r % 16:
        return ref[pl.ds(pl.multiple_of(GROUP_BASE[g] + t * Q_TILE, Q_TILE), Q_TILE), :]
    n = Q_TILE // r
    base = pl.multiple_of(GROUP_BASE[g] + t * n * PADDED_PITCH, 8)
    return jnp.concatenate([ref[pl.ds(base + i * PADDED_PITCH, r), :] for i in range(n)], axis=0)


def _split_heads(q):
    first = lax.broadcasted_iota(jnp.int32, (1, LANES), 1) < HEAD_DIM
    scale = HEAD_DIM ** -0.5
    m0 = jnp.where(first, scale, 0.0).astype(BF16)
    m1 = jnp.where(first, 0.0, scale).astype(BF16)
    return jnp.concatenate([q * m0, q * m1], axis=0)


def _softmax_pv(s, v):
    first = lax.broadcasted_iota(jnp.int32, (1, LANES), 1) < HEAD_DIM
    m = jnp.max(s, axis=-1, keepdims=True)
    p = jnp.exp(s - m).astype(BF16)
    v1 = jnp.concatenate([v, jnp.ones((v.shape[0], LANES), BF16)], axis=1)
    o2 = jnp.dot(p, v1, preferred_element_type=F32)
    acc = jnp.where(first, o2[:Q_TILE, :LANES], o2[Q_TILE:, :LANES])
    l = jnp.where(first, o2[:Q_TILE, LANES:], o2[Q_TILE:, LANES:])
    m_t = jnp.where(first, m[:Q_TILE], m[Q_TILE:])
    return acc, m_t, l


_NT = (((1,), (1,)), ((), ()))


def _attn_kernel(rel_ref, q0_ref, k0_ref, v0_ref, q1_ref, k1_ref, v1_ref, q2_ref, k2_ref, v2_ref,
                 o_ref, bias_ref, acc_ref, m_ref, l_ref):
    hp = pl.program_id(0)

    @pl.when(pl.program_id(1) == 0)
    def _():
        _build_bias_tiles(rel_ref, bias_ref, hp * HEADS_PER_STEP)

    qkv = ((q0_ref, k0_ref, v0_ref), (q1_ref, k1_ref, v1_ref), (q2_ref, k2_ref, v2_ref))
    for pair in range(PAIRS_PER_STEP):
        lanes = slice(pair * LANES, (pair + 1) * LANES)
        for g in range(N_GROUPS):
            r = DILATIONS[g]
            tiles_per_seq = SEQ // r // Q_TILE
            q_ref, k_ref, v_ref = qkv[g]

            def put(t, res, g=g, tiles_per_seq=tiles_per_seq):
                rows = _token_rows(g, t // tiles_per_seq, (t % tiles_per_seq) * Q_TILE)
                for ref, val in zip((acc_ref, m_ref, l_ref), res):
                    ref[rows, :] = val

            if tiles_per_seq > 1:
                for t in range(N_Q_TILES):
                    tl = t % tiles_per_seq
                    kind = 0 if tl == 0 else (2 if tl == tiles_per_seq - 1 else 1)
                    q0 = t * Q_TILE
                    k0 = q0 - N_SIDE * kind
                    q2 = _split_heads(q_ref[0, q0:q0 + Q_TILE, lanes])
                    s = lax.dot_general(q2, k_ref[0, k0:k0 + K_TILE, lanes], _NT,
                                        preferred_element_type=F32)
                    s = s + jnp.concatenate([bias_ref[g, 2 * pair, kind],
                                             bias_ref[g, 2 * pair + 1, kind]], axis=0)
                    put(t, _softmax_pv(s, v_ref[0, k0:k0 + K_TILE, lanes]))
            else:
                bias = jnp.concatenate([bias_ref[g, 2 * pair, 0, :, :Q_TILE],
                                        bias_ref[g, 2 * pair + 1, 0, :, :Q_TILE]], axis=0)
                for t in range(0, N_Q_TILES, 2):
                    q0 = t * Q_TILE
                    q = q_ref[0, q0:q0 + 2 * Q_TILE, lanes]
                    q2 = jnp.concatenate([_split_heads(q[:Q_TILE]), _split_heads(q[Q_TILE:])], axis=0)
                    s = lax.dot_general(q2, k_ref[0, q0:q0 + 2 * Q_TILE, lanes], _NT,
                                        preferred_element_type=F32)
                    for i in range(2):
                        si = s[2 * i * Q_TILE:2 * (i + 1) * Q_TILE, i * Q_TILE:(i + 1) * Q_TILE]
                        v = v_ref[0, q0 + i * Q_TILE:q0 + (i + 1) * Q_TILE, lanes]
                        put(t + i, _softmax_pv(si + bias, v))

        def merge_body(t, carry, lanes=lanes):
            m = [_load_tokens(m_ref, g, t) for g in range(N_GROUPS)]
            top = jnp.maximum(jnp.maximum(m[0], m[1]), m[2])
            w = [jnp.exp(x - top) for x in m]
            num = sum(w[g] * _load_tokens(acc_ref, g, t) for g in range(N_GROUPS))
            den = sum(w[g] * _load_tokens(l_ref, g, t) for g in range(N_GROUPS))
            rows = pl.ds(pl.multiple_of(t * Q_TILE, Q_TILE), Q_TILE)
            o_ref[0, rows, lanes] = (num / den).astype(BF16)
            return carry

        lax.fori_loop(0, N_Q_TILES, merge_body, 0)


def _attention(qkv, rel_bias):
    n_hp = N_HEADS // HEADS_PER_STEP
    blk = (1, SEQ, STEP_LANES)
    specs = [pl.BlockSpec(memory_space=pltpu.SMEM)]
    args = [rel_bias]
    for g in range(N_GROUPS):
        for part in range(3):
            specs.append(pl.BlockSpec(blk, lambda hp, b, part=part: (b, 0, part * n_hp + hp)))
            args.append(qkv[g])
    return pl.pallas_call(
        _attn_kernel,
        grid=(n_hp, BATCH),
        in_specs=specs,
        out_specs=pl.BlockSpec(blk, lambda hp, b: (b, 0, hp)),
        out_shape=jax.ShapeDtypeStruct((BATCH, SEQ, D_MODEL), BF16),
        scratch_shapes=[pltpu.VMEM((N_GROUPS, HEADS_PER_STEP, N_KINDS, Q_TILE, K_TILE), F32),
                        pltpu.VMEM((sum(GROUP_ROWS), LANES), F32),
                        pltpu.VMEM((sum(GROUP_ROWS), LANES), F32),
                        pltpu.VMEM((sum(GROUP_ROWS), LANES), F32)],
        compiler_params=_params(("arbitrary", "arbitrary")),
        name="dilated_attention",
    )(*args)


MLP_TM = 512
MLP_FC = 1024
MLP_STAGES = 2 + 2 * (D_FF // MLP_FC)


_DONE = object()


def _interleave(order, **stage_generators):
    for key in order:
        next(stage_generators[key], _DONE)
    for gen in stage_generators.values():
        assert next(gen, _DONE) is _DONE, "order does not cover every stage"


def _mlp_stages(read_x, gpre_ref, gpost_ref, wup_ref, wdn_ref, o_ref):
    h = _rms(read_x(), gpre_ref[...]).astype(BF16)
    acc = jnp.zeros((MLP_TM, D_MODEL), F32)
    yield
    for c in range(D_FF // MLP_FC):
        cols = slice(c * MLP_FC, (c + 1) * MLP_FC)
        u = jnp.dot(h, wup_ref[:, cols].astype(BF16), preferred_element_type=F32)
        u = jnp.square(jnp.maximum(u, 0.0)).astype(BF16)
        yield
        acc = acc + jnp.dot(u, wdn_ref[cols, :].astype(BF16), preferred_element_type=F32)
        yield
    o_ref[...] = read_x() + _rms(acc, gpost_ref[...])


def _proj_mlp_kernel(a_ref, wo_ref, gmix_ref, x_ref, gpre_ref, gpost_ref, wup_ref, wdn_ref,
                     *refs, n_cast):
    f32_refs, o_ref, bf16_refs = refs[:n_cast], refs[n_cast], refs[n_cast + 1:]
    for src_ref, dst_ref in zip(f32_refs, bf16_refs):
        dst_ref[...] = src_ref[...].astype(BF16)
    m = jnp.dot(a_ref[...], wo_ref[...].astype(BF16), preferred_element_type=F32)
    x = x_ref[...] + _rms(m, gmix_ref[...])
    _interleave("m" * MLP_STAGES,
                m=_mlp_stages(lambda: x, gpre_ref, gpost_ref, wup_ref, wdn_ref, o_ref))


def _proj_mlp(x2d, a2d, w_o, g_mix, g_pre, g_post, w_up, w_down, to_bf16):
    n = x2d.shape[0]
    steps = n // MLP_TM
    resident = pl.Buffered(1)
    tile = pl.BlockSpec((MLP_TM, D_MODEL), lambda i: (i, 0))
    gain = pl.BlockSpec((1, D_MODEL), lambda i: (0, 0))
    cast_in, cast_out, cast_shape = [], [], []
    for w, layer in to_bf16:
        _, rows, cols = w.shape
        cast_in.append(pl.BlockSpec((None, rows // steps, cols), lambda i, layer=layer: (layer, i, 0)))
        cast_out.append(pl.BlockSpec((rows // steps, cols), lambda i: (i, 0)))
        cast_shape.append(jax.ShapeDtypeStruct((rows, cols), BF16))
    out = pl.pallas_call(
        functools.partial(_proj_mlp_kernel, n_cast=len(to_bf16)),
        grid=(steps,),
        in_specs=[tile,
                  pl.BlockSpec((None, D_MODEL, D_MODEL), lambda i: (0, 0, 0), pipeline_mode=resident),
                  gain, tile, gain, gain,
                  pl.BlockSpec((None, D_MODEL, D_FF), lambda i: (0, 0, 0), pipeline_mode=resident),
                  pl.BlockSpec((None, D_FF, D_MODEL), lambda i: (0, 0, 0), pipeline_mode=resident)]
        + cast_in,
        out_specs=[tile] + cast_out,
        out_shape=[jax.ShapeDtypeStruct((n, D_MODEL), F32)] + cast_shape,
        compiler_params=_params(("arbitrary",)),
        name="attn_proj_mlp",
    )(a2d, w_o, g_mix.reshape(1, D_MODEL), x2d, g_pre.reshape(1, D_MODEL),
      g_post.reshape(1, D_MODEL), w_up, w_down, *[w for w, _ in to_bf16])
    return out[0], out[1:]


CONV_TS = 512
HALO = 16
CONV_ROWS = CONV_TS + 2 * HALO
CONV_RB = 128
CONV_NC = 256


N_TILES = BATCH * SEQ // CONV_TS
TILES_PER_SEQ = SEQ // CONV_TS
assert CONV_TS == MLP_TM


def _conv_stages(xm_ref, xt_ref, xb_ref, first, last, gpre_ref, w1_ref, b1_ref, wdw_ref, bdw_ref,
                 lng_ref, lnb_ref, w2_ref, b2_ref, gpost_ref, o_ref, h_ref, u_ref, c_ref):
    gpre = gpre_ref[...]
    h_ref[0:HALO, :] = _rms(xt_ref[0], gpre).astype(BF16)
    h_ref[HALO:HALO + CONV_TS, :] = _rms(xm_ref[0], gpre).astype(BF16)
    h_ref[HALO + CONV_TS:, :] = _rms(xb_ref[0], gpre).astype(BF16)
    yield

    h = h_ref[...]
    half = CONV_RB // 2
    for nc in range(D_MODEL // CONV_NC):
        cols = slice(nc * CONV_NC, (nc + 1) * CONV_NC)
        gcols = slice(D_MODEL + nc * CONV_NC, D_MODEL + (nc + 1) * CONV_NC)
        a = jnp.dot(h, w1_ref[:, cols], preferred_element_type=F32) + b1_ref[:, cols]
        gate = jnp.dot(h, w1_ref[:, gcols], preferred_element_type=F32) + b1_ref[:, gcols]
        u = a * jax.nn.sigmoid(gate)
        for k in range(CONV_NC // LANES):
            lc = nc * CONV_NC // LANES + k
            lanes = slice(lc * LANES, (lc + 1) * LANES)
            uk = u[:, k * LANES:(k + 1) * LANES]
            u_ref[lc, 0:HALO, :] = jnp.where(first, 0.0, uk[0:HALO])
            u_ref[lc, HALO:HALO + CONV_TS, :] = uk[HALO:HALO + CONV_TS]
            u_ref[lc, HALO + CONV_TS:, :] = jnp.where(last, 0.0, uk[HALO + CONV_TS:])
        yield
        for k in range(CONV_NC // LANES):
            lc = nc * CONV_NC // LANES + k
            lanes = slice(lc * LANES, (lc + 1) * LANES)
            for r0 in range(0, CONV_TS, CONV_RB):
                for phase in range(2):
                    acc = bdw_ref[:, lanes]
                    for t in range(CONV_WIDTH):
                        start = r0 + phase + t + HALO - CONV_WIDTH // 2
                        acc = acc + (u_ref[lc, pl.ds(start, half, stride=2), :]
                                     * wdw_ref[t:t + 1, lanes])
                    c_ref[lc, pl.ds(r0 + phase, half, stride=2), :] = acc
            yield

    v = jnp.concatenate([c_ref[lc] for lc in range(D_MODEL // LANES)], axis=1)
    mu = jnp.mean(v, axis=-1, keepdims=True)
    var = jnp.mean(jnp.square(v - mu), axis=-1, keepdims=True)
    y = (v - mu) * lax.rsqrt(var + LN_EPS) * lng_ref[...] + lnb_ref[...]
    y = (y * jax.nn.sigmoid(y)).astype(BF16)
    yield
    z = jnp.dot(y, w2_ref[...], preferred_element_type=F32) + b2_ref[...]
    o_ref[...] = xm_ref[0] + _rms(z, gpost_ref[...])


def _conv_mlp_kernel(xm_ref, xt_ref, xb_ref, gpre_ref, w1_ref, b1_ref, wdw_ref, bdw_ref, lng_ref,
                     lnb_ref, w2_ref, b2_ref, gpost_ref, g2pre_ref, g2post_ref, wup_ref, wdn_ref,
                     o_ref, h_ref, u_ref, c_ref, x_ref, xnew_ref):
    s = pl.program_id(0)

    @pl.when(s == 0)
    def _():
        x_ref[...] = jnp.zeros((CONV_TS, D_MODEL), F32)

    j = jnp.minimum(s, N_TILES - 1) % TILES_PER_SEQ
    order = "ccccc" + "mm" + "cmccm" * 2 + "cmcm" * 2
    _interleave(
        order,
        c=_conv_stages(xm_ref, xt_ref, xb_ref, j == 0, j == TILES_PER_SEQ - 1, gpre_ref, w1_ref,
                       b1_ref, wdw_ref, bdw_ref, lng_ref, lnb_ref, w2_ref, b2_ref, gpost_ref,
                       xnew_ref, h_ref, u_ref, c_ref),
        m=_mlp_stages(lambda: x_ref[...], g2pre_ref, g2post_ref, wup_ref, wdn_ref, o_ref))
    x_ref[...] = xnew_ref[...]


def _conv_mlp_layer(x, g_pre, w1, b1, wdw, bdw, lng, lnb, w2, b2, g_post, g2_pre, g2_post,
                    w_up, w_down):
    row = lambda v: v.reshape(1, -1)
    resident = pl.Buffered(1)
    const = lambda shape, **kw: pl.BlockSpec(shape, lambda s: (0,) * len(shape), **kw)
    halo_blocks = CONV_TS // HALO

    def tile(s):
        t = jnp.minimum(s, N_TILES - 1)
        return t // TILES_PER_SEQ, t % TILES_PER_SEQ

    def main_map(s):
        b, j = tile(s)
        return b, j, 0

    def top_map(s):
        b, j = tile(s)
        return b, jnp.maximum(j * halo_blocks - 1, 0), 0

    def bot_map(s):
        b, j = tile(s)
        return b, jnp.minimum((j + 1) * halo_blocks, SEQ // HALO - 1), 0

    gain = const((1, D_MODEL))
    return pl.pallas_call(
        _conv_mlp_kernel,
        grid=(N_TILES + 1,),
        in_specs=[pl.BlockSpec((1, CONV_TS, D_MODEL), main_map),
                  pl.BlockSpec((1, HALO, D_MODEL), top_map),
                  pl.BlockSpec((1, HALO, D_MODEL), bot_map),
                  gain,
                  const((D_MODEL, 2 * D_MODEL), pipeline_mode=resident),
                  const((1, 2 * D_MODEL)),
                  const((CONV_WIDTH, D_MODEL)),
                  gain, gain, gain,
                  const((D_MODEL, D_MODEL), pipeline_mode=resident),
                  gain, gain, gain, gain,
                  const((D_MODEL, D_FF), pipeline_mode=resident),
                  const((D_FF, D_MODEL), pipeline_mode=resident)],
        out_specs=pl.BlockSpec((CONV_TS, D_MODEL), lambda s: (jnp.maximum(s - 1, 0), 0)),
        out_shape=jax.ShapeDtypeStruct((BATCH * SEQ, D_MODEL), F32),
        scratch_shapes=[pltpu.VMEM((CONV_ROWS, D_MODEL), BF16),
                        pltpu.VMEM((D_MODEL // LANES, CONV_ROWS, LANES), F32),
                        pltpu.VMEM((D_MODEL // LANES, CONV_TS, LANES), F32),
                        pltpu.VMEM((CONV_TS, D_MODEL), F32),
                        pltpu.VMEM((CONV_TS, D_MODEL), F32)],
        compiler_params=_params(("arbitrary",)),
        name="conformer_conv_mlp",
    )(x, x, x, row(g_pre), w1, row(b1), wdw, row(bdw), row(lng), row(lnb), w2, row(b2),
      row(g_post), row(g2_pre), row(g2_post), w_up, w_down)


def kernel(x, rel_bias, norm_mix_pre, norm_mix_post, norm_mlp_pre, norm_mlp_post, attn_w_qkv,
           attn_w_o, conv_w_pw1, conv_b_pw1, conv_w_dw, conv_b_dw, conv_ln_g, conv_ln_b,
           conv_w_pw2, conv_b_pw2, mlp_w_up, mlp_w_down):
    n_tok = BATCH * SEQ
    x2d = x.reshape(n_tok, D_MODEL)

    h = _prenorm(x, norm_mix_pre[0])
    w_qkv = attn_w_qkv.reshape(D_MODEL, N_GROUPS * GROUP_WIDTH)
    qkv = [_qkv_proj(h[g], w_qkv, g) for g in range(N_GROUPS)]
    a = _attention(qkv, rel_bias)
    x2d, (w_pw1, w_pw2, w_up1, w_down1) = _proj_mlp(
        x2d, a.reshape(n_tok, D_MODEL), attn_w_o, norm_mix_post[0], norm_mlp_pre[0],
        norm_mlp_post[0], mlp_w_up, mlp_w_down,
        to_bf16=[(conv_w_pw1, 0), (conv_w_pw2, 0), (mlp_w_up, 1), (mlp_w_down, 1)])

    x2d = _conv_mlp_layer(x2d.reshape(BATCH, SEQ, D_MODEL), norm_mix_pre[1], w_pw1, conv_b_pw1[0],
                          conv_w_dw[0], conv_b_dw[0], conv_ln_g[0], conv_ln_b[0], w_pw2,
                          conv_b_pw2[0], norm_mix_post[1], norm_mlp_pre[1], norm_mlp_post[1],
                          w_up1, w_down1)
    return x2d.reshape(BATCH, SEQ, D_MODEL)
```

```python
import functools
import math

import jax
import jax.numpy as jnp
from jax import lax
from jax.experimental import pallas as pl
from jax.experimental.pallas import tpu as pltpu

D_MODEL = 1024
BATCH = 8
SEQ = 2048
HEAD_DIM = 64
N_HEADS = 16
DILATIONS = (1, 4, 16)
N_SIDE = 64
N_GROUPS = 3
GROUP_WIDTH = 3 * N_HEADS * HEAD_DIM
N_BUCKETS = 32
MAX_DISTANCE = 1024
CONV_WIDTH = 31
D_FF = 4 * D_MODEL
RMS_EPS = 1e-6
LN_EPS = 1e-5
NEG_INF = -1e30

F32 = jnp.float32
BF16 = jnp.bfloat16

LANES = 128
Q_TILE = 128
K_TILE = Q_TILE + 2 * N_SIDE
VMEM_LIMIT = 56 * 1024 * 1024


def _rms(x, g):
    return x * lax.rsqrt(jnp.mean(x * x, axis=-1, keepdims=True) + RMS_EPS) * g


def _params(semantics):
    return pltpu.CompilerParams(dimension_semantics=semantics, vmem_limit_bytes=VMEM_LIMIT)


QKV_TN = 1024
QKV_TM = 512
QKV_NJ = GROUP_WIDTH // QKV_TN
N_SLABS = D_MODEL // LANES
PIECE = 256
N_PIECES = SEQ // PIECE
PIECES_PER_STEP = -(-N_PIECES // QKV_NJ)


def _aligned(v, m):
    return v if isinstance(v, int) else pl.multiple_of(v, m)


def _build_lhs(x_refs, g_ref, lhs_ref, slot, piece, r):
    L = SEQ // r
    if r == 1:
        parts = [(piece * PIECE, PIECE)]
    elif L >= PIECE:
        per_res = L // PIECE
        parts = [(piece // per_res + r * ((piece % per_res) * PIECE), PIECE)]
    else:
        n_res = PIECE // L
        parts = [(piece * n_res + i, L) for i in range(n_res)]
    row0 = _aligned(piece * PIECE, PIECE)
    for start, n in parts:
        rows = pl.ds(start, n) if r == 1 else pl.ds(start, n, stride=r)
        xs = [x_ref[0, rows, :] for x_ref in x_refs]
        ss = jnp.sum(sum(x * x for x in xs), axis=-1, keepdims=True)
        scale = lax.rsqrt(ss * (1.0 / D_MODEL) + RMS_EPS)
        for k, x in enumerate(xs):
            lanes = slice(k * LANES, (k + 1) * LANES)
            lhs_ref[slot, pl.ds(row0, n), lanes] = (x * scale * g_ref[:, lanes]).astype(BF16)
        row0 = _aligned(row0 + n, n)


def _qkv_kernel(*refs, r):
    x_refs = refs[:N_SLABS]
    g_ref, w_ref, o_ref, wb_ref, lhs_ref = refs[N_SLABS:]
    b, j = pl.program_id(0), pl.program_id(1)
    slot, other = b % 2, (b + 1) % 2

    @pl.when((b == 0) & (j == 0))
    def _():
        for piece in range(N_PIECES):
            _build_lhs(x_refs, g_ref, lhs_ref, 0, piece, r)

    @pl.when(b == 0)
    def _():
        wb_ref[j] = w_ref[...].astype(BF16)

    for m in range(SEQ // QKV_TM):
        rows = slice(m * QKV_TM, (m + 1) * QKV_TM)
        o_ref[0, rows, :] = jnp.dot(lhs_ref[slot, rows, :], wb_ref[j],
                                    preferred_element_type=F32).astype(BF16)

    for i in range(PIECES_PER_STEP):
        piece = jnp.minimum(j * PIECES_PER_STEP + i, N_PIECES - 1)
        _build_lhs(x_refs, g_ref, lhs_ref, other, piece, r)

    @pl.when((b == 0) & (j == 1))
    def _():
        for piece in range(PIECES_PER_STEP):
            _build_lhs(x_refs, g_ref, lhs_ref, 1, piece, r)


def _qkv_proj(x, g_pre, w_qkv, g):
    def x_map(k):
        def index(b, j):
            first = jnp.logical_and(b == 0, j == 0)
            return jnp.where(first, 0, jnp.minimum(b + 1, BATCH - 1)), 0, k
        return index

    def w_map(b, j):
        return 0, g * QKV_NJ + jnp.where(b == 0, j, QKV_NJ - 1)

    return pl.pallas_call(
        functools.partial(_qkv_kernel, r=DILATIONS[g]),
        grid=(BATCH, QKV_NJ),
        in_specs=[pl.BlockSpec((1, SEQ, LANES), x_map(k)) for k in range(N_SLABS)]
        + [pl.BlockSpec((1, D_MODEL), lambda b, j: (0, 0)),
           pl.BlockSpec((D_MODEL, QKV_TN), w_map)],
        out_specs=pl.BlockSpec((1, SEQ, QKV_TN), lambda b, j: (b, 0, j)),
        out_shape=jax.ShapeDtypeStruct((BATCH, SEQ, GROUP_WIDTH), BF16),
        scratch_shapes=[pltpu.VMEM((QKV_NJ, D_MODEL, QKV_TN), BF16),
                        pltpu.VMEM((2, SEQ, D_MODEL), BF16)],
        compiler_params=_params(("arbitrary", "arbitrary")),
        name=f"qkv_proj_g{g}",
    )(*([x] * N_SLABS), g_pre.reshape(1, D_MODEL), w_qkv)


HEADS_PER_STEP = 4
PAIRS_PER_STEP = HEADS_PER_STEP // 2
STEP_LANES = HEADS_PER_STEP * HEAD_DIM
N_Q_TILES = SEQ // Q_TILE
N_KINDS = 3
PADDED_PITCH = 24
GROUP_ROWS = tuple(SEQ // 16 * PADDED_PITCH if r % 16 == 0 else SEQ for r in DILATIONS)
GROUP_BASE = tuple(sum(GROUP_ROWS[:g]) for g in range(N_GROUPS))


def _bias_row(rel_ref, g, head):
    r = DILATIONS[g]
    nb = N_BUCKETS // 2
    max_exact = nb // 2
    delta = lax.broadcasted_iota(jnp.int32, (8, K_TILE), 1) - N_SIDE
    rel = delta * r
    n = jnp.abs(rel)
    nf = jnp.maximum(n, 1).astype(F32)
    large = max_exact + (jnp.log(nf / max_exact) / math.log(MAX_DISTANCE / max_exact)
                         * (nb - max_exact)).astype(jnp.int32)
    large = jnp.minimum(large, nb - 1)
    bucket = jnp.where(rel > 0, nb, 0) + jnp.where(n < max_exact, n, large)
    u = jnp.zeros((8, K_TILE), F32)
    for b in range(N_BUCKETS):
        u = jnp.where(bucket == b, rel_ref[b, g * N_HEADS + head], u)
    return jnp.where(jnp.abs(delta) <= N_SIDE, u, NEG_INF)


def _build_bias_tiles(rel_ref, bias_ref, head0):
    qi = lax.broadcasted_iota(jnp.int32, (Q_TILE, K_TILE), 0)
    kj = lax.broadcasted_iota(jnp.int32, (Q_TILE, K_TILE), 1)
    for g in range(N_GROUPS):
        for h in range(HEADS_PER_STEP):
            u = _bias_row(rel_ref, g, head0 + h)
            ub = jnp.broadcast_to(u[0:1, :], (Q_TILE, K_TILE))
            for kind in range(N_KINDS):
                shift = (N_SIDE * kind - N_SIDE) % K_TILE
                t = pltpu.roll(ub, shift, 1, stride=1, stride_axis=0)
                in_band = jnp.abs(kj - qi - N_SIDE * kind) <= N_SIDE
                bias_ref[g, h, kind] = jnp.where(in_band, t, NEG_INF)


def _token_rows(g, c, l0):
    r = DILATIONS[g]
    if r == 1:
        return pl.ds(GROUP_BASE[g] + l0, Q_TILE)
    pitch = PADDED_PITCH if r % 16 == 0 else r
    return pl.ds(GROUP_BASE[g] + l0 * pitch + c, Q_TILE, stride=pitch)


def _load_tokens(ref, g, t):
    r = DILATIONS[g]
    if r % 16:
        return ref[pl.ds(pl.multiple_of(GROUP_BASE[g] + t * Q_TILE, Q_TILE), Q_TILE), :]
    n = Q_TILE // r
    base = pl.multiple_of(GROUP_BASE[g] + t * n * PADDED_PITCH, 8)
    return jnp.concatenate([ref[pl.ds(base + i * PADDED_PITCH, r), :] for i in range(n)], axis=0)


def _split_heads(q):
    first = lax.broadcasted_iota(jnp.int32, (1, LANES), 1) < HEAD_DIM
    scale = HEAD_DIM ** -0.5
    m0 = jnp.where(first, scale, 0.0).astype(BF16)
    m1 = jnp.where(first, 0.0, scale).astype(BF16)
    return jnp.concatenate([q * m0, q * m1], axis=0)


def _softmax_pv(s, v):
    first = lax.broadcasted_iota(jnp.int32, (1, LANES), 1) < HEAD_DIM
    m = jnp.max(s, axis=-1, keepdims=True)
    p = jnp.exp(s - m).astype(BF16)
    v1 = jnp.concatenate([v, jnp.ones((v.shape[0], LANES), BF16)], axis=1)
    o2 = jnp.dot(p, v1, preferred_element_type=F32)
    acc = jnp.where(first, o2[:Q_TILE, :LANES], o2[Q_TILE:, :LANES])
    l = jnp.where(first, o2[:Q_TILE, LANES:], o2[Q_TILE:, LANES:])
    m_t = jnp.where(first, m[:Q_TILE], m[Q_TILE:])
    return acc, m_t, l


_NT = (((1,), (1,)), ((), ()))


def _attn_kernel(rel_ref, q0_ref, k0_ref, v0_ref, q1_ref, k1_ref, v1_ref, q2_ref, k2_ref, v2_ref,
                 o_ref, bias_ref, acc_ref, m_ref, l_ref):
    hp = pl.program_id(0)

    @pl.when(pl.program_id(1) == 0)
    def _():
        _build_bias_tiles(rel_ref, bias_ref, hp * HEADS_PER_STEP)

    qkv = ((q0_ref, k0_ref, v0_ref), (q1_ref, k1_ref, v1_ref), (q2_ref, k2_ref, v2_ref))
    for pair in range(PAIRS_PER_STEP):
        lanes = slice(pair * LANES, (pair + 1) * LANES)
        for g in range(N_GROUPS):
            r = DILATIONS[g]
            tiles_per_seq = SEQ // r // Q_TILE
            q_ref, k_ref, v_ref = qkv[g]

            def put(t, res, g=g, tiles_per_seq=tiles_per_seq):
                rows = _token_rows(g, t // tiles_per_seq, (t % tiles_per_seq) * Q_TILE)
                for ref, val in zip((acc_ref, m_ref, l_ref), res):
                    ref[rows, :] = val

            if tiles_per_seq > 1:
                for t in range(N_Q_TILES):
                    tl = t % tiles_per_seq
                    kind = 0 if tl == 0 else (2 if tl == tiles_per_seq - 1 else 1)
                    q0 = t * Q_TILE
                    k0 = q0 - N_SIDE * kind
                    q2 = _split_heads(q_ref[0, q0:q0 + Q_TILE, lanes])
                    s = lax.dot_general(q2, k_ref[0, k0:k0 + K_TILE, lanes], _NT,
                                        preferred_element_type=F32)
                    s = s + jnp.concatenate([bias_ref[g, 2 * pair, kind],
                                             bias_ref[g, 2 * pair + 1, kind]], axis=0)
                    put(t, _softmax_pv(s, v_ref[0, k0:k0 + K_TILE, lanes]))
            else:
                bias = jnp.concatenate([bias_ref[g, 2 * pair, 0, :, :Q_TILE],
                                        bias_ref[g, 2 * pair + 1, 0, :, :Q_TILE]], axis=0)
                for t in range(0, N_Q_TILES, 2):
                    q0 = t * Q_TILE
                    q = q_ref[0, q0:q0 + 2 * Q_TILE, lanes]
                    q2 = jnp.concatenate([_split_heads(q[:Q_TILE]), _split_heads(q[Q_TILE:])], axis=0)
                    s = lax.dot_general(q2, k_ref[0, q0:q0 + 2 * Q_TILE, lanes], _NT,
                                        preferred_element_type=F32)
                    for i in range(2):
                        si = s[2 * i * Q_TILE:2 * (i + 1) * Q_TILE, i * Q_TILE:(i + 1) * Q_TILE]
                        v = v_ref[0, q0 + i * Q_TILE:q0 + (i + 1) * Q_TILE, lanes]
                        put(t + i, _softmax_pv(si + bias, v))

        def merge_body(t, carry, lanes=lanes):
            m = [_load_tokens(m_ref, g, t) for g in range(N_GROUPS)]
            top = jnp.maximum(jnp.maximum(m[0], m[1]), m[2])
            w = [jnp.exp(x - top) for x in m]
            num = sum(w[g] * _load_tokens(acc_ref, g, t) for g in range(N_GROUPS))
            den = sum(w[g] * _load_tokens(l_ref, g, t) for g in range(N_GROUPS))
            rows = pl.ds(pl.multiple_of(t * Q_TILE, Q_TILE), Q_TILE)
            o_ref[0, rows, lanes] = (num / den).astype(BF16)
            return carry

        lax.fori_loop(0, N_Q_TILES, merge_body, 0)


def _attention(qkv, rel_bias):
    n_hp = N_HEADS // HEADS_PER_STEP
    blk = (1, SEQ, STEP_LANES)
    specs = [pl.BlockSpec(memory_space=pltpu.SMEM)]
    args = [rel_bias]
    for g in range(N_GROUPS):
        for part in range(3):
            specs.append(pl.BlockSpec(blk, lambda hp, b, part=part: (b, 0, part * n_hp + hp)))
            args.append(qkv[g])
    return pl.pallas_call(
        _attn_kernel,
        grid=(n_hp, BATCH),
        in_specs=specs,
        out_specs=pl.BlockSpec(blk, lambda hp, b: (b, 0, hp)),
        out_shape=jax.ShapeDtypeStruct((BATCH, SEQ, D_MODEL), BF16),
        scratch_shapes=[pltpu.VMEM((N_GROUPS, HEADS_PER_STEP, N_KINDS, Q_TILE, K_TILE), F32),
                        pltpu.VMEM((sum(GROUP_ROWS), LANES), F32),
                        pltpu.VMEM((sum(GROUP_ROWS), LANES), F32),
                        pltpu.VMEM((sum(GROUP_ROWS), LANES), F32)],
        compiler_params=_params(("arbitrary", "arbitrary")),
        name="dilated_attention",
    )(*args)


MLP_TM = 512
MLP_FC = 1024
MLP_STAGES = 2 + 2 * (D_FF // MLP_FC)


_DONE = object()


def _interleave(order, **stage_generators):
    for key in order:
        next(stage_generators[key], _DONE)
    for gen in stage_generators.values():
        assert next(gen, _DONE) is _DONE, "order does not cover every stage"


def _mlp_stages(read_x, gpre_ref, gpost_ref, wup_ref, wdn_ref, o_ref):
    h = _rms(read_x(), gpre_ref[...]).astype(BF16)
    acc = jnp.zeros((MLP_TM, D_MODEL), F32)
    yield
    for c in range(D_FF // MLP_FC):
        cols = slice(c * MLP_FC, (c + 1) * MLP_FC)
        u = jnp.dot(h, wup_ref[:, cols].astype(BF16), preferred_element_type=F32)
        u = jnp.square(jnp.maximum(u, 0.0)).astype(BF16)
        yield
        acc = acc + jnp.dot(u, wdn_ref[cols, :].astype(BF16), preferred_element_type=F32)
        yield
    o_ref[...] = read_x() + _rms(acc, gpost_ref[...])


def _proj_mlp_kernel(a_ref, wo_ref, gmix_ref, x_ref, gpre_ref, gpost_ref, wup_ref, wdn_ref,
                     *refs, n_cast):
    f32_refs, o_ref, bf16_refs = refs[:n_cast], refs[n_cast], refs[n_cast + 1:]
    for src_ref, dst_ref in zip(f32_refs, bf16_refs):
        dst_ref[...] = src_ref[...].astype(BF16)
    m = jnp.dot(a_ref[...], wo_ref[...].astype(BF16), preferred_element_type=F32)
    x = x_ref[...] + _rms(m, gmix_ref[...])
    _interleave("m" * MLP_STAGES,
                m=_mlp_stages(lambda: x, gpre_ref, gpost_ref, wup_ref, wdn_ref, o_ref))


def _proj_mlp(x2d, a2d, w_o, g_mix, g_pre, g_post, w_up, w_down, to_bf16):
    n = x2d.shape[0]
    steps = n // MLP_TM
    resident = pl.Buffered(1)
    tile = pl.BlockSpec((MLP_TM, D_MODEL), lambda i: (i, 0))
    gain = pl.BlockSpec((1, D_MODEL), lambda i: (0, 0))
    cast_in, cast_out, cast_shape = [], [], []
    for w, layer in to_bf16:
        _, rows, cols = w.shape
        cast_in.append(pl.BlockSpec((None, rows // steps, cols), lambda i, layer=layer: (layer, i, 0)))
        cast_out.append(pl.BlockSpec((rows // steps, cols), lambda i: (i, 0)))
        cast_shape.append(jax.ShapeDtypeStruct((rows, cols), BF16))
    out = pl.pallas_call(
        functools.partial(_proj_mlp_kernel, n_cast=len(to_bf16)),
        grid=(steps,),
        in_specs=[tile,
                  pl.BlockSpec((None, D_MODEL, D_MODEL), lambda i: (0, 0, 0), pipeline_mode=resident),
                  gain, tile, gain, gain,
                  pl.BlockSpec((None, D_MODEL, D_FF), lambda i: (0, 0, 0), pipeline_mode=resident),
                  pl.BlockSpec((None, D_FF, D_MODEL), lambda i: (0, 0, 0), pipeline_mode=resident)]
        + cast_in,
        out_specs=[tile] + cast_out,
        out_shape=[jax.ShapeDtypeStruct((n, D_MODEL), F32)] + cast_shape,
        compiler_params=_params(("arbitrary",)),
        name="attn_proj_mlp",
    )(a2d, w_o, g_mix.reshape(1, D_MODEL), x2d, g_pre.reshape(1, D_MODEL),
      g_post.reshape(1, D_MODEL), w_up, w_down, *[w for w, _ in to_bf16])
    return out[0], out[1:]


CONV_TS = 512
HALO = 16
CONV_ROWS = CONV_TS + 2 * HALO
CONV_RB = 128
CONV_NC = 256


N_TILES = BATCH * SEQ // CONV_TS
TILES_PER_SEQ = SEQ // CONV_TS
assert CONV_TS == MLP_TM


def _conv_stages(xm_ref, xt_ref, xb_ref, first, last, gpre_ref, w1_ref, b1_ref, wdw_ref, bdw_ref,
                 lng_ref, lnb_ref, w2_ref, b2_ref, gpost_ref, o_ref, h_ref, u_ref, c_ref):
    gpre = gpre_ref[...]
    h_ref[0:HALO, :] = _rms(xt_ref[0], gpre).astype(BF16)
    h_ref[HALO:HALO + CONV_TS, :] = _rms(xm_ref[0], gpre).astype(BF16)
    h_ref[HALO + CONV_TS:, :] = _rms(xb_ref[0], gpre).astype(BF16)
    yield

    h = h_ref[...]
    half = CONV_RB // 2
    for nc in range(D_MODEL // CONV_NC):
        cols = slice(nc * CONV_NC, (nc + 1) * CONV_NC)
        gcols = slice(D_MODEL + nc * CONV_NC, D_MODEL + (nc + 1) * CONV_NC)
        a = jnp.dot(h, w1_ref[:, cols], preferred_element_type=F32) + b1_ref[:, cols]
        gate = jnp.dot(h, w1_ref[:, gcols], preferred_element_type=F32) + b1_ref[:, gcols]
        u = a * jax.nn.sigmoid(gate)
        for k in range(CONV_NC // LANES):
            lc = nc * CONV_NC // LANES + k
            lanes = slice(lc * LANES, (lc + 1) * LANES)
            uk = u[:, k * LANES:(k + 1) * LANES]
            u_ref[lc, 0:HALO, :] = jnp.where(first, 0.0, uk[0:HALO])
            u_ref[lc, HALO:HALO + CONV_TS, :] = uk[HALO:HALO + CONV_TS]
            u_ref[lc, HALO + CONV_TS:, :] = jnp.where(last, 0.0, uk[HALO + CONV_TS:])
        yield
        for k in range(CONV_NC // LANES):
            lc = nc * CONV_NC // LANES + k
            lanes = slice(lc * LANES, (lc + 1) * LANES)
            for r0 in range(0, CONV_TS, CONV_RB):
                for phase in range(2):
                    acc = bdw_ref[:, lanes]
                    for t in range(CONV_WIDTH):
                        start = r0 + phase + t + HALO - CONV_WIDTH // 2
                        acc = acc + (u_ref[lc, pl.ds(start, half, stride=2), :]
                                     * wdw_ref[t:t + 1, lanes])
                    c_ref[lc, pl.ds(r0 + phase, half, stride=2), :] = acc
            yield

    v = jnp.concatenate([c_ref[lc] for lc in range(D_MODEL // LANES)], axis=1)
    mu = jnp.mean(v, axis=-1, keepdims=True)
    var = jnp.mean(jnp.square(v - mu), axis=-1, keepdims=True)
    y = (v - mu) * lax.rsqrt(var + LN_EPS) * lng_ref[...] + lnb_ref[...]
    y = (y * jax.nn.sigmoid(y)).astype(BF16)
    yield
    z = jnp.dot(y, w2_ref[...], preferred_element_type=F32) + b2_ref[...]
    o_ref[...] = xm_ref[0] + _rms(z, gpost_ref[...])


def _conv_mlp_kernel(xm_ref, xt_ref, xb_ref, gpre_ref, w1_ref, b1_ref, wdw_ref, bdw_ref, lng_ref,
                     lnb_ref, w2_ref, b2_ref, gpost_ref, g2pre_ref, g2post_ref, wup_ref, wdn_ref,
                     o_ref, h_ref, u_ref, c_ref, x_ref, xnew_ref):
    s = pl.program_id(0)

    @pl.when(s == 0)
    def _():
        x_ref[...] = jnp.zeros((CONV_TS, D_MODEL), F32)

    j = jnp.minimum(s, N_TILES - 1) % TILES_PER_SEQ
    order = "ccccc" + "mm" + "cmccm" * 2 + "cmcm" * 2
    _interleave(
        order,
        c=_conv_stages(xm_ref, xt_ref, xb_ref, j == 0, j == TILES_PER_SEQ - 1, gpre_ref, w1_ref,
                       b1_ref, wdw_ref, bdw_ref, lng_ref, lnb_ref, w2_ref, b2_ref, gpost_ref,
                       xnew_ref, h_ref, u_ref, c_ref),
        m=_mlp_stages(lambda: x_ref[...], g2pre_ref, g2post_ref, wup_ref, wdn_ref, o_ref))
    x_ref[...] = xnew_ref[...]


def _conv_mlp_layer(x, g_pre, w1, b1, wdw, bdw, lng, lnb, w2, b2, g_post, g2_pre, g2_post,
                    w_up, w_down):
    row = lambda v: v.reshape(1, -1)
    resident = pl.Buffered(1)
    const = lambda shape, **kw: pl.BlockSpec(shape, lambda s: (0,) * len(shape), **kw)
    halo_blocks = CONV_TS // HALO

    def tile(s):
        t = jnp.minimum(s, N_TILES - 1)
        return t // TILES_PER_SEQ, t % TILES_PER_SEQ

    def main_map(s):
        b, j = tile(s)
        return b, j, 0

    def top_map(s):
        b, j = tile(s)
        return b, jnp.maximum(j * halo_blocks - 1, 0), 0

    def bot_map(s):
        b, j = tile(s)
        return b, jnp.minimum((j + 1) * halo_blocks, SEQ // HALO - 1), 0

    gain = const((1, D_MODEL))
    return pl.pallas_call(
        _conv_mlp_kernel,
        grid=(N_TILES + 1,),
        in_specs=[pl.BlockSpec((1, CONV_TS, D_MODEL), main_map),
                  pl.BlockSpec((1, HALO, D_MODEL), top_map),
                  pl.BlockSpec((1, HALO, D_MODEL), bot_map),
                  gain,
                  const((D_MODEL, 2 * D_MODEL), pipeline_mode=resident),
                  const((1, 2 * D_MODEL)),
                  const((CONV_WIDTH, D_MODEL)),
                  gain, gain, gain,
                  const((D_MODEL, D_MODEL), pipeline_mode=resident),
                  gain, gain, gain, gain,
                  const((D_MODEL, D_FF), pipeline_mode=resident),
                  const((D_FF, D_MODEL), pipeline_mode=resident)],
        out_specs=pl.BlockSpec((CONV_TS, D_MODEL), lambda s: (jnp.maximum(s - 1, 0), 0)),
        out_shape=jax.ShapeDtypeStruct((BATCH * SEQ, D_MODEL), F32),
        scratch_shapes=[pltpu.VMEM((CONV_ROWS, D_MODEL), BF16),
                        pltpu.VMEM((D_MODEL // LANES, CONV_ROWS, LANES), F32),
                        pltpu.VMEM((D_MODEL // LANES, CONV_TS, LANES), F32),
                        pltpu.VMEM((CONV_TS, D_MODEL), F32),
                        pltpu.VMEM((CONV_TS, D_MODEL), F32)],
        compiler_params=_params(("arbitrary",)),
        name="conformer_conv_mlp",
    )(x, x, x, row(g_pre), w1, row(b1), wdw, row(bdw), row(lng), row(lnb), w2, row(b2),
      row(g_post), row(g2_pre), row(g2_post), w_up, w_down)


def kernel(x, rel_bias, norm_mix_pre, norm_mix_post, norm_mlp_pre, norm_mlp_post, attn_w_qkv,
           attn_w_o, conv_w_pw1, conv_b_pw1, conv_w_dw, conv_b_dw, conv_ln_g, conv_ln_b,
           conv_w_pw2, conv_b_pw2, mlp_w_up, mlp_w_down):
    n_tok = BATCH * SEQ
    x2d = x.reshape(n_tok, D_MODEL)

    w_qkv = attn_w_qkv.reshape(D_MODEL, N_GROUPS * GROUP_WIDTH)
    qkv = [_qkv_proj(x, norm_mix_pre[0], w_qkv, g) for g in range(N_GROUPS)]
    a = _attention(qkv, rel_bias)
    x2d, (w_pw1, w_pw2, w_up1, w_down1) = _proj_mlp(
        x2d, a.reshape(n_tok, D_MODEL), attn_w_o, norm_mix_post[0], norm_mlp_pre[0],
        norm_mlp_post[0], mlp_w_up, mlp_w_down,
        to_bf16=[(conv_w_pw1, 0), (conv_w_pw2, 0), (mlp_w_up, 1), (mlp_w_down, 1)])

    x2d = _conv_mlp_layer(x2d.reshape(BATCH, SEQ, D_MODEL), norm_mix_pre[1], w_pw1, conv_b_pw1[0],
                          conv_w_dw[0], conv_b_dw[0], conv_ln_g[0], conv_ln_b[0], w_pw2,
                          conv_b_pw2[0], norm_mix_post[1], norm_mlp_pre[1], norm_mlp_post[1],
                          w_up1, w_down1)
    return x2d.reshape(BATCH, SEQ, D_MODEL)
```

```python
import functools
import math

import jax
import jax.numpy as jnp
from jax import lax
from jax.experimental import pallas as pl
from jax.experimental.pallas import tpu as pltpu

D_MODEL = 1024
BATCH = 8
SEQ = 2048
HEAD_DIM = 64
N_HEADS = 16
DILATIONS = (1, 4, 16)
N_SIDE = 64
N_GROUPS = 3
GROUP_WIDTH = 3 * N_HEADS * HEAD_DIM
N_BUCKETS = 32
MAX_DISTANCE = 1024
CONV_WIDTH = 31
D_FF = 4 * D_MODEL
RMS_EPS = 1e-6
LN_EPS = 1e-5
NEG_INF = -1e30

F32 = jnp.float32
BF16 = jnp.bfloat16

LANES = 128
Q_TILE = 128
K_TILE = Q_TILE + 2 * N_SIDE
VMEM_LIMIT = 56 * 1024 * 1024


def _rms(x, g):
    return x * lax.rsqrt(jnp.mean(x * x, axis=-1, keepdims=True) + RMS_EPS) * g


def _params(semantics):
    return pltpu.CompilerParams(dimension_semantics=semantics, vmem_limit_bytes=VMEM_LIMIT)


NORM_ROWS = 256


def _prenorm_kernel(x_ref, g_ref, *refs):
    out_refs, slab_ref = refs[:-1], refs[-1]

    def chunk(i, carry):
        rows = pl.ds(pl.multiple_of(i * NORM_ROWS, NORM_ROWS), NORM_ROWS)
        hn = _rms(x_ref[0, rows, :], g_ref[...])
        for k in range(D_MODEL // LANES):
            slab_ref[k, rows, :] = hn[:, k * LANES:(k + 1) * LANES]
        for out_ref, r in zip(out_refs, DILATIONS):
            if r == 1:
                out_ref[0, rows, :] = hn.astype(BF16)
        return carry

    lax.fori_loop(0, SEQ // NORM_ROWS, chunk, 0)
    for out_ref, r in zip(out_refs, DILATIONS):
        if r == 1:
            continue
        L = SEQ // r
        for c in range(r):
            for k in range(D_MODEL // LANES):
                out_ref[0, c * L:(c + 1) * L, k * LANES:(k + 1) * LANES] = (
                    slab_ref[k, pl.ds(c, L, stride=r), :].astype(BF16))


def _prenorm(x, g):
    blk = pl.BlockSpec((1, SEQ, D_MODEL), lambda b: (b, 0, 0))
    return pl.pallas_call(
        _prenorm_kernel,
        grid=(BATCH,),
        in_specs=[blk, pl.BlockSpec((1, D_MODEL), lambda b: (0, 0))],
        out_specs=[blk] * N_GROUPS,
        out_shape=[jax.ShapeDtypeStruct((BATCH, SEQ, D_MODEL), BF16)] * N_GROUPS,
        scratch_shapes=[pltpu.VMEM((D_MODEL // LANES, SEQ, LANES), F32)],
        compiler_params=_params(("arbitrary",)),
        name="prenorm",
    )(x, g.reshape(1, D_MODEL))


QKV_TN = 1024
QKV_TM = 512
assert QKV_TN == N_HEADS * HEAD_DIM
LOG2E = math.log2(math.e)
Q_SCALE = HEAD_DIM ** -0.5 * LOG2E


def _qkv_kernel(h_ref, w_ref, o_ref, wb_ref):
    @pl.when(pl.program_id(1) == 0)
    def _():
        wb_ref[...] = w_ref[...].astype(BF16)

    scale = jnp.where(pl.program_id(0) == 0, Q_SCALE, 1.0)
    for m in range(SEQ // QKV_TM):
        rows = slice(m * QKV_TM, (m + 1) * QKV_TM)
        res = jnp.dot(h_ref[0, rows, :], wb_ref[...], preferred_element_type=F32)
        o_ref[0, rows, :] = (res * scale).astype(BF16)


def _qkv_proj(h, w_qkv, g):
    nj = GROUP_WIDTH // QKV_TN
    return pl.pallas_call(
        _qkv_kernel,
        grid=(nj, BATCH),
        in_specs=[pl.BlockSpec((1, SEQ, D_MODEL), lambda j, b: (b, 0, 0)),
                  pl.BlockSpec((D_MODEL, QKV_TN), lambda j, b: (0, g * nj + j))],
        out_specs=pl.BlockSpec((1, SEQ, QKV_TN), lambda j, b: (b, 0, j)),
        out_shape=jax.ShapeDtypeStruct((BATCH, SEQ, GROUP_WIDTH), BF16),
        scratch_shapes=[pltpu.VMEM((D_MODEL, QKV_TN), BF16)],
        compiler_params=_params(("arbitrary", "arbitrary")),
        name=f"qkv_proj_g{g}",
    )(h, w_qkv)


HEADS_PER_STEP = 4
PAIRS_PER_STEP = HEADS_PER_STEP // 2
STEP_LANES = HEADS_PER_STEP * HEAD_DIM
N_Q_TILES = SEQ // Q_TILE
N_KINDS = 3
PADDED_PITCH = 24
GROUP_ROWS = tuple(SEQ // 16 * PADDED_PITCH if r % 16 == 0 else SEQ for r in DILATIONS)
GROUP_BASE = tuple(sum(GROUP_ROWS[:g]) for g in range(N_GROUPS))


def _bias_row(rel_ref, g, head):
    r = DILATIONS[g]
    nb = N_BUCKETS // 2
    max_exact = nb // 2
    delta = lax.broadcasted_iota(jnp.int32, (8, K_TILE), 1) - N_SIDE
    rel = delta * r
    n = jnp.abs(rel)
    nf = jnp.maximum(n, 1).astype(F32)
    large = max_exact + (jnp.log(nf / max_exact) / math.log(MAX_DISTANCE / max_exact)
                         * (nb - max_exact)).astype(jnp.int32)
    large = jnp.minimum(large, nb - 1)
    bucket = jnp.where(rel > 0, nb, 0) + jnp.where(n < max_exact, n, large)
    u = jnp.zeros((8, K_TILE), F32)
    for b in range(N_BUCKETS):
        u = jnp.where(bucket == b, rel_ref[b, g * N_HEADS + head], u)
    return jnp.where(jnp.abs(delta) <= N_SIDE, u * LOG2E, NEG_INF)


def _build_bias_tiles(rel_ref, bias_ref, head0):
    qi = lax.broadcasted_iota(jnp.int32, (Q_TILE, K_TILE), 0)
    kj = lax.broadcasted_iota(jnp.int32, (Q_TILE, K_TILE), 1)
    for g in range(N_GROUPS):
        for h in range(HEADS_PER_STEP):
            u = _bias_row(rel_ref, g, head0 + h)
            ub = jnp.broadcast_to(u[0:1, :], (Q_TILE, K_TILE))
            for kind in range(N_KINDS):
                shift = (N_SIDE * kind - N_SIDE) % K_TILE
                t = pltpu.roll(ub, shift, 1, stride=1, stride_axis=0)
                in_band = jnp.abs(kj - qi - N_SIDE * kind) <= N_SIDE
                bias_ref[g, h, kind] = jnp.where(in_band, t, NEG_INF)


def _token_rows(g, c, l0):
    r = DILATIONS[g]
    if r == 1:
        return pl.ds(GROUP_BASE[g] + l0, Q_TILE)
    pitch = PADDED_PITCH if r % 16 == 0 else r
    return pl.ds(GROUP_BASE[g] + l0 * pitch + c, Q_TILE, stride=pitch)


def _load_tokens(ref, g, t):
    r = DILATIONS[g]
    if r % 16:
        return ref[pl.ds(pl.multiple_of(GROUP_BASE[g] + t * Q_TILE, Q_TILE), Q_TILE), :]
    n = Q_TILE // r
    base = pl.multiple_of(GROUP_BASE[g] + t * n * PADDED_PITCH, 8)
    return jnp.concatenate([ref[pl.ds(base + i * PADDED_PITCH, r), :] for i in range(n)], axis=0)


def _split_heads(q):
    first = lax.broadcasted_iota(jnp.int32, (1, LANES), 1) < HEAD_DIM
    zero = jnp.zeros_like(q)
    return jnp.concatenate([jnp.where(first, q, zero), jnp.where(first, zero, q)], axis=0)


def _softmax_pv(s, v):
    first = lax.broadcasted_iota(jnp.int32, (1, LANES), 1) < HEAD_DIM
    m = jnp.max(s, axis=-1, keepdims=True)
    p = jnp.exp2(s - m).astype(BF16)
    v1 = jnp.concatenate([v, jnp.ones((v.shape[0], LANES), BF16)], axis=1)
    o2 = jnp.dot(p, v1, preferred_element_type=F32)
    acc = jnp.where(first, o2[:Q_TILE, :LANES], o2[Q_TILE:, :LANES])
    l = jnp.where(first, o2[:Q_TILE, LANES:], o2[Q_TILE:, LANES:])
    m_t = jnp.where(first, m[:Q_TILE], m[Q_TILE:])
    return acc, m_t, l


_NT = (((1,), (1,)), ((), ()))


def _attn_kernel(rel_ref, q0_ref, k0_ref, v0_ref, q1_ref, k1_ref, v1_ref, q2_ref, k2_ref, v2_ref,
                 o_ref, bias_ref, acc_ref, m_ref, l_ref):
    hp = pl.program_id(0)

    @pl.when(pl.program_id(1) == 0)
    def _():
        _build_bias_tiles(rel_ref, bias_ref, hp * HEADS_PER_STEP)

    qkv = ((q0_ref, k0_ref, v0_ref), (q1_ref, k1_ref, v1_ref), (q2_ref, k2_ref, v2_ref))
    for pair in range(PAIRS_PER_STEP):
        lanes = slice(pair * LANES, (pair + 1) * LANES)
        for g in range(N_GROUPS):
            r = DILATIONS[g]
            tiles_per_seq = SEQ // r // Q_TILE
            q_ref, k_ref, v_ref = qkv[g]

            def put(t, res, g=g, tiles_per_seq=tiles_per_seq):
                rows = _token_rows(g, t // tiles_per_seq, (t % tiles_per_seq) * Q_TILE)
                for ref, val in zip((acc_ref, m_ref, l_ref), res):
                    ref[rows, :] = val

            if tiles_per_seq > 1:
                for t in range(N_Q_TILES):
                    tl = t % tiles_per_seq
                    kind = 0 if tl == 0 else (2 if tl == tiles_per_seq - 1 else 1)
                    q0 = t * Q_TILE
                    k0 = q0 - N_SIDE * kind
                    q2 = _split_heads(q_ref[0, q0:q0 + Q_TILE, lanes])
                    s = lax.dot_general(q2, k_ref[0, k0:k0 + K_TILE, lanes], _NT,
                                        preferred_element_type=F32)
                    s = s + jnp.concatenate([bias_ref[g, 2 * pair, kind],
                                             bias_ref[g, 2 * pair + 1, kind]], axis=0)
                    put(t, _softmax_pv(s, v_ref[0, k0:k0 + K_TILE, lanes]))
            else:
                bias = jnp.concatenate([bias_ref[g, 2 * pair, 0, :, :Q_TILE],
                                        bias_ref[g, 2 * pair + 1, 0, :, :Q_TILE]], axis=0)
                for t in range(0, N_Q_TILES, 2):
                    q0 = t * Q_TILE
                    q = q_ref[0, q0:q0 + 2 * Q_TILE, lanes]
                    q2 = jnp.concatenate([_split_heads(q[:Q_TILE]), _split_heads(q[Q_TILE:])], axis=0)
                    s = lax.dot_general(q2, k_ref[0, q0:q0 + 2 * Q_TILE, lanes], _NT,
                                        preferred_element_type=F32)
                    for i in range(2):
                        si = s[2 * i * Q_TILE:2 * (i + 1) * Q_TILE, i * Q_TILE:(i + 1) * Q_TILE]
                        v = v_ref[0, q0 + i * Q_TILE:q0 + (i + 1) * Q_TILE, lanes]
                        put(t + i, _softmax_pv(si + bias, v))

        def merge_body(t, carry, lanes=lanes):
            m = [_load_tokens(m_ref, g, t) for g in range(N_GROUPS)]
            top = jnp.maximum(jnp.maximum(m[0], m[1]), m[2])
            w = [jnp.exp2(x - top) for x in m]
            num = sum(w[g] * _load_tokens(acc_ref, g, t) for g in range(N_GROUPS))
            den = sum(w[g] * _load_tokens(l_ref, g, t) for g in range(N_GROUPS))
            rows = pl.ds(pl.multiple_of(t * Q_TILE, Q_TILE), Q_TILE)
            o_ref[0, rows, lanes] = (num / den).astype(BF16)
            return carry

        lax.fori_loop(0, N_Q_TILES, merge_body, 0, unroll=4)


def _attention(qkv, rel_bias):
    n_hp = N_HEADS // HEADS_PER_STEP
    blk = (1, SEQ, STEP_LANES)
    specs = [pl.BlockSpec(memory_space=pltpu.SMEM)]
    args = [rel_bias]
    for g in range(N_GROUPS):
        for part in range(3):
            specs.append(pl.BlockSpec(blk, lambda hp, b, part=part: (b, 0, part * n_hp + hp)))
            args.append(qkv[g])
    return pl.pallas_call(
        _attn_kernel,
        grid=(n_hp, BATCH),
        in_specs=specs,
        out_specs=pl.BlockSpec(blk, lambda hp, b: (b, 0, hp)),
        out_shape=jax.ShapeDtypeStruct((BATCH, SEQ, D_MODEL), BF16),
        scratch_shapes=[pltpu.VMEM((N_GROUPS, HEADS_PER_STEP, N_KINDS, Q_TILE, K_TILE), F32),
                        pltpu.VMEM((sum(GROUP_ROWS), LANES), F32),
                        pltpu.VMEM((sum(GROUP_ROWS), LANES), F32),
                        pltpu.VMEM((sum(GROUP_ROWS), LANES), F32)],
        compiler_params=_params(("arbitrary", "arbitrary")),
        name="dilated_attention",
    )(*args)


MLP_TM = 512
MLP_FC = 1024
MLP_STAGES = 2 + 2 * (D_FF // MLP_FC)


_DONE = object()


def _interleave(order, **stage_generators):
    for key in order:
        next(stage_generators[key], _DONE)
    for gen in stage_generators.values():
        assert next(gen, _DONE) is _DONE, "order does not cover every stage"


def _mlp_stages(read_x, gpre_ref, gpost_ref, wup_ref, wdn_ref, o_ref):
    h = _rms(read_x(), gpre_ref[...]).astype(BF16)
    acc = jnp.zeros((MLP_TM, D_MODEL), F32)
    yield
    for c in range(D_FF // MLP_FC):
        cols = slice(c * MLP_FC, (c + 1) * MLP_FC)
        u = jnp.dot(h, wup_ref[:, cols].astype(BF16), preferred_element_type=F32)
        u = jnp.square(jnp.maximum(u, 0.0)).astype(BF16)
        yield
        acc = acc + jnp.dot(u, wdn_ref[cols, :].astype(BF16), preferred_element_type=F32)
        yield
    o_ref[...] = read_x() + _rms(acc, gpost_ref[...])


def _proj_mlp_kernel(a_ref, wo_ref, gmix_ref, x_ref, gpre_ref, gpost_ref, wup_ref, wdn_ref,
                     *refs, n_cast):
    f32_refs, o_ref, bf16_refs = refs[:n_cast], refs[n_cast], refs[n_cast + 1:]
    for src_ref, dst_ref in zip(f32_refs, bf16_refs):
        dst_ref[...] = src_ref[...].astype(BF16)
    m = jnp.dot(a_ref[...], wo_ref[...].astype(BF16), preferred_element_type=F32)
    x = x_ref[...] + _rms(m, gmix_ref[...])
    _interleave("m" * MLP_STAGES,
                m=_mlp_stages(lambda: x, gpre_ref, gpost_ref, wup_ref, wdn_ref, o_ref))


def _proj_mlp(x2d, a2d, w_o, g_mix, g_pre, g_post, w_up, w_down, to_bf16):
    n = x2d.shape[0]
    steps = n // MLP_TM
    resident = pl.Buffered(1)
    tile = pl.BlockSpec((MLP_TM, D_MODEL), lambda i: (i, 0))
    gain = pl.BlockSpec((1, D_MODEL), lambda i: (0, 0))
    cast_in, cast_out, cast_shape = [], [], []
    for w, layer in to_bf16:
        _, rows, cols = w.shape
        cast_in.append(pl.BlockSpec((None, rows // steps, cols), lambda i, layer=layer: (layer, i, 0)))
        cast_out.append(pl.BlockSpec((rows // steps, cols), lambda i: (i, 0)))
        cast_shape.append(jax.ShapeDtypeStruct((rows, cols), BF16))
    out = pl.pallas_call(
        functools.partial(_proj_mlp_kernel, n_cast=len(to_bf16)),
        grid=(steps,),
        in_specs=[tile,
                  pl.BlockSpec((None, D_MODEL, D_MODEL), lambda i: (0, 0, 0), pipeline_mode=resident),
                  gain, tile, gain, gain,
                  pl.BlockSpec((None, D_MODEL, D_FF), lambda i: (0, 0, 0), pipeline_mode=resident),
                  pl.BlockSpec((None, D_FF, D_MODEL), lambda i: (0, 0, 0), pipeline_mode=resident)]
        + cast_in,
        out_specs=[tile] + cast_out,
        out_shape=[jax.ShapeDtypeStruct((n, D_MODEL), F32)] + cast_shape,
        compiler_params=_params(("arbitrary",)),
        name="attn_proj_mlp",
    )(a2d, w_o, g_mix.reshape(1, D_MODEL), x2d, g_pre.reshape(1, D_MODEL),
      g_post.reshape(1, D_MODEL), w_up, w_down, *[w for w, _ in to_bf16])
    return out[0], out[1:]


CONV_TS = 512
HALO = 16
CONV_ROWS = CONV_TS + 2 * HALO
CONV_RB = 128
CONV_NC = 256


N_TILES = BATCH * SEQ // CONV_TS
TILES_PER_SEQ = SEQ // CONV_TS
assert CONV_TS == MLP_TM


def _conv_stages(xm_ref, xt_ref, xb_ref, first, last, gpre_ref, w1_ref, b1_ref, wdw_ref, bdw_ref,
                 lng_ref, lnb_ref, w2_ref, b2_ref, gpost_ref, o_ref, h_ref, u_ref, c_ref):
    gpre = gpre_ref[...]
    h_ref[0:HALO, :] = _rms(xt_ref[0], gpre).astype(BF16)
    h_ref[HALO:HALO + CONV_TS, :] = _rms(xm_ref[0], gpre).astype(BF16)
    h_ref[HALO + CONV_TS:, :] = _rms(xb_ref[0], gpre).astype(BF16)
    yield

    h = h_ref[...]
    half = CONV_RB // 2
    for nc in range(D_MODEL // CONV_NC):
        cols = slice(nc * CONV_NC, (nc + 1) * CONV_NC)
        gcols = slice(D_MODEL + nc * CONV_NC, D_MODEL + (nc + 1) * CONV_NC)
        a = jnp.dot(h, w1_ref[:, cols], preferred_element_type=F32) + b1_ref[:, cols]
        gate = jnp.dot(h, w1_ref[:, gcols], preferred_element_type=F32) + b1_ref[:, gcols]
        u = a * jax.nn.sigmoid(gate)
        for k in range(CONV_NC // LANES):
            lc = nc * CONV_NC // LANES + k
            lanes = slice(lc * LANES, (lc + 1) * LANES)
            uk = u[:, k * LANES:(k + 1) * LANES]
            u_ref[lc, 0:HALO, :] = jnp.where(first, 0.0, uk[0:HALO])
            u_ref[lc, HALO:HALO + CONV_TS, :] = uk[HALO:HALO + CONV_TS]
            u_ref[lc, HALO + CONV_TS:, :] = jnp.where(last, 0.0, uk[HALO + CONV_TS:])
        yield
        for k in range(CONV_NC // LANES):
            lc = nc * CONV_NC // LANES + k
            lanes = slice(lc * LANES, (lc + 1) * LANES)
            for r0 in range(0, CONV_TS, CONV_RB):
                for phase in range(2):
                    acc = bdw_ref[:, lanes]
                    for t in range(CONV_WIDTH):
                        start = r0 + phase + t + HALO - CONV_WIDTH // 2
                        acc = acc + (u_ref[lc, pl.ds(start, half, stride=2), :]
                                     * wdw_ref[t:t + 1, lanes])
                    c_ref[lc, pl.ds(r0 + phase, half, stride=2), :] = acc
            yield

    v = jnp.concatenate([c_ref[lc] for lc in range(D_MODEL // LANES)], axis=1)
    mu = jnp.mean(v, axis=-1, keepdims=True)
    var = jnp.mean(jnp.square(v - mu), axis=-1, keepdims=True)
    y = (v - mu) * lax.rsqrt(var + LN_EPS) * lng_ref[...] + lnb_ref[...]
    y = (y * jax.nn.sigmoid(y)).astype(BF16)
    yield
    z = jnp.dot(y, w2_ref[...], preferred_element_type=F32) + b2_ref[...]
    o_ref[...] = xm_ref[0] + _rms(z, gpost_ref[...])


def _conv_mlp_kernel(xm_ref, xt_ref, xb_ref, gpre_ref, w1_ref, b1_ref, wdw_ref, bdw_ref, lng_ref,
                     lnb_ref, w2_ref, b2_ref, gpost_ref, g2pre_ref, g2post_ref, wup_ref, wdn_ref,
                     o_ref, h_ref, u_ref, c_ref, x_ref, xnew_ref):
    s = pl.program_id(0)

    @pl.when(s == 0)
    def _():
        x_ref[...] = jnp.zeros((CONV_TS, D_MODEL), F32)

    j = jnp.minimum(s, N_TILES - 1) % TILES_PER_SEQ
    order = "ccccc" + "mm" + "cmccm" * 2 + "cmcm" * 2
    _interleave(
        order,
        c=_conv_stages(xm_ref, xt_ref, xb_ref, j == 0, j == TILES_PER_SEQ - 1, gpre_ref, w1_ref,
                       b1_ref, wdw_ref, bdw_ref, lng_ref, lnb_ref, w2_ref, b2_ref, gpost_ref,
                       xnew_ref, h_ref, u_ref, c_ref),
        m=_mlp_stages(lambda: x_ref[...], g2pre_ref, g2post_ref, wup_ref, wdn_ref, o_ref))
    x_ref[...] = xnew_ref[...]


def _conv_mlp_layer(x, g_pre, w1, b1, wdw, bdw, lng, lnb, w2, b2, g_post, g2_pre, g2_post,
                    w_up, w_down):
    row = lambda v: v.reshape(1, -1)
    resident = pl.Buffered(1)
    const = lambda shape, **kw: pl.BlockSpec(shape, lambda s: (0,) * len(shape), **kw)
    halo_blocks = CONV_TS // HALO

    def tile(s):
        t = jnp.minimum(s, N_TILES - 1)
        return t // TILES_PER_SEQ, t % TILES_PER_SEQ

    def main_map(s):
        b, j = tile(s)
        return b, j, 0

    def top_map(s):
        b, j = tile(s)
        return b, jnp.maximum(j * halo_blocks - 1, 0), 0

    def bot_map(s):
        b, j = tile(s)
        return b, jnp.minimum((j + 1) * halo_blocks, SEQ // HALO - 1), 0

    gain = const((1, D_MODEL))
    return pl.pallas_call(
        _conv_mlp_kernel,
        grid=(N_TILES + 1,),
        in_specs=[pl.BlockSpec((1, CONV_TS, D_MODEL), main_map),
                  pl.BlockSpec((1, HALO, D_MODEL), top_map),
                  pl.BlockSpec((1, HALO, D_MODEL), bot_map),
                  gain,
                  const((D_MODEL, 2 * D_MODEL), pipeline_mode=resident),
                  const((1, 2 * D_MODEL)),
                  const((CONV_WIDTH, D_MODEL)),
                  gain, gain, gain,
                  const((D_MODEL, D_MODEL), pipeline_mode=resident),
                  gain, gain, gain, gain,
                  const((D_MODEL, D_FF), pipeline_mode=resident),
                  const((D_FF, D_MODEL), pipeline_mode=resident)],
        out_specs=pl.BlockSpec((CONV_TS, D_MODEL), lambda s: (jnp.maximum(s - 1, 0), 0)),
        out_shape=jax.ShapeDtypeStruct((BATCH * SEQ, D_MODEL), F32),
        scratch_shapes=[pltpu.VMEM((CONV_ROWS, D_MODEL), BF16),
                        pltpu.VMEM((D_MODEL // LANES, CONV_ROWS, LANES), F32),
                        pltpu.VMEM((D_MODEL // LANES, CONV_TS, LANES), F32),
                        pltpu.VMEM((CONV_TS, D_MODEL), F32),
                        pltpu.VMEM((CONV_TS, D_MODEL), F32)],
        compiler_params=_params(("arbitrary",)),
        name="conformer_conv_mlp",
    )(x, x, x, row(g_pre), w1, row(b1), wdw, row(bdw), row(lng), row(lnb), w2, row(b2),
      row(g_post), row(g2_pre), row(g2_post), w_up, w_down)


def kernel(x, rel_bias, norm_mix_pre, norm_mix_post, norm_mlp_pre, norm_mlp_post, attn_w_qkv,
           attn_w_o, conv_w_pw1, conv_b_pw1, conv_w_dw, conv_b_dw, conv_ln_g, conv_ln_b,
           conv_w_pw2, conv_b_pw2, mlp_w_up, mlp_w_down):
    n_tok = BATCH * SEQ
    x2d = x.reshape(n_tok, D_MODEL)

    h = _prenorm(x, norm_mix_pre[0])
    w_qkv = attn_w_qkv.reshape(D_MODEL, N_GROUPS * GROUP_WIDTH)
    qkv = [_qkv_proj(h[g], w_qkv, g) for g in range(N_GROUPS)]
    a = _attention(qkv, rel_bias)
    x2d, (w_pw1, w_pw2, w_up1, w_down1) = _proj_mlp(
        x2d, a.reshape(n_tok, D_MODEL), attn_w_o, norm_mix_post[0], norm_mlp_pre[0],
        norm_mlp_post[0], mlp_w_up, mlp_w_down,
        to_bf16=[(conv_w_pw1, 0), (conv_w_pw2, 0), (mlp_w_up, 1), (mlp_w_down, 1)])

    x2d = _conv_mlp_layer(x2d.reshape(BATCH, SEQ, D_MODEL), norm_mix_pre[1], w_pw1, conv_b_pw1[0],
                          conv_w_dw[0], conv_b_dw[0], conv_ln_g[0], conv_ln_b[0], w_pw2,
                          conv_b_pw2[0], norm_mix_post[1], norm_mlp_pre[1], norm_mlp_post[1],
                          w_up1, w_down1)
    return x2d.reshape(BATCH, SEQ, D_MODEL)
```

```python
import functools
import math

import jax
import jax.numpy as jnp
from jax import lax
from jax.experimental import pallas as pl
from jax.experimental.pallas import tpu as pltpu

D_MODEL = 1024
BATCH = 8
SEQ = 2048
HEAD_DIM = 64
N_HEADS = 16
DILATIONS = (1, 4, 16)
N_SIDE = 64
N_GROUPS = 3
GROUP_WIDTH = 3 * N_HEADS * HEAD_DIM
N_BUCKETS = 32
MAX_DISTANCE = 1024
CONV_WIDTH = 31
D_FF = 4 * D_MODEL
RMS_EPS = 1e-6
LN_EPS = 1e-5
NEG_INF = -1e30

F32 = jnp.float32
BF16 = jnp.bfloat16

LANES = 128
Q_TILE = 128
K_TILE = Q_TILE + 2 * N_SIDE
VMEM_LIMIT = 56 * 1024 * 1024


def _rms(x, g):
    return x * lax.rsqrt(jnp.mean(x * x, axis=-1, keepdims=True) + RMS_EPS) * g


def _params(semantics):
    return pltpu.CompilerParams(dimension_semantics=semantics, vmem_limit_bytes=VMEM_LIMIT)


NORM_ROWS = 256


def _prenorm_kernel(x_ref, g_ref, *refs):
    out_refs, slab_ref = refs[:-1], refs[-1]

    def chunk(i, carry):
        rows = pl.ds(pl.multiple_of(i * NORM_ROWS, NORM_ROWS), NORM_ROWS)
        hn = _rms(x_ref[0, rows, :], g_ref[...])
        for k in range(D_MODEL // LANES):
            slab_ref[k, rows, :] = hn[:, k * LANES:(k + 1) * LANES]
        for out_ref, r in zip(out_refs, DILATIONS):
            if r == 1:
                out_ref[0, rows, :] = hn.astype(BF16)
        return carry

    lax.fori_loop(0, SEQ // NORM_ROWS, chunk, 0)
    for out_ref, r in zip(out_refs, DILATIONS):
        if r == 1:
            continue
        L = SEQ // r
        for c in range(r):
            for k in range(D_MODEL // LANES):
                out_ref[0, c * L:(c + 1) * L, k * LANES:(k + 1) * LANES] = (
                    slab_ref[k, pl.ds(c, L, stride=r), :].astype(BF16))


def _prenorm(x, g):
    blk = pl.BlockSpec((1, SEQ, D_MODEL), lambda b: (b, 0, 0))
    return pl.pallas_call(
        _prenorm_kernel,
        grid=(BATCH,),
        in_specs=[blk, pl.BlockSpec((1, D_MODEL), lambda b: (0, 0))],
        out_specs=[blk] * N_GROUPS,
        out_shape=[jax.ShapeDtypeStruct((BATCH, SEQ, D_MODEL), BF16)] * N_GROUPS,
        scratch_shapes=[pltpu.VMEM((D_MODEL // LANES, SEQ, LANES), F32)],
        compiler_params=_params(("arbitrary",)),
        name="prenorm",
    )(x, g.reshape(1, D_MODEL))


QKV_TN = 1024
QKV_TM = 512
assert QKV_TN == N_HEADS * HEAD_DIM
LOG2E = math.log2(math.e)
Q_SCALE = HEAD_DIM ** -0.5 * LOG2E


def _qkv_kernel(h_ref, w_ref, o_ref, wb_ref):
    @pl.when(pl.program_id(1) == 0)
    def _():
        wb_ref[...] = w_ref[...].astype(BF16)

    scale = jnp.where(pl.program_id(0) == 0, Q_SCALE, 1.0)
    for m in range(SEQ // QKV_TM):
        rows = slice(m * QKV_TM, (m + 1) * QKV_TM)
        res = jnp.dot(h_ref[0, rows, :], wb_ref[...], preferred_element_type=F32)
        o_ref[0, rows, :] = (res * scale).astype(BF16)


def _qkv_proj(h, w_qkv, g):
    nj = GROUP_WIDTH // QKV_TN
    return pl.pallas_call(
        _qkv_kernel,
        grid=(nj, BATCH),
        in_specs=[pl.BlockSpec((1, SEQ, D_MODEL), lambda j, b: (b, 0, 0)),
                  pl.BlockSpec((D_MODEL, QKV_TN), lambda j, b: (0, g * nj + j))],
        out_specs=pl.BlockSpec((1, SEQ, QKV_TN), lambda j, b: (b, 0, j)),
        out_shape=jax.ShapeDtypeStruct((BATCH, SEQ, GROUP_WIDTH), BF16),
        scratch_shapes=[pltpu.VMEM((D_MODEL, QKV_TN), BF16)],
        compiler_params=_params(("arbitrary", "arbitrary")),
        name=f"qkv_proj_g{g}",
    )(h, w_qkv)


HEADS_PER_STEP = 4
PAIRS_PER_STEP = HEADS_PER_STEP // 2
STEP_LANES = HEADS_PER_STEP * HEAD_DIM
N_Q_TILES = SEQ // Q_TILE
N_KINDS = 3
PADDED_PITCH = 24
GROUP_ROWS = tuple(SEQ // 16 * PADDED_PITCH if r % 16 == 0 else SEQ for r in DILATIONS)
GROUP_BASE = tuple(sum(GROUP_ROWS[:g]) for g in range(N_GROUPS))


def _bias_row(rel_ref, g, head):
    r = DILATIONS[g]
    nb = N_BUCKETS // 2
    max_exact = nb // 2
    delta = lax.broadcasted_iota(jnp.int32, (8, K_TILE), 1) - N_SIDE
    rel = delta * r
    n = jnp.abs(rel)
    nf = jnp.maximum(n, 1).astype(F32)
    large = max_exact + (jnp.log(nf / max_exact) / math.log(MAX_DISTANCE / max_exact)
                         * (nb - max_exact)).astype(jnp.int32)
    large = jnp.minimum(large, nb - 1)
    bucket = jnp.where(rel > 0, nb, 0) + jnp.where(n < max_exact, n, large)
    u = jnp.zeros((8, K_TILE), F32)
    for b in range(N_BUCKETS):
        u = jnp.where(bucket == b, rel_ref[b, g * N_HEADS + head], u)
    return jnp.where(jnp.abs(delta) <= N_SIDE, u * LOG2E, NEG_INF)


def _build_bias_tiles(rel_ref, bias_ref, head0):
    qi = lax.broadcasted_iota(jnp.int32, (Q_TILE, K_TILE), 0)
    kj = lax.broadcasted_iota(jnp.int32, (Q_TILE, K_TILE), 1)
    for g in range(N_GROUPS):
        for h in range(HEADS_PER_STEP):
            u = _bias_row(rel_ref, g, head0 + h)
            ub = jnp.broadcast_to(u[0:1, :], (Q_TILE, K_TILE))
            for kind in range(N_KINDS):
                shift = (N_SIDE * kind - N_SIDE) % K_TILE
                t = pltpu.roll(ub, shift, 1, stride=1, stride_axis=0)
                in_band = jnp.abs(kj - qi - N_SIDE * kind) <= N_SIDE
                bias_ref[g, h, kind] = jnp.where(in_band, t, NEG_INF)


def _token_rows(g, c, l0):
    r = DILATIONS[g]
    if r == 1:
        return pl.ds(GROUP_BASE[g] + l0, Q_TILE)
    pitch = PADDED_PITCH if r % 16 == 0 else r
    return pl.ds(GROUP_BASE[g] + l0 * pitch + c, Q_TILE, stride=pitch)


def _load_tokens(ref, pair, g, t):
    r = DILATIONS[g]
    if r % 16:
        return ref[pair, pl.ds(pl.multiple_of(GROUP_BASE[g] + t * Q_TILE, Q_TILE), Q_TILE), :]
    n = Q_TILE // r
    base = pl.multiple_of(GROUP_BASE[g] + t * n * PADDED_PITCH, 8)
    return jnp.concatenate([ref[pair, pl.ds(base + i * PADDED_PITCH, r), :] for i in range(n)],
                           axis=0)


def _split_heads(q):
    first = lax.broadcasted_iota(jnp.int32, (1, LANES), 1) < HEAD_DIM
    zero = jnp.zeros_like(q)
    return jnp.concatenate([jnp.where(first, q, zero), jnp.where(first, zero, q)], axis=0)


def _softmax_pv(s, v):
    first = lax.broadcasted_iota(jnp.int32, (1, LANES), 1) < HEAD_DIM
    m = jnp.max(s, axis=-1, keepdims=True)
    p = jnp.exp2(s - m).astype(BF16)
    v1 = jnp.concatenate([v, jnp.ones((v.shape[0], LANES), BF16)], axis=1)
    o2 = jnp.dot(p, v1, preferred_element_type=F32)
    acc = jnp.where(first, o2[:Q_TILE, :LANES], o2[Q_TILE:, :LANES])
    l = jnp.where(first, o2[:Q_TILE, LANES:], o2[Q_TILE:, LANES:])
    m_t = jnp.where(first, m[:Q_TILE], m[Q_TILE:])
    return acc, m_t, l


_NT = (((1,), (1,)), ((), ()))


def _attn_kernel(rel_ref, q0_ref, k0_ref, v0_ref, q1_ref, k1_ref, v1_ref, q2_ref, k2_ref, v2_ref,
                 o_ref, bias_ref, acc_ref, m_ref, l_ref):
    hp = pl.program_id(0)

    @pl.when(pl.program_id(1) == 0)
    def _():
        _build_bias_tiles(rel_ref, bias_ref, hp * HEADS_PER_STEP)

    qkv = ((q0_ref, k0_ref, v0_ref), (q1_ref, k1_ref, v1_ref), (q2_ref, k2_ref, v2_ref))
    for pair in range(PAIRS_PER_STEP):
        lanes = slice(pair * LANES, (pair + 1) * LANES)
        for g in range(N_GROUPS):
            r = DILATIONS[g]
            tiles_per_seq = SEQ // r // Q_TILE
            q_ref, k_ref, v_ref = qkv[g]

            def put(t, res, g=g, tiles_per_seq=tiles_per_seq, pair=pair):
                rows = _token_rows(g, t // tiles_per_seq, (t % tiles_per_seq) * Q_TILE)
                for ref, val in zip((acc_ref, m_ref, l_ref), res):
                    ref[pair, rows, :] = val

            if tiles_per_seq > 1:
                for t in range(N_Q_TILES):
                    tl = t % tiles_per_seq
                    kind = 0 if tl == 0 else (2 if tl == tiles_per_seq - 1 else 1)
                    q0 = t * Q_TILE
                    k0 = q0 - N_SIDE * kind
                    q2 = _split_heads(q_ref[0, q0:q0 + Q_TILE, lanes])
                    s = lax.dot_general(q2, k_ref[0, k0:k0 + K_TILE, lanes], _NT,
                                        preferred_element_type=F32)
                    s = s + jnp.concatenate([bias_ref[g, 2 * pair, kind],
                                             bias_ref[g, 2 * pair + 1, kind]], axis=0)
                    put(t, _softmax_pv(s, v_ref[0, k0:k0 + K_TILE, lanes]))
            else:
                bias = jnp.concatenate([bias_ref[g, 2 * pair, 0, :, :Q_TILE],
                                        bias_ref[g, 2 * pair + 1, 0, :, :Q_TILE]], axis=0)
                for t in range(0, N_Q_TILES, 2):
                    q0 = t * Q_TILE
                    q = q_ref[0, q0:q0 + 2 * Q_TILE, lanes]
                    q2 = jnp.concatenate([_split_heads(q[:Q_TILE]), _split_heads(q[Q_TILE:])], axis=0)
                    s = lax.dot_general(q2, k_ref[0, q0:q0 + 2 * Q_TILE, lanes], _NT,
                                        preferred_element_type=F32)
                    for i in range(2):
                        si = s[2 * i * Q_TILE:2 * (i + 1) * Q_TILE, i * Q_TILE:(i + 1) * Q_TILE]
                        v = v_ref[0, q0 + i * Q_TILE:q0 + (i + 1) * Q_TILE, lanes]
                        put(t + i, _softmax_pv(si + bias, v))

    for pair in range(PAIRS_PER_STEP):
        lanes = slice(pair * LANES, (pair + 1) * LANES)

        def merge_body(t, carry, lanes=lanes, pair=pair):
            m = [_load_tokens(m_ref, pair, g, t) for g in range(N_GROUPS)]
            top = jnp.maximum(jnp.maximum(m[0], m[1]), m[2])
            w = [jnp.exp2(x - top) for x in m]
            num = sum(w[g] * _load_tokens(acc_ref, pair, g, t) for g in range(N_GROUPS))
            den = sum(w[g] * _load_tokens(l_ref, pair, g, t) for g in range(N_GROUPS))
            rows = pl.ds(pl.multiple_of(t * Q_TILE, Q_TILE), Q_TILE)
            o_ref[0, rows, lanes] = (num / den).astype(BF16)
            return carry

        lax.fori_loop(0, N_Q_TILES, merge_body, 0, unroll=4)


def _attention(qkv, rel_bias):
    n_hp = N_HEADS // HEADS_PER_STEP
    blk = (1, SEQ, STEP_LANES)
    specs = [pl.BlockSpec(memory_space=pltpu.SMEM)]
    args = [rel_bias]
    for g in range(N_GROUPS):
        for part in range(3):
            specs.append(pl.BlockSpec(blk, lambda hp, b, part=part: (b, 0, part * n_hp + hp)))
            args.append(qkv[g])
    return pl.pallas_call(
        _attn_kernel,
        grid=(n_hp, BATCH),
        in_specs=specs,
        out_specs=pl.BlockSpec(blk, lambda hp, b: (b, 0, hp)),
        out_shape=jax.ShapeDtypeStruct((BATCH, SEQ, D_MODEL), BF16),
        scratch_shapes=[pltpu.VMEM((N_GROUPS, HEADS_PER_STEP, N_KINDS, Q_TILE, K_TILE), F32),
                        pltpu.VMEM((PAIRS_PER_STEP, sum(GROUP_ROWS), LANES), F32),
                        pltpu.VMEM((PAIRS_PER_STEP, sum(GROUP_ROWS), LANES), F32),
                        pltpu.VMEM((PAIRS_PER_STEP, sum(GROUP_ROWS), LANES), F32)],
        compiler_params=_params(("arbitrary", "arbitrary")),
        name="dilated_attention",
    )(*args)


MLP_TM = 512
MLP_FC = 1024
MLP_STAGES = 2 + 2 * (D_FF // MLP_FC)


_DONE = object()


def _interleave(order, **stage_generators):
    for key in order:
        next(stage_generators[key], _DONE)
    for gen in stage_generators.values():
        assert next(gen, _DONE) is _DONE, "order does not cover every stage"


def _mlp_stages(read_x, gpre_ref, gpost_ref, wup_ref, wdn_ref, o_ref):
    h = _rms(read_x(), gpre_ref[...]).astype(BF16)
    acc = jnp.zeros((MLP_TM, D_MODEL), F32)
    yield
    for c in range(D_FF // MLP_FC):
        cols = slice(c * MLP_FC, (c + 1) * MLP_FC)
        u = jnp.dot(h, wup_ref[:, cols].astype(BF16), preferred_element_type=F32)
        u = jnp.square(jnp.maximum(u, 0.0)).astype(BF16)
        yield
        acc = acc + jnp.dot(u, wdn_ref[cols, :].astype(BF16), preferred_element_type=F32)
        yield
    o_ref[...] = read_x() + _rms(acc, gpost_ref[...])


def _proj_mlp_kernel(a_ref, wo_ref, gmix_ref, x_ref, gpre_ref, gpost_ref, wup_ref, wdn_ref,
                     *refs, n_cast):
    f32_refs, o_ref, bf16_refs = refs[:n_cast], refs[n_cast], refs[n_cast + 1:]
    for src_ref, dst_ref in zip(f32_refs, bf16_refs):
        dst_ref[...] = src_ref[...].astype(BF16)
    m = jnp.dot(a_ref[...], wo_ref[...].astype(BF16), preferred_element_type=F32)
    x = x_ref[...] + _rms(m, gmix_ref[...])
    _interleave("m" * MLP_STAGES,
                m=_mlp_stages(lambda: x, gpre_ref, gpost_ref, wup_ref, wdn_ref, o_ref))


def _proj_mlp(x2d, a2d, w_o, g_mix, g_pre, g_post, w_up, w_down, to_bf16):
    n = x2d.shape[0]
    steps = n // MLP_TM
    resident = pl.Buffered(1)
    tile = pl.BlockSpec((MLP_TM, D_MODEL), lambda i: (i, 0))
    gain = pl.BlockSpec((1, D_MODEL), lambda i: (0, 0))
    cast_in, cast_out, cast_shape = [], [], []
    for w, layer in to_bf16:
        _, rows, cols = w.shape
        cast_in.append(pl.BlockSpec((None, rows // steps, cols), lambda i, layer=layer: (layer, i, 0)))
        cast_out.append(pl.BlockSpec((rows // steps, cols), lambda i: (i, 0)))
        cast_shape.append(jax.ShapeDtypeStruct((rows, cols), BF16))
    out = pl.pallas_call(
        functools.partial(_proj_mlp_kernel, n_cast=len(to_bf16)),
        grid=(steps,),
        in_specs=[tile,
                  pl.BlockSpec((None, D_MODEL, D_MODEL), lambda i: (0, 0, 0), pipeline_mode=resident),
                  gain, tile, gain, gain,
                  pl.BlockSpec((None, D_MODEL, D_FF), lambda i: (0, 0, 0), pipeline_mode=resident),
                  pl.BlockSpec((None, D_FF, D_MODEL), lambda i: (0, 0, 0), pipeline_mode=resident)]
        + cast_in,
        out_specs=[tile] + cast_out,
        out_shape=[jax.ShapeDtypeStruct((n, D_MODEL), F32)] + cast_shape,
        compiler_params=_params(("arbitrary",)),
        name="attn_proj_mlp",
    )(a2d, w_o, g_mix.reshape(1, D_MODEL), x2d, g_pre.reshape(1, D_MODEL),
      g_post.reshape(1, D_MODEL), w_up, w_down, *[w for w, _ in to_bf16])
    return out[0], out[1:]


CONV_TS = 512
HALO = 16
CONV_ROWS = CONV_TS + 2 * HALO
CONV_RB = 128
CONV_NC = 256


N_TILES = BATCH * SEQ // CONV_TS
TILES_PER_SEQ = SEQ // CONV_TS
assert CONV_TS == MLP_TM


def _conv_stages(xm_ref, xt_ref, xb_ref, first, last, gpre_ref, w1_ref, b1_ref, wdw_ref, bdw_ref,
                 lng_ref, lnb_ref, w2_ref, b2_ref, gpost_ref, o_ref, h_ref, u_ref, c_ref):
    gpre = gpre_ref[...]
    h_ref[0:HALO, :] = _rms(xt_ref[0], gpre).astype(BF16)
    h_ref[HALO:HALO + CONV_TS, :] = _rms(xm_ref[0], gpre).astype(BF16)
    h_ref[HALO + CONV_TS:, :] = _rms(xb_ref[0], gpre).astype(BF16)
    yield

    h = h_ref[...]
    half = CONV_RB // 2
    for nc in range(D_MODEL // CONV_NC):
        cols = slice(nc * CONV_NC, (nc + 1) * CONV_NC)
        gcols = slice(D_MODEL + nc * CONV_NC, D_MODEL + (nc + 1) * CONV_NC)
        a = jnp.dot(h, w1_ref[:, cols], preferred_element_type=F32) + b1_ref[:, cols]
        gate = jnp.dot(h, w1_ref[:, gcols], preferred_element_type=F32) + b1_ref[:, gcols]
        u = a * jax.nn.sigmoid(gate)
        for k in range(CONV_NC // LANES):
            lc = nc * CONV_NC // LANES + k
            lanes = slice(lc * LANES, (lc + 1) * LANES)
            uk = u[:, k * LANES:(k + 1) * LANES]
            u_ref[lc, 0:HALO, :] = jnp.where(first, 0.0, uk[0:HALO])
            u_ref[lc, HALO:HALO + CONV_TS, :] = uk[HALO:HALO + CONV_TS]
            u_ref[lc, HALO + CONV_TS:, :] = jnp.where(last, 0.0, uk[HALO + CONV_TS:])
        yield
        for k in range(CONV_NC // LANES):
            lc = nc * CONV_NC // LANES + k
            lanes = slice(lc * LANES, (lc + 1) * LANES)
            for r0 in range(0, CONV_TS, CONV_RB):
                for phase in range(2):
                    acc = bdw_ref[:, lanes]
                    for t in range(CONV_WIDTH):
                        start = r0 + phase + t + HALO - CONV_WIDTH // 2
                        acc = acc + (u_ref[lc, pl.ds(start, half, stride=2), :]
                                     * wdw_ref[t:t + 1, lanes])
                    c_ref[lc, pl.ds(r0 + phase, half, stride=2), :] = acc
            yield

    v = jnp.concatenate([c_ref[lc] for lc in range(D_MODEL // LANES)], axis=1)
    mu = jnp.mean(v, axis=-1, keepdims=True)
    var = jnp.mean(jnp.square(v - mu), axis=-1, keepdims=True)
    y = (v - mu) * lax.rsqrt(var + LN_EPS) * lng_ref[...] + lnb_ref[...]
    y = (y * jax.nn.sigmoid(y)).astype(BF16)
    yield
    z = jnp.dot(y, w2_ref[...], preferred_element_type=F32) + b2_ref[...]
    o_ref[...] = xm_ref[0] + _rms(z, gpost_ref[...])


def _conv_mlp_kernel(xm_ref, xt_ref, xb_ref, gpre_ref, w1_ref, b1_ref, wdw_ref, bdw_ref, lng_ref,
                     lnb_ref, w2_ref, b2_ref, gpost_ref, g2pre_ref, g2post_ref, wup_ref, wdn_ref,
                     o_ref, h_ref, u_ref, c_ref, x_ref, xnew_ref):
    s = pl.program_id(0)

    @pl.when(s == 0)
    def _():
        x_ref[...] = jnp.zeros((CONV_TS, D_MODEL), F32)

    j = jnp.minimum(s, N_TILES - 1) % TILES_PER_SEQ
    order = "ccccc" + "mm" + "cmccm" * 2 + "cmcm" * 2
    _interleave(
        order,
        c=_conv_stages(xm_ref, xt_ref, xb_ref, j == 0, j == TILES_PER_SEQ - 1, gpre_ref, w1_ref,
                       b1_ref, wdw_ref, bdw_ref, lng_ref, lnb_ref, w2_ref, b2_ref, gpost_ref,
                       xnew_ref, h_ref, u_ref, c_ref),
        m=_mlp_stages(lambda: x_ref[...], g2pre_ref, g2post_ref, wup_ref, wdn_ref, o_ref))
    x_ref[...] = xnew_ref[...]


def _conv_mlp_layer(x, g_pre, w1, b1, wdw, bdw, lng, lnb, w2, b2, g_post, g2_pre, g2_post,
                    w_up, w_down):
    row = lambda v: v.reshape(1, -1)
    resident = pl.Buffered(1)
    const = lambda shape, **kw: pl.BlockSpec(shape, lambda s: (0,) * len(shape), **kw)
    halo_blocks = CONV_TS // HALO

    def tile(s):
        t = jnp.minimum(s, N_TILES - 1)
        return t // TILES_PER_SEQ, t % TILES_PER_SEQ

    def main_map(s):
        b, j = tile(s)
        return b, j, 0

    def top_map(s):
        b, j = tile(s)
        return b, jnp.maximum(j * halo_blocks - 1, 0), 0

    def bot_map(s):
        b, j = tile(s)
        return b, jnp.minimum((j + 1) * halo_blocks, SEQ // HALO - 1), 0

    gain = const((1, D_MODEL))
    return pl.pallas_call(
        _conv_mlp_kernel,
        grid=(N_TILES + 1,),
        in_specs=[pl.BlockSpec((1, CONV_TS, D_MODEL), main_map),
                  pl.BlockSpec((1, HALO, D_MODEL), top_map),
                  pl.BlockSpec((1, HALO, D_MODEL), bot_map),
                  gain,
                  const((D_MODEL, 2 * D_MODEL), pipeline_mode=resident),
                  const((1, 2 * D_MODEL)),
                  const((CONV_WIDTH, D_MODEL)),
                  gain, gain, gain,
                  const((D_MODEL, D_MODEL), pipeline_mode=resident),
                  gain, gain, gain, gain,
                  const((D_MODEL, D_FF), pipeline_mode=resident),
                  const((D_FF, D_MODEL), pipeline_mode=resident)],
        out_specs=pl.BlockSpec((CONV_TS, D_MODEL), lambda s: (jnp.maximum(s - 1, 0), 0)),
        out_shape=jax.ShapeDtypeStruct((BATCH * SEQ, D_MODEL), F32),
        scratch_shapes=[pltpu.VMEM((CONV_ROWS, D_MODEL), BF16),
                        pltpu.VMEM((D_MODEL // LANES, CONV_ROWS, LANES), F32),
                        pltpu.VMEM((D_MODEL // LANES, CONV_TS, LANES), F32),
                        pltpu.VMEM((CONV_TS, D_MODEL), F32),
                        pltpu.VMEM((CONV_TS, D_MODEL), F32)],
        compiler_params=_params(("arbitrary",)),
        name="conformer_conv_mlp",
    )(x, x, x, row(g_pre), w1, row(b1), wdw, row(bdw), row(lng), row(lnb), w2, row(b2),
      row(g_post), row(g2_pre), row(g2_post), w_up, w_down)


def kernel(x, rel_bias, norm_mix_pre, norm_mix_post, norm_mlp_pre, norm_mlp_post, attn_w_qkv,
           attn_w_o, conv_w_pw1, conv_b_pw1, conv_w_dw, conv_b_dw, conv_ln_g, conv_ln_b,
           conv_w_pw2, conv_b_pw2, mlp_w_up, mlp_w_down):
    n_tok = BATCH * SEQ
    x2d = x.reshape(n_tok, D_MODEL)

    h = _prenorm(x, norm_mix_pre[0])
    w_qkv = attn_w_qkv.reshape(D_MODEL, N_GROUPS * GROUP_WIDTH)
    qkv = [_qkv_proj(h[g], w_qkv, g) for g in range(N_GROUPS)]
    a = _attention(qkv, rel_bias)
    x2d, (w_pw1, w_pw2, w_up1, w_down1) = _proj_mlp(
        x2d, a.reshape(n_tok, D_MODEL), attn_w_o, norm_mix_post[0], norm_mlp_pre[0],
        norm_mlp_post[0], mlp_w_up, mlp_w_down,
        to_bf16=[(conv_w_pw1, 0), (conv_w_pw2, 0), (mlp_w_up, 1), (mlp_w_down, 1)])

    x2d = _conv_mlp_layer(x2d.reshape(BATCH, SEQ, D_MODEL), norm_mix_pre[1], w_pw1, conv_b_pw1[0],
                          conv_w_dw[0], conv_b_dw[0], conv_ln_g[0], conv_ln_b[0], w_pw2,
                          conv_b_pw2[0], norm_mix_post[1], norm_mlp_pre[1], norm_mlp_post[1],
                          w_up1, w_down1)
    return x2d.reshape(BATCH, SEQ, D_MODEL)
```

```python
import functools
import math

import jax
import jax.numpy as jnp
from jax import lax
from jax.experimental import pallas as pl
from jax.experimental.pallas import tpu as pltpu

D_MODEL = 1024
BATCH = 8
SEQ = 2048
HEAD_DIM = 64
N_HEADS = 16
DILATIONS = (1, 4, 16)
N_SIDE = 64
N_GROUPS = 3
GROUP_WIDTH = 3 * N_HEADS * HEAD_DIM
N_BUCKETS = 32
MAX_DISTANCE = 1024
CONV_WIDTH = 31
D_FF = 4 * D_MODEL
RMS_EPS = 1e-6
LN_EPS = 1e-5
NEG_INF = -1e30

F32 = jnp.float32
BF16 = jnp.bfloat16

LANES = 128
Q_TILE = 128
K_TILE = Q_TILE + 2 * N_SIDE
VMEM_LIMIT = 56 * 1024 * 1024


def _rms(x, g):
    return x * lax.rsqrt(jnp.mean(x * x, axis=-1, keepdims=True) + RMS_EPS) * g


def _params(semantics):
    return pltpu.CompilerParams(dimension_semantics=semantics, vmem_limit_bytes=VMEM_LIMIT)


NORM_ROWS = 256


def _prenorm_kernel(x_ref, g_ref, *refs):
    out_refs, slab_ref = refs[:-1], refs[-1]

    def chunk(i, carry):
        rows = pl.ds(pl.multiple_of(i * NORM_ROWS, NORM_ROWS), NORM_ROWS)
        hn = _rms(x_ref[0, rows, :], g_ref[...])
        for k in range(D_MODEL // LANES):
            slab_ref[k, rows, :] = hn[:, k * LANES:(k + 1) * LANES]
        for out_ref, r in zip(out_refs, DILATIONS):
            if r == 1:
                out_ref[0, rows, :] = hn.astype(BF16)
        return carry

    lax.fori_loop(0, SEQ // NORM_ROWS, chunk, 0)
    for out_ref, r in zip(out_refs, DILATIONS):
        if r == 1:
            continue
        L = SEQ // r
        for c in range(r):
            for k in range(D_MODEL // LANES):
                out_ref[0, c * L:(c + 1) * L, k * LANES:(k + 1) * LANES] = (
                    slab_ref[k, pl.ds(c, L, stride=r), :].astype(BF16))


def _prenorm(x, g):
    blk = pl.BlockSpec((1, SEQ, D_MODEL), lambda b: (b, 0, 0))
    return pl.pallas_call(
        _prenorm_kernel,
        grid=(BATCH,),
        in_specs=[blk, pl.BlockSpec((1, D_MODEL), lambda b: (0, 0))],
        out_specs=[blk] * N_GROUPS,
        out_shape=[jax.ShapeDtypeStruct((BATCH, SEQ, D_MODEL), BF16)] * N_GROUPS,
        scratch_shapes=[pltpu.VMEM((D_MODEL // LANES, SEQ, LANES), F32)],
        compiler_params=_params(("arbitrary",)),
        name="prenorm",
    )(x, g.reshape(1, D_MODEL))


QKV_TN = 1536
QKV_TM = 512
Q_WIDTH = N_HEADS * HEAD_DIM
assert QKV_TN >= Q_WIDTH
LOG2E = math.log2(math.e)
Q_SCALE = HEAD_DIM ** -0.5 * LOG2E


def _qkv_kernel(h_ref, w_ref, o_ref, wb_ref):
    @pl.when(pl.program_id(1) == 0)
    def _():
        wb_ref[...] = w_ref[...].astype(BF16)

    col = lax.broadcasted_iota(jnp.int32, (1, QKV_TN), 1)
    scale = jnp.where((pl.program_id(0) == 0) & (col < Q_WIDTH), Q_SCALE, 1.0)
    for m in range(SEQ // QKV_TM):
        rows = slice(m * QKV_TM, (m + 1) * QKV_TM)
        res = jnp.dot(h_ref[0, rows, :], wb_ref[...], preferred_element_type=F32)
        o_ref[0, rows, :] = (res * scale).astype(BF16)


def _qkv_proj(h, w_qkv, g):
    nj = GROUP_WIDTH // QKV_TN
    return pl.pallas_call(
        _qkv_kernel,
        grid=(nj, BATCH),
        in_specs=[pl.BlockSpec((1, SEQ, D_MODEL), lambda j, b: (b, 0, 0)),
                  pl.BlockSpec((D_MODEL, QKV_TN), lambda j, b: (0, g * nj + j))],
        out_specs=pl.BlockSpec((1, SEQ, QKV_TN), lambda j, b: (b, 0, j)),
        out_shape=jax.ShapeDtypeStruct((BATCH, SEQ, GROUP_WIDTH), BF16),
        scratch_shapes=[pltpu.VMEM((D_MODEL, QKV_TN), BF16)],
        compiler_params=_params(("arbitrary", "arbitrary")),
        name=f"qkv_proj_g{g}",
    )(h, w_qkv)


HEADS_PER_STEP = 4
PAIRS_PER_STEP = HEADS_PER_STEP // 2
STEP_LANES = HEADS_PER_STEP * HEAD_DIM
N_Q_TILES = SEQ // Q_TILE
N_KINDS = 3
PADDED_PITCH = 24
GROUP_ROWS = tuple(SEQ // 16 * PADDED_PITCH if r % 16 == 0 else SEQ for r in DILATIONS)
GROUP_BASE = tuple(sum(GROUP_ROWS[:g]) for g in range(N_GROUPS))


def _bias_row(rel_ref, g, head):
    r = DILATIONS[g]
    nb = N_BUCKETS // 2
    max_exact = nb // 2
    delta = lax.broadcasted_iota(jnp.int32, (8, K_TILE), 1) - N_SIDE
    rel = delta * r
    n = jnp.abs(rel)
    nf = jnp.maximum(n, 1).astype(F32)
    large = max_exact + (jnp.log(nf / max_exact) / math.log(MAX_DISTANCE / max_exact)
                         * (nb - max_exact)).astype(jnp.int32)
    large = jnp.minimum(large, nb - 1)
    bucket = jnp.where(rel > 0, nb, 0) + jnp.where(n < max_exact, n, large)
    u = jnp.zeros((8, K_TILE), F32)
    for b in range(N_BUCKETS):
        u = jnp.where(bucket == b, rel_ref[b, g * N_HEADS + head], u)
    return jnp.where(jnp.abs(delta) <= N_SIDE, u * LOG2E, NEG_INF)


def _build_bias_tiles(rel_ref, bias_ref, head0):
    qi = lax.broadcasted_iota(jnp.int32, (Q_TILE, K_TILE), 0)
    kj = lax.broadcasted_iota(jnp.int32, (Q_TILE, K_TILE), 1)
    for g in range(N_GROUPS):
        for h in range(HEADS_PER_STEP):
            u = _bias_row(rel_ref, g, head0 + h)
            ub = jnp.broadcast_to(u[0:1, :], (Q_TILE, K_TILE))
            for kind in range(N_KINDS):
                shift = (N_SIDE * kind - N_SIDE) % K_TILE
                t = pltpu.roll(ub, shift, 1, stride=1, stride_axis=0)
                in_band = jnp.abs(kj - qi - N_SIDE * kind) <= N_SIDE
                bias_ref[g, h, kind] = jnp.where(in_band, t, NEG_INF)


def _token_rows(g, c, l0):
    r = DILATIONS[g]
    if r == 1:
        return pl.ds(GROUP_BASE[g] + l0, Q_TILE)
    pitch = PADDED_PITCH if r % 16 == 0 else r
    return pl.ds(GROUP_BASE[g] + l0 * pitch + c, Q_TILE, stride=pitch)


def _load_tokens(ref, pair, g, t):
    r = DILATIONS[g]
    if r % 16:
        return ref[pair, pl.ds(pl.multiple_of(GROUP_BASE[g] + t * Q_TILE, Q_TILE), Q_TILE), :]
    n = Q_TILE // r
    base = pl.multiple_of(GROUP_BASE[g] + t * n * PADDED_PITCH, 8)
    return jnp.concatenate([ref[pair, pl.ds(base + i * PADDED_PITCH, r), :] for i in range(n)],
                           axis=0)


def _split_heads(q):
    first = lax.broadcasted_iota(jnp.int32, (1, LANES), 1) < HEAD_DIM
    zero = jnp.zeros_like(q)
    return jnp.concatenate([jnp.where(first, q, zero), jnp.where(first, zero, q)], axis=0)


def _softmax_pv(s, v):
    first = lax.broadcasted_iota(jnp.int32, (1, LANES), 1) < HEAD_DIM
    m = jnp.max(s, axis=-1, keepdims=True)
    p = jnp.exp2(s - m).astype(BF16)
    v1 = jnp.concatenate([v, jnp.ones((v.shape[0], LANES), BF16)], axis=1)
    o2 = jnp.dot(p, v1, preferred_element_type=F32)
    acc = jnp.where(first, o2[:Q_TILE, :LANES], o2[Q_TILE:, :LANES])
    l = jnp.where(first, o2[:Q_TILE, LANES:], o2[Q_TILE:, LANES:])
    m_t = jnp.where(first, m[:Q_TILE], m[Q_TILE:])
    return acc, m_t, l


_NT = (((1,), (1,)), ((), ()))


def _attn_kernel(rel_ref, q0_ref, k0_ref, v0_ref, q1_ref, k1_ref, v1_ref, q2_ref, k2_ref, v2_ref,
                 o_ref, bias_ref, acc_ref, m_ref, l_ref):
    hp = pl.program_id(0)

    @pl.when(pl.program_id(1) == 0)
    def _():
        _build_bias_tiles(rel_ref, bias_ref, hp * HEADS_PER_STEP)

    qkv = ((q0_ref, k0_ref, v0_ref), (q1_ref, k1_ref, v1_ref), (q2_ref, k2_ref, v2_ref))
    for pair in range(PAIRS_PER_STEP):
        lanes = slice(pair * LANES, (pair + 1) * LANES)
        for g in range(N_GROUPS):
            r = DILATIONS[g]
            tiles_per_seq = SEQ // r // Q_TILE
            q_ref, k_ref, v_ref = qkv[g]

            def put(t, res, g=g, tiles_per_seq=tiles_per_seq, pair=pair):
                rows = _token_rows(g, t // tiles_per_seq, (t % tiles_per_seq) * Q_TILE)
                for ref, val in zip((acc_ref, m_ref, l_ref), res):
                    ref[pair, rows, :] = val

            if tiles_per_seq > 1:
                for t in range(N_Q_TILES):
                    tl = t % tiles_per_seq
                    kind = 0 if tl == 0 else (2 if tl == tiles_per_seq - 1 else 1)
                    q0 = t * Q_TILE
                    k0 = q0 - N_SIDE * kind
                    q2 = _split_heads(q_ref[0, q0:q0 + Q_TILE, lanes])
                    s = lax.dot_general(q2, k_ref[0, k0:k0 + K_TILE, lanes], _NT,
                                        preferred_element_type=F32)
                    s = s + jnp.concatenate([bias_ref[g, 2 * pair, kind],
                                             bias_ref[g, 2 * pair + 1, kind]], axis=0)
                    put(t, _softmax_pv(s, v_ref[0, k0:k0 + K_TILE, lanes]))
            else:
                bias = jnp.concatenate([bias_ref[g, 2 * pair, 0, :, :Q_TILE],
                                        bias_ref[g, 2 * pair + 1, 0, :, :Q_TILE]], axis=0)
                for t in range(0, N_Q_TILES, 2):
                    q0 = t * Q_TILE
                    q = q_ref[0, q0:q0 + 2 * Q_TILE, lanes]
                    q2 = jnp.concatenate([_split_heads(q[:Q_TILE]), _split_heads(q[Q_TILE:])], axis=0)
                    s = lax.dot_general(q2, k_ref[0, q0:q0 + 2 * Q_TILE, lanes], _NT,
                                        preferred_element_type=F32)
                    for i in range(2):
                        si = s[2 * i * Q_TILE:2 * (i + 1) * Q_TILE, i * Q_TILE:(i + 1) * Q_TILE]
                        v = v_ref[0, q0 + i * Q_TILE:q0 + (i + 1) * Q_TILE, lanes]
                        put(t + i, _softmax_pv(si + bias, v))

    for pair in range(PAIRS_PER_STEP):
        lanes = slice(pair * LANES, (pair + 1) * LANES)

        def merge_body(t, carry, lanes=lanes, pair=pair):
            m = [_load_tokens(m_ref, pair, g, t) for g in range(N_GROUPS)]
            top = jnp.maximum(jnp.maximum(m[0], m[1]), m[2])
            w = [jnp.exp2(x - top) for x in m]
            num = sum(w[g] * _load_tokens(acc_ref, pair, g, t) for g in range(N_GROUPS))
            den = sum(w[g] * _load_tokens(l_ref, pair, g, t) for g in range(N_GROUPS))
            rows = pl.ds(pl.multiple_of(t * Q_TILE, Q_TILE), Q_TILE)
            o_ref[0, rows, lanes] = (num / den).astype(BF16)
            return carry

        lax.fori_loop(0, N_Q_TILES, merge_body, 0, unroll=4)


def _attention(qkv, rel_bias):
    n_hp = N_HEADS // HEADS_PER_STEP
    blk = (1, SEQ, STEP_LANES)
    specs = [pl.BlockSpec(memory_space=pltpu.SMEM)]
    args = [rel_bias]
    for g in range(N_GROUPS):
        for part in range(3):
            specs.append(pl.BlockSpec(blk, lambda hp, b, part=part: (b, 0, part * n_hp + hp)))
            args.append(qkv[g])
    return pl.pallas_call(
        _attn_kernel,
        grid=(n_hp, BATCH),
        in_specs=specs,
        out_specs=pl.BlockSpec(blk, lambda hp, b: (b, 0, hp)),
        out_shape=jax.ShapeDtypeStruct((BATCH, SEQ, D_MODEL), BF16),
        scratch_shapes=[pltpu.VMEM((N_GROUPS, HEADS_PER_STEP, N_KINDS, Q_TILE, K_TILE), F32),
                        pltpu.VMEM((PAIRS_PER_STEP, sum(GROUP_ROWS), LANES), F32),
                        pltpu.VMEM((PAIRS_PER_STEP, sum(GROUP_ROWS), LANES), F32),
                        pltpu.VMEM((PAIRS_PER_STEP, sum(GROUP_ROWS), LANES), F32)],
        compiler_params=_params(("arbitrary", "arbitrary")),
        name="dilated_attention",
    )(*args)


MLP_TM = 512
MLP_FC = 1024
MLP_STAGES = 2 + 2 * (D_FF // MLP_FC)


_DONE = object()


def _interleave(order, **stage_generators):
    for key in order:
        next(stage_generators[key], _DONE)
    for gen in stage_generators.values():
        assert next(gen, _DONE) is _DONE, "order does not cover every stage"


def _mlp_stages(read_x, gpre_ref, gpost_ref, wup_ref, wdn_ref, o_ref):
    h = _rms(read_x(), gpre_ref[...]).astype(BF16)
    acc = jnp.zeros((MLP_TM, D_MODEL), F32)
    yield
    for c in range(D_FF // MLP_FC):
        cols = slice(c * MLP_FC, (c + 1) * MLP_FC)
        u = jnp.dot(h, wup_ref[:, cols].astype(BF16), preferred_element_type=F32)
        u = jnp.square(jnp.maximum(u, 0.0)).astype(BF16)
        yield
        acc = acc + jnp.dot(u, wdn_ref[cols, :].astype(BF16), preferred_element_type=F32)
        yield
    o_ref[...] = read_x() + _rms(acc, gpost_ref[...])


def _proj_mlp_kernel(a_ref, wo_ref, gmix_ref, x_ref, gpre_ref, gpost_ref, wup_ref, wdn_ref,
                     *refs, n_cast):
    f32_refs, o_ref, bf16_refs = refs[:n_cast], refs[n_cast], refs[n_cast + 1:]
    for src_ref, dst_ref in zip(f32_refs, bf16_refs):
        dst_ref[...] = src_ref[...].astype(BF16)
    m = jnp.dot(a_ref[...], wo_ref[...].astype(BF16), preferred_element_type=F32)
    x = x_ref[...] + _rms(m, gmix_ref[...])
    _interleave("m" * MLP_STAGES,
                m=_mlp_stages(lambda: x, gpre_ref, gpost_ref, wup_ref, wdn_ref, o_ref))


def _proj_mlp(x2d, a2d, w_o, g_mix, g_pre, g_post, w_up, w_down, to_bf16):
    n = x2d.shape[0]
    steps = n // MLP_TM
    resident = pl.Buffered(1)
    tile = pl.BlockSpec((MLP_TM, D_MODEL), lambda i: (i, 0))
    gain = pl.BlockSpec((1, D_MODEL), lambda i: (0, 0))
    cast_in, cast_out, cast_shape = [], [], []
    for w, layer in to_bf16:
        _, rows, cols = w.shape
        cast_in.append(pl.BlockSpec((None, rows // steps, cols), lambda i, layer=layer: (layer, i, 0)))
        cast_out.append(pl.BlockSpec((rows // steps, cols), lambda i: (i, 0)))
        cast_shape.append(jax.ShapeDtypeStruct((rows, cols), BF16))
    out = pl.pallas_call(
        functools.partial(_proj_mlp_kernel, n_cast=len(to_bf16)),
        grid=(steps,),
        in_specs=[tile,
                  pl.BlockSpec((None, D_MODEL, D_MODEL), lambda i: (0, 0, 0), pipeline_mode=resident),
                  gain, tile, gain, gain,
                  pl.BlockSpec((None, D_MODEL, D_FF), lambda i: (0, 0, 0), pipeline_mode=resident),
                  pl.BlockSpec((None, D_FF, D_MODEL), lambda i: (0, 0, 0), pipeline_mode=resident)]
        + cast_in,
        out_specs=[tile] + cast_out,
        out_shape=[jax.ShapeDtypeStruct((n, D_MODEL), F32)] + cast_shape,
        compiler_params=_params(("arbitrary",)),
        name="attn_proj_mlp",
    )(a2d, w_o, g_mix.reshape(1, D_MODEL), x2d, g_pre.reshape(1, D_MODEL),
      g_post.reshape(1, D_MODEL), w_up, w_down, *[w for w, _ in to_bf16])
    return out[0], out[1:]


CONV_TS = 512
HALO = 16
CONV_ROWS = CONV_TS + 2 * HALO
CONV_RB = 128
CONV_NC = 256


N_TILES = BATCH * SEQ // CONV_TS
TILES_PER_SEQ = SEQ // CONV_TS
assert CONV_TS == MLP_TM


def _conv_stages(xm_ref, xt_ref, xb_ref, first, last, gpre_ref, w1_ref, b1_ref, wdw_ref, bdw_ref,
                 lng_ref, lnb_ref, w2_ref, b2_ref, gpost_ref, o_ref, h_ref, u_ref, c_ref):
    gpre = gpre_ref[...]
    h_ref[0:HALO, :] = _rms(xt_ref[0], gpre).astype(BF16)
    h_ref[HALO:HALO + CONV_TS, :] = _rms(xm_ref[0], gpre).astype(BF16)
    h_ref[HALO + CONV_TS:, :] = _rms(xb_ref[0], gpre).astype(BF16)
    yield

    h = h_ref[...]
    half = CONV_RB // 2
    for nc in range(D_MODEL // CONV_NC):
        cols = slice(nc * CONV_NC, (nc + 1) * CONV_NC)
        gcols = slice(D_MODEL + nc * CONV_NC, D_MODEL + (nc + 1) * CONV_NC)
        a = jnp.dot(h, w1_ref[:, cols], preferred_element_type=F32) + b1_ref[:, cols]
        gate = jnp.dot(h, w1_ref[:, gcols], preferred_element_type=F32) + b1_ref[:, gcols]
        u = a * jax.nn.sigmoid(gate)
        for k in range(CONV_NC // LANES):
            lc = nc * CONV_NC // LANES + k
            lanes = slice(lc * LANES, (lc + 1) * LANES)
            uk = u[:, k * LANES:(k + 1) * LANES]
            u_ref[lc, 0:HALO, :] = jnp.where(first, 0.0, uk[0:HALO])
            u_ref[lc, HALO:HALO + CONV_TS, :] = uk[HALO:HALO + CONV_TS]
            u_ref[lc, HALO + CONV_TS:, :] = jnp.where(last, 0.0, uk[HALO + CONV_TS:])
        yield
        for k in range(CONV_NC // LANES):
            lc = nc * CONV_NC // LANES + k
            lanes = slice(lc * LANES, (lc + 1) * LANES)
            for r0 in range(0, CONV_TS, CONV_RB):
                for phase in range(2):
                    acc = bdw_ref[:, lanes]
                    for t in range(CONV_WIDTH):
                        start = r0 + phase + t + HALO - CONV_WIDTH // 2
                        acc = acc + (u_ref[lc, pl.ds(start, half, stride=2), :]
                                     * wdw_ref[t:t + 1, lanes])
                    c_ref[lc, pl.ds(r0 + phase, half, stride=2), :] = acc
            yield

    v = jnp.concatenate([c_ref[lc] for lc in range(D_MODEL // LANES)], axis=1)
    mu = jnp.mean(v, axis=-1, keepdims=True)
    var = jnp.mean(jnp.square(v - mu), axis=-1, keepdims=True)
    y = (v - mu) * lax.rsqrt(var + LN_EPS) * lng_ref[...] + lnb_ref[...]
    y = (y * jax.nn.sigmoid(y)).astype(BF16)
    yield
    z = jnp.dot(y, w2_ref[...], preferred_element_type=F32) + b2_ref[...]
    o_ref[...] = xm_ref[0] + _rms(z, gpost_ref[...])


def _conv_mlp_kernel(xm_ref, xt_ref, xb_ref, gpre_ref, w1_ref, b1_ref, wdw_ref, bdw_ref, lng_ref,
                     lnb_ref, w2_ref, b2_ref, gpost_ref, g2pre_ref, g2post_ref, wup_ref, wdn_ref,
                     o_ref, h_ref, u_ref, c_ref, x_ref, xnew_ref):
    s = pl.program_id(0)

    @pl.when(s == 0)
    def _():
        x_ref[...] = jnp.zeros((CONV_TS, D_MODEL), F32)

    j = jnp.minimum(s, N_TILES - 1) % TILES_PER_SEQ
    order = "ccccc" + "mm" + "cmccm" * 2 + "cmcm" * 2
    _interleave(
        order,
        c=_conv_stages(xm_ref, xt_ref, xb_ref, j == 0, j == TILES_PER_SEQ - 1, gpre_ref, w1_ref,
                       b1_ref, wdw_ref, bdw_ref, lng_ref, lnb_ref, w2_ref, b2_ref, gpost_ref,
                       xnew_ref, h_ref, u_ref, c_ref),
        m=_mlp_stages(lambda: x_ref[...], g2pre_ref, g2post_ref, wup_ref, wdn_ref, o_ref))
    x_ref[...] = xnew_ref[...]


def _conv_mlp_layer(x, g_pre, w1, b1, wdw, bdw, lng, lnb, w2, b2, g_post, g2_pre, g2_post,
                    w_up, w_down):
    row = lambda v: v.reshape(1, -1)
    resident = pl.Buffered(1)
    const = lambda shape, **kw: pl.BlockSpec(shape, lambda s: (0,) * len(shape), **kw)
    halo_blocks = CONV_TS // HALO

    def tile(s):
        t = jnp.minimum(s, N_TILES - 1)
        return t // TILES_PER_SEQ, t % TILES_PER_SEQ

    def main_map(s):
        b, j = tile(s)
        return b, j, 0

    def top_map(s):
        b, j = tile(s)
        return b, jnp.maximum(j * halo_blocks - 1, 0), 0

    def bot_map(s):
        b, j = tile(s)
        return b, jnp.minimum((j + 1) * halo_blocks, SEQ // HALO - 1), 0

    gain = const((1, D_MODEL))
    return pl.pallas_call(
        _conv_mlp_kernel,
        grid=(N_TILES + 1,),
        in_specs=[pl.BlockSpec((1, CONV_TS, D_MODEL), main_map),
                  pl.BlockSpec((1, HALO, D_MODEL), top_map),
                  pl.BlockSpec((1, HALO, D_MODEL), bot_map),
                  gain,
                  const((D_MODEL, 2 * D_MODEL), pipeline_mode=resident),
                  const((1, 2 * D_MODEL)),
                  const((CONV_WIDTH, D_MODEL)),
                  gain, gain, gain,
                  const((D_MODEL, D_MODEL), pipeline_mode=resident),
                  gain, gain, gain, gain,
                  const((D_MODEL, D_FF), pipeline_mode=resident),
                  const((D_FF, D_MODEL), pipeline_mode=resident)],
        out_specs=pl.BlockSpec((CONV_TS, D_MODEL), lambda s: (jnp.maximum(s - 1, 0), 0)),
        out_shape=jax.ShapeDtypeStruct((BATCH * SEQ, D_MODEL), F32),
        scratch_shapes=[pltpu.VMEM((CONV_ROWS, D_MODEL), BF16),
                        pltpu.VMEM((D_MODEL // LANES, CONV_ROWS, LANES), F32),
                        pltpu.VMEM((D_MODEL // LANES, CONV_TS, LANES), F32),
                        pltpu.VMEM((CONV_TS, D_MODEL), F32),
                        pltpu.VMEM((CONV_TS, D_MODEL), F32)],
        compiler_params=_params(("arbitrary",)),
        name="conformer_conv_mlp",
    )(x, x, x, row(g_pre), w1, row(b1), wdw, row(bdw), row(lng), row(lnb), w2, row(b2),
      row(g_post), row(g2_pre), row(g2_post), w_up, w_down)


def kernel(x, rel_bias, norm_mix_pre, norm_mix_post, norm_mlp_pre, norm_mlp_post, attn_w_qkv,
           attn_w_o, conv_w_pw1, conv_b_pw1, conv_w_dw, conv_b_dw, conv_ln_g, conv_ln_b,
           conv_w_pw2, conv_b_pw2, mlp_w_up, mlp_w_down):
    n_tok = BATCH * SEQ
    x2d = x.reshape(n_tok, D_MODEL)

    h = _prenorm(x, norm_mix_pre[0])
    w_qkv = attn_w_qkv.reshape(D_MODEL, N_GROUPS * GROUP_WIDTH)
    qkv = [_qkv_proj(h[g], w_qkv, g) for g in range(N_GROUPS)]
    a = _attention(qkv, rel_bias)
    x2d, (w_pw1, w_pw2, w_up1, w_down1) = _proj_mlp(
        x2d, a.reshape(n_tok, D_MODEL), attn_w_o, norm_mix_post[0], norm_mlp_pre[0],
        norm_mlp_post[0], mlp_w_up, mlp_w_down,
        to_bf16=[(conv_w_pw1, 0), (conv_w_pw2, 0), (mlp_w_up, 1), (mlp_w_down, 1)])

    x2d = _conv_mlp_layer(x2d.reshape(BATCH, SEQ, D_MODEL), norm_mix_pre[1], w_pw1, conv_b_pw1[0],
                          conv_w_dw[0], conv_b_dw[0], conv_ln_g[0], conv_ln_b[0], w_pw2,
                          conv_b_pw2[0], norm_mix_post[1], norm_mlp_pre[1], norm_mlp_post[1],
                          w_up1, w_down1)
    return x2d.reshape(BATCH, SEQ, D_MODEL)
```

```python
import functools
import math

import jax
import jax.numpy as jnp
from jax import lax
from jax.experimental import pallas as pl
from jax.experimental.pallas import tpu as pltpu

D_MODEL = 1024
BATCH = 8
SEQ = 2048
HEAD_DIM = 64
N_HEADS = 16
DILATIONS = (1, 4, 16)
N_SIDE = 64
N_GROUPS = 3
GROUP_WIDTH = 3 * N_HEADS * HEAD_DIM
N_BUCKETS = 32
MAX_DISTANCE = 1024
CONV_WIDTH = 31
D_FF = 4 * D_MODEL
RMS_EPS = 1e-6
LN_EPS = 1e-5
NEG_INF = -1e30

F32 = jnp.float32
BF16 = jnp.bfloat16

LANES = 128
Q_TILE = 128
K_TILE = Q_TILE + 2 * N_SIDE
VMEM_LIMIT = 56 * 1024 * 1024


def _rms(x, g):
    return x * lax.rsqrt(jnp.mean(x * x, axis=-1, keepdims=True) + RMS_EPS) * g


def _params(semantics):
    return pltpu.CompilerParams(dimension_semantics=semantics, vmem_limit_bytes=VMEM_LIMIT)


NORM_ROWS = 256


def _prenorm_kernel(x_ref, g_ref, *refs):
    out_refs, slab_ref, regroup_ref = refs[:-2], refs[-2], refs[-1]
    assert DILATIONS == (1, 4, 16)
    out0_ref, out1_ref, out2_ref = out_refs

    def chunk(i, carry):
        rows = pl.ds(pl.multiple_of(i * NORM_ROWS, NORM_ROWS), NORM_ROWS)
        hn = _rms(x_ref[0, rows, :], g_ref[...])
        for k in range(D_MODEL // LANES):
            slab_ref[k, rows, :] = hn[:, k * LANES:(k + 1) * LANES]
        out0_ref[0, rows, :] = hn.astype(BF16)
        return carry

    lax.fori_loop(0, SEQ // NORM_ROWS, chunk, 0)
    r1 = DILATIONS[1]
    L1, L2 = SEQ // r1, SEQ // DILATIONS[2]
    for k in range(D_MODEL // LANES):
        lanes = slice(k * LANES, (k + 1) * LANES)
        for c in range(r1):
            v = slab_ref[k, pl.ds(c, L1, stride=r1), :]
            out1_ref[0, c * L1:(c + 1) * L1, lanes] = v.astype(BF16)
            regroup_ref[c * L1:(c + 1) * L1, :] = v
        for c in range(DILATIONS[2]):
            start = (c % r1) * L1 + c // r1
            out2_ref[0, c * L2:(c + 1) * L2, lanes] = (
                regroup_ref[pl.ds(start, L2, stride=r1), :].astype(BF16))


def _prenorm(x, g):
    blk = pl.BlockSpec((1, SEQ, D_MODEL), lambda b: (b, 0, 0))
    return pl.pallas_call(
        _prenorm_kernel,
        grid=(BATCH,),
        in_specs=[blk, pl.BlockSpec((1, D_MODEL), lambda b: (0, 0))],
        out_specs=[blk] * N_GROUPS,
        out_shape=[jax.ShapeDtypeStruct((BATCH, SEQ, D_MODEL), BF16)] * N_GROUPS,
        scratch_shapes=[pltpu.VMEM((D_MODEL // LANES, SEQ, LANES), F32),
                        pltpu.VMEM((SEQ, LANES), F32)],
        compiler_params=_params(("arbitrary",)),
        name="prenorm",
    )(x, g.reshape(1, D_MODEL))


QKV_TN = 1536
QKV_TM = 512
Q_WIDTH = N_HEADS * HEAD_DIM
assert QKV_TN >= Q_WIDTH
LOG2E = math.log2(math.e)
Q_SCALE = HEAD_DIM ** -0.5 * LOG2E


def _qkv_kernel(h_ref, w_ref, o_ref, wb_ref):
    @pl.when(pl.program_id(1) == 0)
    def _():
        wb_ref[...] = w_ref[...].astype(BF16)

    col = lax.broadcasted_iota(jnp.int32, (1, QKV_TN), 1)
    scale = jnp.where((pl.program_id(0) == 0) & (col < Q_WIDTH), Q_SCALE, 1.0)
    for m in range(SEQ // QKV_TM):
        rows = slice(m * QKV_TM, (m + 1) * QKV_TM)
        res = jnp.dot(h_ref[0, rows, :], wb_ref[...], preferred_element_type=F32)
        o_ref[0, rows, :] = (res * scale).astype(BF16)


def _qkv_proj(h, w_qkv, g):
    nj = GROUP_WIDTH // QKV_TN
    return pl.pallas_call(
        _qkv_kernel,
        grid=(nj, BATCH),
        in_specs=[pl.BlockSpec((1, SEQ, D_MODEL), lambda j, b: (b, 0, 0)),
                  pl.BlockSpec((D_MODEL, QKV_TN), lambda j, b: (0, g * nj + j))],
        out_specs=pl.BlockSpec((1, SEQ, QKV_TN), lambda j, b: (b, 0, j)),
        out_shape=jax.ShapeDtypeStruct((BATCH, SEQ, GROUP_WIDTH), BF16),
        scratch_shapes=[pltpu.VMEM((D_MODEL, QKV_TN), BF16)],
        compiler_params=_params(("arbitrary", "arbitrary")),
        name=f"qkv_proj_g{g}",
    )(h, w_qkv)


HEADS_PER_STEP = 4
PAIRS_PER_STEP = HEADS_PER_STEP // 2
STEP_LANES = HEADS_PER_STEP * HEAD_DIM
N_Q_TILES = SEQ // Q_TILE
N_KINDS = 3
PADDED_PITCH = 24
GROUP_ROWS = tuple(SEQ // 16 * PADDED_PITCH if r % 16 == 0 else SEQ for r in DILATIONS)
GROUP_BASE = tuple(sum(GROUP_ROWS[:g]) for g in range(N_GROUPS))


def _bias_row(rel_ref, g, head):
    r = DILATIONS[g]
    nb = N_BUCKETS // 2
    max_exact = nb // 2
    delta = lax.broadcasted_iota(jnp.int32, (8, K_TILE), 1) - N_SIDE
    rel = delta * r
    n = jnp.abs(rel)
    nf = jnp.maximum(n, 1).astype(F32)
    large = max_exact + (jnp.log(nf / max_exact) / math.log(MAX_DISTANCE / max_exact)
                         * (nb - max_exact)).astype(jnp.int32)
    large = jnp.minimum(large, nb - 1)
    bucket = jnp.where(rel > 0, nb, 0) + jnp.where(n < max_exact, n, large)
    u = jnp.zeros((8, K_TILE), F32)
    for b in range(N_BUCKETS):
        u = jnp.where(bucket == b, rel_ref[b, g * N_HEADS + head], u)
    return jnp.where(jnp.abs(delta) <= N_SIDE, u * LOG2E, NEG_INF)


def _build_bias_tiles(rel_ref, bias_ref, head0):
    qi = lax.broadcasted_iota(jnp.int32, (Q_TILE, K_TILE), 0)
    kj = lax.broadcasted_iota(jnp.int32, (Q_TILE, K_TILE), 1)
    for g in range(N_GROUPS):
        for h in range(HEADS_PER_STEP):
            u = _bias_row(rel_ref, g, head0 + h)
            ub = jnp.broadcast_to(u[0:1, :], (Q_TILE, K_TILE))
            for kind in range(N_KINDS):
                shift = (N_SIDE * kind - N_SIDE) % K_TILE
                t = pltpu.roll(ub, shift, 1, stride=1, stride_axis=0)
                in_band = jnp.abs(kj - qi - N_SIDE * kind) <= N_SIDE
                bias_ref[g, h, kind] = jnp.where(in_band, t, NEG_INF)


def _token_rows(g, c, l0):
    r = DILATIONS[g]
    if r == 1:
        return pl.ds(GROUP_BASE[g] + l0, Q_TILE)
    pitch = PADDED_PITCH if r % 16 == 0 else r
    return pl.ds(GROUP_BASE[g] + l0 * pitch + c, Q_TILE, stride=pitch)


def _load_tokens(ref, pair, g, t):
    r = DILATIONS[g]
    if r % 16:
        return ref[pair, pl.ds(pl.multiple_of(GROUP_BASE[g] + t * Q_TILE, Q_TILE), Q_TILE), :]
    n = Q_TILE // r
    base = pl.multiple_of(GROUP_BASE[g] + t * n * PADDED_PITCH, 8)
    return jnp.concatenate([ref[pair, pl.ds(base + i * PADDED_PITCH, r), :] for i in range(n)],
                           axis=0)


def _split_heads(q):
    first = lax.broadcasted_iota(jnp.int32, (1, LANES), 1) < HEAD_DIM
    zero = jnp.zeros_like(q)
    return jnp.concatenate([jnp.where(first, q, zero), jnp.where(first, zero, q)], axis=0)


def _softmax_pv(s, v):
    first = lax.broadcasted_iota(jnp.int32, (1, LANES), 1) < HEAD_DIM
    m = jnp.max(s, axis=-1, keepdims=True)
    p = jnp.exp2(s - m).astype(BF16)
    v1 = jnp.concatenate([v, jnp.ones((v.shape[0], LANES), BF16)], axis=1)
    o2 = jnp.dot(p, v1, preferred_element_type=F32)
    acc = jnp.where(first, o2[:Q_TILE, :LANES], o2[Q_TILE:, :LANES])
    l = jnp.where(first, o2[:Q_TILE, LANES:], o2[Q_TILE:, LANES:])
    m_t = jnp.where(first, m[:Q_TILE], m[Q_TILE:])
    return acc, m_t, l


_NT = (((1,), (1,)), ((), ()))


def _attn_kernel(rel_ref, q0_ref, k0_ref, v0_ref, q1_ref, k1_ref, v1_ref, q2_ref, k2_ref, v2_ref,
                 o_ref, bias_ref, acc_ref, m_ref, l_ref):
    hp = pl.program_id(0)

    @pl.when(pl.program_id(1) == 0)
    def _():
        _build_bias_tiles(rel_ref, bias_ref, hp * HEADS_PER_STEP)

    qkv = ((q0_ref, k0_ref, v0_ref), (q1_ref, k1_ref, v1_ref), (q2_ref, k2_ref, v2_ref))
    for pair in range(PAIRS_PER_STEP):
        lanes = slice(pair * LANES, (pair + 1) * LANES)
        for g in range(N_GROUPS):
            r = DILATIONS[g]
            tiles_per_seq = SEQ // r // Q_TILE
            q_ref, k_ref, v_ref = qkv[g]

            def put(t, res, g=g, tiles_per_seq=tiles_per_seq, pair=pair):
                rows = _token_rows(g, t // tiles_per_seq, (t % tiles_per_seq) * Q_TILE)
                for ref, val in zip((acc_ref, m_ref, l_ref), res):
                    ref[pair, rows, :] = val

            if tiles_per_seq > 1:
                for t in range(N_Q_TILES):
                    tl = t % tiles_per_seq
                    kind = 0 if tl == 0 else (2 if tl == tiles_per_seq - 1 else 1)
                    q0 = t * Q_TILE
                    k0 = q0 - N_SIDE * kind
                    q2 = _split_heads(q_ref[0, q0:q0 + Q_TILE, lanes])
                    s = lax.dot_general(q2, k_ref[0, k0:k0 + K_TILE, lanes], _NT,
                                        preferred_element_type=F32)
                    s = s + jnp.concatenate([bias_ref[g, 2 * pair, kind],
                                             bias_ref[g, 2 * pair + 1, kind]], axis=0)
                    put(t, _softmax_pv(s, v_ref[0, k0:k0 + K_TILE, lanes]))
            else:
                bias = jnp.concatenate([bias_ref[g, 2 * pair, 0, :, :Q_TILE],
                                        bias_ref[g, 2 * pair + 1, 0, :, :Q_TILE]], axis=0)
                for t in range(0, N_Q_TILES, 2):
                    q0 = t * Q_TILE
                    q = q_ref[0, q0:q0 + 2 * Q_TILE, lanes]
                    q2 = jnp.concatenate([_split_heads(q[:Q_TILE]), _split_heads(q[Q_TILE:])], axis=0)
                    s = lax.dot_general(q2, k_ref[0, q0:q0 + 2 * Q_TILE, lanes], _NT,
                                        preferred_element_type=F32)
                    for i in range(2):
                        si = s[2 * i * Q_TILE:2 * (i + 1) * Q_TILE, i * Q_TILE:(i + 1) * Q_TILE]
                        v = v_ref[0, q0 + i * Q_TILE:q0 + (i + 1) * Q_TILE, lanes]
                        put(t + i, _softmax_pv(si + bias, v))

    for pair in range(PAIRS_PER_STEP):
        lanes = slice(pair * LANES, (pair + 1) * LANES)

        def merge_body(t, carry, lanes=lanes, pair=pair):
            m = [_load_tokens(m_ref, pair, g, t) for g in range(N_GROUPS)]
            top = jnp.maximum(jnp.maximum(m[0], m[1]), m[2])
            w = [jnp.exp2(x - top) for x in m]
            num = sum(w[g] * _load_tokens(acc_ref, pair, g, t) for g in range(N_GROUPS))
            den = sum(w[g] * _load_tokens(l_ref, pair, g, t) for g in range(N_GROUPS))
            rows = pl.ds(pl.multiple_of(t * Q_TILE, Q_TILE), Q_TILE)
            o_ref[0, rows, lanes] = (num / den).astype(BF16)
            return carry

        lax.fori_loop(0, N_Q_TILES, merge_body, 0, unroll=4)


def _attention(qkv, rel_bias):
    n_hp = N_HEADS // HEADS_PER_STEP
    blk = (1, SEQ, STEP_LANES)
    specs = [pl.BlockSpec(memory_space=pltpu.SMEM)]
    args = [rel_bias]
    for g in range(N_GROUPS):
        for part in range(3):
            specs.append(pl.BlockSpec(blk, lambda hp, b, part=part: (b, 0, part * n_hp + hp)))
            args.append(qkv[g])
    return pl.pallas_call(
        _attn_kernel,
        grid=(n_hp, BATCH),
        in_specs=specs,
        out_specs=pl.BlockSpec(blk, lambda hp, b: (b, 0, hp)),
        out_shape=jax.ShapeDtypeStruct((BATCH, SEQ, D_MODEL), BF16),
        scratch_shapes=[pltpu.VMEM((N_GROUPS, HEADS_PER_STEP, N_KINDS, Q_TILE, K_TILE), F32),
                        pltpu.VMEM((PAIRS_PER_STEP, sum(GROUP_ROWS), LANES), F32),
                        pltpu.VMEM((PAIRS_PER_STEP, sum(GROUP_ROWS), LANES), F32),
                        pltpu.VMEM((PAIRS_PER_STEP, sum(GROUP_ROWS), LANES), F32)],
        compiler_params=_params(("arbitrary", "arbitrary")),
        name="dilated_attention",
    )(*args)


MLP_TM = 512
MLP_FC = 1024
MLP_STAGES = 2 + 2 * (D_FF // MLP_FC)


_DONE = object()


def _interleave(order, **stage_generators):
    for key in order:
        next(stage_generators[key], _DONE)
    for gen in stage_generators.values():
        assert next(gen, _DONE) is _DONE, "order does not cover every stage"


def _mlp_stages(read_x, gpre_ref, gpost_ref, wup_ref, wdn_ref, o_ref):
    h = _rms(read_x(), gpre_ref[...]).astype(BF16)
    acc = jnp.zeros((MLP_TM, D_MODEL), F32)
    yield
    for c in range(D_FF // MLP_FC):
        cols = slice(c * MLP_FC, (c + 1) * MLP_FC)
        u = jnp.dot(h, wup_ref[:, cols].astype(BF16), preferred_element_type=F32)
        u = jnp.square(jnp.maximum(u, 0.0)).astype(BF16)
        yield
        acc = acc + jnp.dot(u, wdn_ref[cols, :].astype(BF16), preferred_element_type=F32)
        yield
    o_ref[...] = read_x() + _rms(acc, gpost_ref[...])


def _proj_mlp_kernel(a_ref, wo_ref, gmix_ref, x_ref, gpre_ref, gpost_ref, wup_ref, wdn_ref,
                     *refs, n_cast):
    f32_refs, o_ref, bf16_refs = refs[:n_cast], refs[n_cast], refs[n_cast + 1:]
    for src_ref, dst_ref in zip(f32_refs, bf16_refs):
        dst_ref[...] = src_ref[...].astype(BF16)
    m = jnp.dot(a_ref[...], wo_ref[...].astype(BF16), preferred_element_type=F32)
    x = x_ref[...] + _rms(m, gmix_ref[...])
    _interleave("m" * MLP_STAGES,
                m=_mlp_stages(lambda: x, gpre_ref, gpost_ref, wup_ref, wdn_ref, o_ref))


def _proj_mlp(x2d, a2d, w_o, g_mix, g_pre, g_post, w_up, w_down, to_bf16):
    n = x2d.shape[0]
    steps = n // MLP_TM
    resident = pl.Buffered(1)
    tile = pl.BlockSpec((MLP_TM, D_MODEL), lambda i: (i, 0))
    gain = pl.BlockSpec((1, D_MODEL), lambda i: (0, 0))
    cast_in, cast_out, cast_shape = [], [], []
    for w, layer in to_bf16:
        _, rows, cols = w.shape
        cast_in.append(pl.BlockSpec((None, rows // steps, cols), lambda i, layer=layer: (layer, i, 0)))
        cast_out.append(pl.BlockSpec((rows // steps, cols), lambda i: (i, 0)))
        cast_shape.append(jax.ShapeDtypeStruct((rows, cols), BF16))
    out = pl.pallas_call(
        functools.partial(_proj_mlp_kernel, n_cast=len(to_bf16)),
        grid=(steps,),
        in_specs=[tile,
                  pl.BlockSpec((None, D_MODEL, D_MODEL), lambda i: (0, 0, 0), pipeline_mode=resident),
                  gain, tile, gain, gain,
                  pl.BlockSpec((None, D_MODEL, D_FF), lambda i: (0, 0, 0), pipeline_mode=resident),
                  pl.BlockSpec((None, D_FF, D_MODEL), lambda i: (0, 0, 0), pipeline_mode=resident)]
        + cast_in,
        out_specs=[tile] + cast_out,
        out_shape=[jax.ShapeDtypeStruct((n, D_MODEL), F32)] + cast_shape,
        compiler_params=_params(("arbitrary",)),
        name="attn_proj_mlp",
    )(a2d, w_o, g_mix.reshape(1, D_MODEL), x2d, g_pre.reshape(1, D_MODEL),
      g_post.reshape(1, D_MODEL), w_up, w_down, *[w for w, _ in to_bf16])
    return out[0], out[1:]


CONV_TS = 512
HALO = 16
CONV_ROWS = CONV_TS + 2 * HALO
CONV_RB = 128
CONV_NC = 256


N_TILES = BATCH * SEQ // CONV_TS
TILES_PER_SEQ = SEQ // CONV_TS
assert CONV_TS == MLP_TM


def _conv_stages(xm_ref, xt_ref, xb_ref, first, last, gpre_ref, w1_ref, b1_ref, wdw_ref, bdw_ref,
                 lng_ref, lnb_ref, w2_ref, b2_ref, gpost_ref, o_ref, h_ref, u_ref, c_ref):
    gpre = gpre_ref[...]
    h_ref[0:HALO, :] = _rms(xt_ref[0], gpre).astype(BF16)
    h_ref[HALO:HALO + CONV_TS, :] = _rms(xm_ref[0], gpre).astype(BF16)
    h_ref[HALO + CONV_TS:, :] = _rms(xb_ref[0], gpre).astype(BF16)
    yield

    h = h_ref[...]
    half = CONV_RB // 2
    for nc in range(D_MODEL // CONV_NC):
        cols = slice(nc * CONV_NC, (nc + 1) * CONV_NC)
        gcols = slice(D_MODEL + nc * CONV_NC, D_MODEL + (nc + 1) * CONV_NC)
        a = jnp.dot(h, w1_ref[:, cols], preferred_element_type=F32) + b1_ref[:, cols]
        gate = jnp.dot(h, w1_ref[:, gcols], preferred_element_type=F32) + b1_ref[:, gcols]
        u = a * jax.nn.sigmoid(gate)
        for k in range(CONV_NC // LANES):
            lc = nc * CONV_NC // LANES + k
            lanes = slice(lc * LANES, (lc + 1) * LANES)
            uk = u[:, k * LANES:(k + 1) * LANES]
            u_ref[lc, 0:HALO, :] = jnp.where(first, 0.0, uk[0:HALO])
            u_ref[lc, HALO:HALO + CONV_TS, :] = uk[HALO:HALO + CONV_TS]
            u_ref[lc, HALO + CONV_TS:, :] = jnp.where(last, 0.0, uk[HALO + CONV_TS:])
        yield
        for k in range(CONV_NC // LANES):
            lc = nc * CONV_NC // LANES + k
            lanes = slice(lc * LANES, (lc + 1) * LANES)
            for r0 in range(0, CONV_TS, CONV_RB):
                for phase in range(2):
                    acc = bdw_ref[:, lanes]
                    for t in range(CONV_WIDTH):
                        start = r0 + phase + t + HALO - CONV_WIDTH // 2
                        acc = acc + (u_ref[lc, pl.ds(start, half, stride=2), :]
                                     * wdw_ref[t:t + 1, lanes])
                    c_ref[lc, pl.ds(r0 + phase, half, stride=2), :] = acc
            yield

    v = jnp.concatenate([c_ref[lc] for lc in range(D_MODEL // LANES)], axis=1)
    mu = jnp.mean(v, axis=-1, keepdims=True)
    var = jnp.mean(jnp.square(v - mu), axis=-1, keepdims=True)
    y = (v - mu) * lax.rsqrt(var + LN_EPS) * lng_ref[...] + lnb_ref[...]
    y = (y * jax.nn.sigmoid(y)).astype(BF16)
    yield
    z = jnp.dot(y, w2_ref[...], preferred_element_type=F32) + b2_ref[...]
    o_ref[...] = xm_ref[0] + _rms(z, gpost_ref[...])


def _conv_mlp_kernel(xm_ref, xt_ref, xb_ref, gpre_ref, w1_ref, b1_ref, wdw_ref, bdw_ref, lng_ref,
                     lnb_ref, w2_ref, b2_ref, gpost_ref, g2pre_ref, g2post_ref, wup_ref, wdn_ref,
                     o_ref, h_ref, u_ref, c_ref, x_ref, xnew_ref):
    s = pl.program_id(0)

    @pl.when(s == 0)
    def _():
        x_ref[...] = jnp.zeros((CONV_TS, D_MODEL), F32)

    j = jnp.minimum(s, N_TILES - 1) % TILES_PER_SEQ
    order = "ccccc" + "mm" + "cmccm" * 2 + "cmcm" * 2
    _interleave(
        order,
        c=_conv_stages(xm_ref, xt_ref, xb_ref, j == 0, j == TILES_PER_SEQ - 1, gpre_ref, w1_ref,
                       b1_ref, wdw_ref, bdw_ref, lng_ref, lnb_ref, w2_ref, b2_ref, gpost_ref,
                       xnew_ref, h_ref, u_ref, c_ref),
        m=_mlp_stages(lambda: x_ref[...], g2pre_ref, g2post_ref, wup_ref, wdn_ref, o_ref))
    x_ref[...] = xnew_ref[...]


def _conv_mlp_layer(x, g_pre, w1, b1, wdw, bdw, lng, lnb, w2, b2, g_post, g2_pre, g2_post,
                    w_up, w_down):
    row = lambda v: v.reshape(1, -1)
    resident = pl.Buffered(1)
    const = lambda shape, **kw: pl.BlockSpec(shape, lambda s: (0,) * len(shape), **kw)
    halo_blocks = CONV_TS // HALO

    def tile(s):
        t = jnp.minimum(s, N_TILES - 1)
        return t // TILES_PER_SEQ, t % TILES_PER_SEQ

    def main_map(s):
        b, j = tile(s)
        return b, j, 0

    def top_map(s):
        b, j = tile(s)
        return b, jnp.maximum(j * halo_blocks - 1, 0), 0

    def bot_map(s):
        b, j = tile(s)
        return b, jnp.minimum((j + 1) * halo_blocks, SEQ // HALO - 1), 0

    gain = const((1, D_MODEL))
    return pl.pallas_call(
        _conv_mlp_kernel,
        grid=(N_TILES + 1,),
        in_specs=[pl.BlockSpec((1, CONV_TS, D_MODEL), main_map),
                  pl.BlockSpec((1, HALO, D_MODEL), top_map),
                  pl.BlockSpec((1, HALO, D_MODEL), bot_map),
                  gain,
                  const((D_MODEL, 2 * D_MODEL), pipeline_mode=resident),
                  const((1, 2 * D_MODEL)),
                  const((CONV_WIDTH, D_MODEL)),
                  gain, gain, gain,
                  const((D_MODEL, D_MODEL), pipeline_mode=resident),
                  gain, gain, gain, gain,
                  const((D_MODEL, D_FF), pipeline_mode=resident),
                  const((D_FF, D_MODEL), pipeline_mode=resident)],
        out_specs=pl.BlockSpec((CONV_TS, D_MODEL), lambda s: (jnp.maximum(s - 1, 0), 0)),
        out_shape=jax.ShapeDtypeStruct((BATCH * SEQ, D_MODEL), F32),
        scratch_shapes=[pltpu.VMEM((CONV_ROWS, D_MODEL), BF16),
                        pltpu.VMEM((D_MODEL // LANES, CONV_ROWS, LANES), F32),
                        pltpu.VMEM((D_MODEL // LANES, CONV_TS, LANES), F32),
                        pltpu.VMEM((CONV_TS, D_MODEL), F32),
                        pltpu.VMEM((CONV_TS, D_MODEL), F32)],
        compiler_params=_params(("arbitrary",)),
        name="conformer_conv_mlp",
    )(x, x, x, row(g_pre), w1, row(b1), wdw, row(bdw), row(lng), row(lnb), w2, row(b2),
      row(g_post), row(g2_pre), row(g2_post), w_up, w_down)


def kernel(x, rel_bias, norm_mix_pre, norm_mix_post, norm_mlp_pre, norm_mlp_post, attn_w_qkv,
           attn_w_o, conv_w_pw1, conv_b_pw1, conv_w_dw, conv_b_dw, conv_ln_g, conv_ln_b,
           conv_w_pw2, conv_b_pw2, mlp_w_up, mlp_w_down):
    n_tok = BATCH * SEQ
    x2d = x.reshape(n_tok, D_MODEL)

    h = _prenorm(x, norm_mix_pre[0])
    w_qkv = attn_w_qkv.reshape(D_MODEL, N_GROUPS * GROUP_WIDTH)
    qkv = [_qkv_proj(h[g], w_qkv, g) for g in range(N_GROUPS)]
    a = _attention(qkv, rel_bias)
    x2d, (w_pw1, w_pw2, w_up1, w_down1) = _proj_mlp(
        x2d, a.reshape(n_tok, D_MODEL), attn_w_o, norm_mix_post[0], norm_mlp_pre[0],
        norm_mlp_post[0], mlp_w_up, mlp_w_down,
        to_bf16=[(conv_w_pw1, 0), (conv_w_pw2, 0), (mlp_w_up, 1), (mlp_w_down, 1)])

    x2d = _conv_mlp_layer(x2d.reshape(BATCH, SEQ, D_MODEL), norm_mix_pre[1], w_pw1, conv_b_pw1[0],
                          conv_w_dw[0], conv_b_dw[0], conv_ln_g[0], conv_ln_b[0], w_pw2,
                          conv_b_pw2[0], norm_mix_post[1], norm_mlp_pre[1], norm_mlp_post[1],
                          w_up1, w_down1)
    return x2d.reshape(BATCH, SEQ, D_MODEL)
```

```python
import functools
import math

import jax
import jax.numpy as jnp
from jax import lax
from jax.experimental import pallas as pl
from jax.experimental.pallas import tpu as pltpu

D_MODEL = 1024
BATCH = 8
SEQ = 2048
HEAD_DIM = 64
N_HEADS = 16
DILATIONS = (1, 4, 16)
N_SIDE = 64
N_GROUPS = 3
GROUP_WIDTH = 3 * N_HEADS * HEAD_DIM
N_BUCKETS = 32
MAX_DISTANCE = 1024
CONV_WIDTH = 31
D_FF = 4 * D_MODEL
RMS_EPS = 1e-6
LN_EPS = 1e-5
NEG_INF = -1e30

F32 = jnp.float32
BF16 = jnp.bfloat16

LANES = 128
Q_TILE = 128
K_TILE = Q_TILE + 2 * N_SIDE
VMEM_LIMIT = 56 * 1024 * 1024


def _rms(x, g):
    return x * lax.rsqrt(jnp.mean(x * x, axis=-1, keepdims=True) + RMS_EPS) * g


def _params(semantics):
    return pltpu.CompilerParams(dimension_semantics=semantics, vmem_limit_bytes=VMEM_LIMIT)


NORM_ROWS = 256


def _prenorm_kernel(x_ref, g_ref, *refs):
    out_ref, slab_ref, regroup_ref = refs
    assert DILATIONS == (1, 4, 16)

    def chunk(i, carry):
        rows = pl.ds(pl.multiple_of(i * NORM_ROWS, NORM_ROWS), NORM_ROWS)
        hn = _rms(x_ref[0, rows, :], g_ref[...])
        for k in range(D_MODEL // LANES):
            slab_ref[k, rows, :] = hn[:, k * LANES:(k + 1) * LANES]
        out_ref[0, 0, rows, :] = hn.astype(BF16)
        return carry

    lax.fori_loop(0, SEQ // NORM_ROWS, chunk, 0)
    r1 = DILATIONS[1]
    L1, L2 = SEQ // r1, SEQ // DILATIONS[2]
    for k in range(D_MODEL // LANES):
        lanes = slice(k * LANES, (k + 1) * LANES)
        for c in range(r1):
            v = slab_ref[k, pl.ds(c, L1, stride=r1), :]
            out_ref[1, 0, c * L1:(c + 1) * L1, lanes] = v.astype(BF16)
            regroup_ref[c * L1:(c + 1) * L1, :] = v
        for c in range(DILATIONS[2]):
            start = (c % r1) * L1 + c // r1
            out_ref[2, 0, c * L2:(c + 1) * L2, lanes] = (
                regroup_ref[pl.ds(start, L2, stride=r1), :].astype(BF16))


def _prenorm(x, g):
    return pl.pallas_call(
        _prenorm_kernel,
        grid=(BATCH,),
        in_specs=[pl.BlockSpec((1, SEQ, D_MODEL), lambda b: (b, 0, 0)),
                  pl.BlockSpec((1, D_MODEL), lambda b: (0, 0))],
        out_specs=pl.BlockSpec((N_GROUPS, 1, SEQ, D_MODEL), lambda b: (0, b, 0, 0)),
        out_shape=jax.ShapeDtypeStruct((N_GROUPS, BATCH, SEQ, D_MODEL), BF16),
        scratch_shapes=[pltpu.VMEM((D_MODEL // LANES, SEQ, LANES), F32),
                        pltpu.VMEM((SEQ, LANES), F32)],
        compiler_params=_params(("arbitrary",)),
        name="prenorm",
    )(x, g.reshape(1, D_MODEL))


QKV_TN = 1536
QKV_TM = 512
Q_WIDTH = N_HEADS * HEAD_DIM
assert QKV_TN >= Q_WIDTH
LOG2E = math.log2(math.e)
Q_SCALE = HEAD_DIM ** -0.5 * LOG2E


def _qkv_kernel(h_ref, w_ref, o_ref, wb_ref):
    @pl.when(pl.program_id(1) == 0)
    def _():
        wb_ref[...] = w_ref[...].astype(BF16)

    col = lax.broadcasted_iota(jnp.int32, (1, QKV_TN), 1)
    first_tile = pl.program_id(0) % (GROUP_WIDTH // QKV_TN) == 0
    scale = jnp.where(first_tile & (col < Q_WIDTH), Q_SCALE, 1.0)
    for m in range(SEQ // QKV_TM):
        rows = slice(m * QKV_TM, (m + 1) * QKV_TM)
        res = jnp.dot(h_ref[0, rows, :], wb_ref[...], preferred_element_type=F32)
        o_ref[0, rows, :] = (res * scale).astype(BF16)


def _qkv_proj(h, w_qkv):
    nj = GROUP_WIDTH // QKV_TN
    return pl.pallas_call(
        _qkv_kernel,
        grid=(N_GROUPS * nj, BATCH),
        in_specs=[pl.BlockSpec((None, 1, SEQ, D_MODEL), lambda i, b: (i // nj, b, 0, 0)),
                  pl.BlockSpec((D_MODEL, QKV_TN), lambda i, b: (0, i))],
        out_specs=pl.BlockSpec((None, 1, SEQ, QKV_TN), lambda i, b: (i // nj, b, 0, i % nj)),
        out_shape=jax.ShapeDtypeStruct((N_GROUPS, BATCH, SEQ, GROUP_WIDTH), BF16),
        scratch_shapes=[pltpu.VMEM((D_MODEL, QKV_TN), BF16)],
        compiler_params=_params(("arbitrary", "arbitrary")),
        name="qkv_proj",
    )(h, w_qkv)


HEADS_PER_STEP = 4
PAIRS_PER_STEP = HEADS_PER_STEP // 2
STEP_LANES = HEADS_PER_STEP * HEAD_DIM
N_Q_TILES = SEQ // Q_TILE
N_KINDS = 3
PADDED_PITCH = 24
GROUP_ROWS = tuple(SEQ // 16 * PADDED_PITCH if r % 16 == 0 else SEQ for r in DILATIONS)
GROUP_BASE = tuple(sum(GROUP_ROWS[:g]) for g in range(N_GROUPS))


def _bias_row(rel_ref, g, head):
    r = DILATIONS[g]
    nb = N_BUCKETS // 2
    max_exact = nb // 2
    delta = lax.broadcasted_iota(jnp.int32, (8, K_TILE), 1) - N_SIDE
    rel = delta * r
    n = jnp.abs(rel)
    nf = jnp.maximum(n, 1).astype(F32)
    large = max_exact + (jnp.log(nf / max_exact) / math.log(MAX_DISTANCE / max_exact)
                         * (nb - max_exact)).astype(jnp.int32)
    large = jnp.minimum(large, nb - 1)
    bucket = jnp.where(rel > 0, nb, 0) + jnp.where(n < max_exact, n, large)
    u = jnp.zeros((8, K_TILE), F32)
    for b in range(N_BUCKETS):
        u = jnp.where(bucket == b, rel_ref[b, g * N_HEADS + head], u)
    return jnp.where(jnp.abs(delta) <= N_SIDE, u * LOG2E, NEG_INF)


def _build_bias_tiles(rel_ref, bias_ref, head0):
    qi = lax.broadcasted_iota(jnp.int32, (Q_TILE, K_TILE), 0)
    kj = lax.broadcasted_iota(jnp.int32, (Q_TILE, K_TILE), 1)
    for g in range(N_GROUPS):
        for h in range(HEADS_PER_STEP):
            u = _bias_row(rel_ref, g, head0 + h)
            ub = jnp.broadcast_to(u[0:1, :], (Q_TILE, K_TILE))
            for kind in range(N_KINDS):
                shift = (N_SIDE * kind - N_SIDE) % K_TILE
                t = pltpu.roll(ub, shift, 1, stride=1, stride_axis=0)
                in_band = jnp.abs(kj - qi - N_SIDE * kind) <= N_SIDE
                bias_ref[g, h, kind] = jnp.where(in_band, t, NEG_INF)


def _token_rows(g, c, l0):
    r = DILATIONS[g]
    if r == 1:
        return pl.ds(GROUP_BASE[g] + l0, Q_TILE)
    pitch = PADDED_PITCH if r % 16 == 0 else r
    return pl.ds(GROUP_BASE[g] + l0 * pitch + c, Q_TILE, stride=pitch)


def _load_tokens(ref, pair, g, t):
    r = DILATIONS[g]
    if r % 16:
        return ref[pair, pl.ds(pl.multiple_of(GROUP_BASE[g] + t * Q_TILE, Q_TILE), Q_TILE), :]
    n = Q_TILE // r
    base = pl.multiple_of(GROUP_BASE[g] + t * n * PADDED_PITCH, 8)
    return jnp.concatenate([ref[pair, pl.ds(base + i * PADDED_PITCH, r), :] for i in range(n)],
                           axis=0)


def _split_heads(q):
    first = lax.broadcasted_iota(jnp.int32, (1, LANES), 1) < HEAD_DIM
    zero = jnp.zeros_like(q)
    return jnp.concatenate([jnp.where(first, q, zero), jnp.where(first, zero, q)], axis=0)


def _softmax_pv(s, v):
    first = lax.broadcasted_iota(jnp.int32, (1, LANES), 1) < HEAD_DIM
    m = jnp.max(s, axis=-1, keepdims=True)
    p = jnp.exp2(s - m).astype(BF16)
    v1 = jnp.concatenate([v, jnp.ones((v.shape[0], LANES), BF16)], axis=1)
    o2 = jnp.dot(p, v1, preferred_element_type=F32)
    acc = jnp.where(first, o2[:Q_TILE, :LANES], o2[Q_TILE:, :LANES])
    l = jnp.where(first, o2[:Q_TILE, LANES:], o2[Q_TILE:, LANES:])
    m_t = jnp.where(first, m[:Q_TILE], m[Q_TILE:])
    return acc, m_t, l


_NT = (((1,), (1,)), ((), ()))


def _attn_kernel(rel_ref, q0_ref, k0_ref, v0_ref, q1_ref, k1_ref, v1_ref, q2_ref, k2_ref, v2_ref,
                 o_ref, bias_ref, acc_ref, m_ref, l_ref):
    hp = pl.program_id(0)

    @pl.when(pl.program_id(1) == 0)
    def _():
        _build_bias_tiles(rel_ref, bias_ref, hp * HEADS_PER_STEP)

    qkv = ((q0_ref, k0_ref, v0_ref), (q1_ref, k1_ref, v1_ref), (q2_ref, k2_ref, v2_ref))
    for pair in range(PAIRS_PER_STEP):
        lanes = slice(pair * LANES, (pair + 1) * LANES)
        for g in range(N_GROUPS):
            r = DILATIONS[g]
            tiles_per_seq = SEQ // r // Q_TILE
            q_ref, k_ref, v_ref = qkv[g]

            def put(t, res, g=g, tiles_per_seq=tiles_per_seq, pair=pair):
                rows = _token_rows(g, t // tiles_per_seq, (t % tiles_per_seq) * Q_TILE)
                for ref, val in zip((acc_ref, m_ref, l_ref), res):
                    ref[pair, rows, :] = val

            if tiles_per_seq > 1:
                for t in range(N_Q_TILES):
                    tl = t % tiles_per_seq
                    kind = 0 if tl == 0 else (2 if tl == tiles_per_seq - 1 else 1)
                    q0 = t * Q_TILE
                    k0 = q0 - N_SIDE * kind
                    q2 = _split_heads(q_ref[0, q0:q0 + Q_TILE, lanes])
                    s = lax.dot_general(q2, k_ref[0, k0:k0 + K_TILE, lanes], _NT,
                                        preferred_element_type=F32)
                    s = s + jnp.concatenate([bias_ref[g, 2 * pair, kind],
                                             bias_ref[g, 2 * pair + 1, kind]], axis=0)
                    put(t, _softmax_pv(s, v_ref[0, k0:k0 + K_TILE, lanes]))
            else:
                bias = jnp.concatenate([bias_ref[g, 2 * pair, 0, :, :Q_TILE],
                                        bias_ref[g, 2 * pair + 1, 0, :, :Q_TILE]], axis=0)
                for t in range(0, N_Q_TILES, 2):
                    q0 = t * Q_TILE
                    q = q_ref[0, q0:q0 + 2 * Q_TILE, lanes]
                    q2 = jnp.concatenate([_split_heads(q[:Q_TILE]), _split_heads(q[Q_TILE:])], axis=0)
                    s = lax.dot_general(q2, k_ref[0, q0:q0 + 2 * Q_TILE, lanes], _NT,
                                        preferred_element_type=F32)
                    for i in range(2):
                        si = s[2 * i * Q_TILE:2 * (i + 1) * Q_TILE, i * Q_TILE:(i + 1) * Q_TILE]
                        v = v_ref[0, q0 + i * Q_TILE:q0 + (i + 1) * Q_TILE, lanes]
                        put(t + i, _softmax_pv(si + bias, v))

    for pair in range(PAIRS_PER_STEP):
        lanes = slice(pair * LANES, (pair + 1) * LANES)

        def merge_body(t, carry, lanes=lanes, pair=pair):
            m = [_load_tokens(m_ref, pair, g, t) for g in range(N_GROUPS)]
            top = jnp.maximum(jnp.maximum(m[0], m[1]), m[2])
            w = [jnp.exp2(x - top) for x in m]
            num = sum(w[g] * _load_tokens(acc_ref, pair, g, t) for g in range(N_GROUPS))
            den = sum(w[g] * _load_tokens(l_ref, pair, g, t) for g in range(N_GROUPS))
            rows = pl.ds(pl.multiple_of(t * Q_TILE, Q_TILE), Q_TILE)
            o_ref[0, rows, lanes] = (num / den).astype(BF16)
            return carry

        lax.fori_loop(0, N_Q_TILES, merge_body, 0, unroll=4)


def _attention(qkv, rel_bias):
    n_hp = N_HEADS // HEADS_PER_STEP
    blk = (1, SEQ, STEP_LANES)
    grp_blk = (None,) + blk
    specs = [pl.BlockSpec(memory_space=pltpu.SMEM)]
    args = [rel_bias]
    for g in range(N_GROUPS):
        for part in range(3):
            specs.append(pl.BlockSpec(grp_blk,
                                      lambda hp, b, g=g, part=part: (g, b, 0, part * n_hp + hp)))
            args.append(qkv)
    return pl.pallas_call(
        _attn_kernel,
        grid=(n_hp, BATCH),
        in_specs=specs,
        out_specs=pl.BlockSpec(blk, lambda hp, b: (b, 0, hp)),
        out_shape=jax.ShapeDtypeStruct((BATCH, SEQ, D_MODEL), BF16),
        scratch_shapes=[pltpu.VMEM((N_GROUPS, HEADS_PER_STEP, N_KINDS, Q_TILE, K_TILE), F32),
                        pltpu.VMEM((PAIRS_PER_STEP, sum(GROUP_ROWS), LANES), F32),
                        pltpu.VMEM((PAIRS_PER_STEP, sum(GROUP_ROWS), LANES), F32),
                        pltpu.VMEM((PAIRS_PER_STEP, sum(GROUP_ROWS), LANES), F32)],
        compiler_params=_params(("arbitrary", "arbitrary")),
        name="dilated_attention",
    )(*args)


MLP_TM = 512
MLP_FC = 1024
MLP_STAGES = 2 + 2 * (D_FF // MLP_FC)


_DONE = object()


def _interleave(order, **stage_generators):
    for key in order:
        next(stage_generators[key], _DONE)
    for gen in stage_generators.values():
        assert next(gen, _DONE) is _DONE, "order does not cover every stage"


def _mlp_stages(read_x, gpre_ref, gpost_ref, wup_ref, wdn_ref, o_ref):
    h = _rms(read_x(), gpre_ref[...]).astype(BF16)
    acc = jnp.zeros((MLP_TM, D_MODEL), F32)
    yield
    for c in range(D_FF // MLP_FC):
        cols = slice(c * MLP_FC, (c + 1) * MLP_FC)
        u = jnp.dot(h, wup_ref[:, cols].astype(BF16), preferred_element_type=F32)
        u = jnp.square(jnp.maximum(u, 0.0)).astype(BF16)
        yield
        acc = acc + jnp.dot(u, wdn_ref[cols, :].astype(BF16), preferred_element_type=F32)
        yield
    o_ref[...] = read_x() + _rms(acc, gpost_ref[...])


def _proj_mlp_kernel(a_ref, wo_ref, gmix_ref, x_ref, gpre_ref, gpost_ref, wup_ref, wdn_ref,
                     *refs, n_cast):
    f32_refs, o_ref, bf16_refs = refs[:n_cast], refs[n_cast], refs[n_cast + 1:]
    for src_ref, dst_ref in zip(f32_refs, bf16_refs):
        dst_ref[...] = src_ref[...].astype(BF16)
    m = jnp.dot(a_ref[...], wo_ref[...].astype(BF16), preferred_element_type=F32)
    x = x_ref[...] + _rms(m, gmix_ref[...])
    _interleave("m" * MLP_STAGES,
                m=_mlp_stages(lambda: x, gpre_ref, gpost_ref, wup_ref, wdn_ref, o_ref))


def _proj_mlp(x2d, a2d, w_o, g_mix, g_pre, g_post, w_up, w_down, to_bf16):
    n = x2d.shape[0]
    steps = n // MLP_TM
    resident = pl.Buffered(1)
    tile = pl.BlockSpec((MLP_TM, D_MODEL), lambda i: (i, 0))
    gain = pl.BlockSpec((1, D_MODEL), lambda i: (0, 0))
    cast_in, cast_out, cast_shape = [], [], []
    for w, layer in to_bf16:
        _, rows, cols = w.shape
        cast_in.append(pl.BlockSpec((None, rows // steps, cols), lambda i, layer=layer: (layer, i, 0)))
        cast_out.append(pl.BlockSpec((rows // steps, cols), lambda i: (i, 0)))
        cast_shape.append(jax.ShapeDtypeStruct((rows, cols), BF16))
    out = pl.pallas_call(
        functools.partial(_proj_mlp_kernel, n_cast=len(to_bf16)),
        grid=(steps,),
        in_specs=[tile,
                  pl.BlockSpec((None, D_MODEL, D_MODEL), lambda i: (0, 0, 0), pipeline_mode=resident),
                  gain, tile, gain, gain,
                  pl.BlockSpec((None, D_MODEL, D_FF), lambda i: (0, 0, 0), pipeline_mode=resident),
                  pl.BlockSpec((None, D_FF, D_MODEL), lambda i: (0, 0, 0), pipeline_mode=resident)]
        + cast_in,
        out_specs=[tile] + cast_out,
        out_shape=[jax.ShapeDtypeStruct((n, D_MODEL), F32)] + cast_shape,
        compiler_params=_params(("arbitrary",)),
        name="attn_proj_mlp",
    )(a2d, w_o, g_mix.reshape(1, D_MODEL), x2d, g_pre.reshape(1, D_MODEL),
      g_post.reshape(1, D_MODEL), w_up, w_down, *[w for w, _ in to_bf16])
    return out[0], out[1:]


CONV_TS = 512
HALO = 16
CONV_ROWS = CONV_TS + 2 * HALO
CONV_RB = 128
CONV_NC = 256


N_TILES = BATCH * SEQ // CONV_TS
TILES_PER_SEQ = SEQ // CONV_TS
assert CONV_TS == MLP_TM


def _conv_stages(xm_ref, xt_ref, xb_ref, first, last, gpre_ref, w1_ref, b1_ref, wdw_ref, bdw_ref,
                 lng_ref, lnb_ref, w2_ref, b2_ref, gpost_ref, o_ref, h_ref, u_ref, c_ref):
    gpre = gpre_ref[...]
    h_ref[0:HALO, :] = _rms(xt_ref[0], gpre).astype(BF16)
    h_ref[HALO:HALO + CONV_TS, :] = _rms(xm_ref[0], gpre).astype(BF16)
    h_ref[HALO + CONV_TS:, :] = _rms(xb_ref[0], gpre).astype(BF16)
    yield

    h = h_ref[...]
    half = CONV_RB // 2
    for nc in range(D_MODEL // CONV_NC):
        cols = slice(nc * CONV_NC, (nc + 1) * CONV_NC)
        gcols = slice(D_MODEL + nc * CONV_NC, D_MODEL + (nc + 1) * CONV_NC)
        a = jnp.dot(h, w1_ref[:, cols], preferred_element_type=F32) + b1_ref[:, cols]
        gate = jnp.dot(h, w1_ref[:, gcols], preferred_element_type=F32) + b1_ref[:, gcols]
        u = a * jax.nn.sigmoid(gate)
        for k in range(CONV_NC // LANES):
            lc = nc * CONV_NC // LANES + k
            lanes = slice(lc * LANES, (lc + 1) * LANES)
            uk = u[:, k * LANES:(k + 1) * LANES]
            u_ref[lc, 0:HALO, :] = jnp.where(first, 0.0, uk[0:HALO])
            u_ref[lc, HALO:HALO + CONV_TS, :] = uk[HALO:HALO + CONV_TS]
            u_ref[lc, HALO + CONV_TS:, :] = jnp.where(last, 0.0, uk[HALO + CONV_TS:])
        yield
        for k in range(CONV_NC // LANES):
            lc = nc * CONV_NC // LANES + k
            lanes = slice(lc * LANES, (lc + 1) * LANES)
            for r0 in range(0, CONV_TS, CONV_RB):
                for phase in range(2):
                    acc = bdw_ref[:, lanes]
                    for t in range(CONV_WIDTH):
                        start = r0 + phase + t + HALO - CONV_WIDTH // 2
                        acc = acc + (u_ref[lc, pl.ds(start, half, stride=2), :]
                                     * wdw_ref[t:t + 1, lanes])
                    c_ref[lc, pl.ds(r0 + phase, half, stride=2), :] = acc
            yield

    v = jnp.concatenate([c_ref[lc] for lc in range(D_MODEL // LANES)], axis=1)
    mu = jnp.mean(v, axis=-1, keepdims=True)
    var = jnp.mean(jnp.square(v - mu), axis=-1, keepdims=True)
    y = (v - mu) * lax.rsqrt(var + LN_EPS) * lng_ref[...] + lnb_ref[...]
    y = (y * jax.nn.sigmoid(y)).astype(BF16)
    yield
    z = jnp.dot(y, w2_ref[...], preferred_element_type=F32) + b2_ref[...]
    o_ref[...] = xm_ref[0] + _rms(z, gpost_ref[...])


def _conv_mlp_kernel(xm_ref, xt_ref, xb_ref, gpre_ref, w1_ref, b1_ref, wdw_ref, bdw_ref, lng_ref,
                     lnb_ref, w2_ref, b2_ref, gpost_ref, g2pre_ref, g2post_ref, wup_ref, wdn_ref,
                     o_ref, h_ref, u_ref, c_ref, x_ref, xnew_ref):
    s = pl.program_id(0)

    @pl.when(s == 0)
    def _():
        x_ref[...] = jnp.zeros((CONV_TS, D_MODEL), F32)

    j = jnp.minimum(s, N_TILES - 1) % TILES_PER_SEQ
    order = "ccccc" + "mm" + "cmccm" * 2 + "cmcm" * 2
    _interleave(
        order,
        c=_conv_stages(xm_ref, xt_ref, xb_ref, j == 0, j == TILES_PER_SEQ - 1, gpre_ref, w1_ref,
                       b1_ref, wdw_ref, bdw_ref, lng_ref, lnb_ref, w2_ref, b2_ref, gpost_ref,
                       xnew_ref, h_ref, u_ref, c_ref),
        m=_mlp_stages(lambda: x_ref[...], g2pre_ref, g2post_ref, wup_ref, wdn_ref, o_ref))
    x_ref[...] = xnew_ref[...]


def _conv_mlp_layer(x, g_pre, w1, b1, wdw, bdw, lng, lnb, w2, b2, g_post, g2_pre, g2_post,
                    w_up, w_down):
    row = lambda v: v.reshape(1, -1)
    resident = pl.Buffered(1)
    const = lambda shape, **kw: pl.BlockSpec(shape, lambda s: (0,) * len(shape), **kw)
    halo_blocks = CONV_TS // HALO

    def tile(s):
        t = jnp.minimum(s, N_TILES - 1)
        return t // TILES_PER_SEQ, t % TILES_PER_SEQ

    def main_map(s):
        b, j = tile(s)
        return b, j, 0

    def top_map(s):
        b, j = tile(s)
        return b, jnp.maximum(j * halo_blocks - 1, 0), 0

    def bot_map(s):
        b, j = tile(s)
        return b, jnp.minimum((j + 1) * halo_blocks, SEQ // HALO - 1), 0

    gain = const((1, D_MODEL))
    return pl.pallas_call(
        _conv_mlp_kernel,
        grid=(N_TILES + 1,),
        in_specs=[pl.BlockSpec((1, CONV_TS, D_MODEL), main_map),
                  pl.BlockSpec((1, HALO, D_MODEL), top_map),
                  pl.BlockSpec((1, HALO, D_MODEL), bot_map),
                  gain,
                  const((D_MODEL, 2 * D_MODEL), pipeline_mode=resident),
                  const((1, 2 * D_MODEL)),
                  const((CONV_WIDTH, D_MODEL)),
                  gain, gain, gain,
                  const((D_MODEL, D_MODEL), pipeline_mode=resident),
                  gain, gain, gain, gain,
                  const((D_MODEL, D_FF), pipeline_mode=resident),
                  const((D_FF, D_MODEL), pipeline_mode=resident)],
        out_specs=pl.BlockSpec((CONV_TS, D_MODEL), lambda s: (jnp.maximum(s - 1, 0), 0)),
        out_shape=jax.ShapeDtypeStruct((BATCH * SEQ, D_MODEL), F32),
        scratch_shapes=[pltpu.VMEM((CONV_ROWS, D_MODEL), BF16),
                        pltpu.VMEM((D_MODEL // LANES, CONV_ROWS, LANES), F32),
                        pltpu.VMEM((D_MODEL // LANES, CONV_TS, LANES), F32),
                        pltpu.VMEM((CONV_TS, D_MODEL), F32),
                        pltpu.VMEM((CONV_TS, D_MODEL), F32)],
        compiler_params=_params(("arbitrary",)),
        name="conformer_conv_mlp",
    )(x, x, x, row(g_pre), w1, row(b1), wdw, row(bdw), row(lng), row(lnb), w2, row(b2),
      row(g_post), row(g2_pre), row(g2_post), w_up, w_down)


def kernel(x, rel_bias, norm_mix_pre, norm_mix_post, norm_mlp_pre, norm_mlp_post, attn_w_qkv,
           attn_w_o, conv_w_pw1, conv_b_pw1, conv_w_dw, conv_b_dw, conv_ln_g, conv_ln_b,
           conv_w_pw2, conv_b_pw2, mlp_w_up, mlp_w_down):
    n_tok = BATCH * SEQ
    x2d = x.reshape(n_tok, D_MODEL)

    h = _prenorm(x, norm_mix_pre[0])
    w_qkv = attn_w_qkv.reshape(D_MODEL, N_GROUPS * GROUP_WIDTH)
    qkv = _qkv_proj(h, w_qkv)
    a = _attention(qkv, rel_bias)
    x2d, (w_pw1, w_pw2, w_up1, w_down1) = _proj_mlp(
        x2d, a.reshape(n_tok, D_MODEL), attn_w_o, norm_mix_post[0], norm_mlp_pre[0],
        norm_mlp_post[0], mlp_w_up, mlp_w_down,
        to_bf16=[(conv_w_pw1, 0), (conv_w_pw2, 0), (mlp_w_up, 1), (mlp_w_down, 1)])

    x2d = _conv_mlp_layer(x2d.reshape(BATCH, SEQ, D_MODEL), norm_mix_pre[1], w_pw1, conv_b_pw1[0],
                          conv_w_dw[0], conv_b_dw[0], conv_ln_g[0], conv_ln_b[0], w_pw2,
                          conv_b_pw2[0], norm_mix_post[1], norm_mlp_pre[1], norm_mlp_post[1],
                          w_up1, w_down1)
    return x2d.reshape(BATCH, SEQ, D_MODEL)
```

```python
import functools
import math

import jax
import jax.numpy as jnp
from jax import lax
from jax.experimental import pallas as pl
from jax.experimental.pallas import tpu as pltpu

D_MODEL = 1024
BATCH = 8
SEQ = 2048
HEAD_DIM = 64
N_HEADS = 16
DILATIONS = (1, 4, 16)
N_SIDE = 64
N_GROUPS = 3
GROUP_WIDTH = 3 * N_HEADS * HEAD_DIM
N_BUCKETS = 32
MAX_DISTANCE = 1024
CONV_WIDTH = 31
D_FF = 4 * D_MODEL
RMS_EPS = 1e-6
LN_EPS = 1e-5
NEG_INF = -1e30

F32 = jnp.float32
BF16 = jnp.bfloat16

LANES = 128
Q_TILE = 128
K_TILE = Q_TILE + 2 * N_SIDE
VMEM_LIMIT = 56 * 1024 * 1024


def _rms(x, g):
    return x * lax.rsqrt(jnp.mean(x * x, axis=-1, keepdims=True) + RMS_EPS) * g


def _params(semantics):
    return pltpu.CompilerParams(dimension_semantics=semantics, vmem_limit_bytes=VMEM_LIMIT)


NORM_ROWS = 256


def _prenorm_kernel(x_ref, g_ref, *refs):
    out_ref, slab_ref, regroup_ref = refs
    assert DILATIONS == (1, 4, 16)

    def chunk(i, carry):
        rows = pl.ds(pl.multiple_of(i * NORM_ROWS, NORM_ROWS), NORM_ROWS)
        hn = _rms(x_ref[0, rows, :], g_ref[...])
        for k in range(D_MODEL // LANES):
            slab_ref[k, rows, :] = hn[:, k * LANES:(k + 1) * LANES]
        out_ref[0, 0, rows, :] = hn.astype(BF16)
        return carry

    lax.fori_loop(0, SEQ // NORM_ROWS, chunk, 0)
    r1 = DILATIONS[1]
    L1, L2 = SEQ // r1, SEQ // DILATIONS[2]
    for k in range(D_MODEL // LANES):
        lanes = slice(k * LANES, (k + 1) * LANES)
        for c in range(r1):
            v = slab_ref[k, pl.ds(c, L1, stride=r1), :]
            out_ref[1, 0, c * L1:(c + 1) * L1, lanes] = v.astype(BF16)
            regroup_ref[c * L1:(c + 1) * L1, :] = v
        for c in range(DILATIONS[2]):
            start = (c % r1) * L1 + c // r1
            out_ref[2, 0, c * L2:(c + 1) * L2, lanes] = (
                regroup_ref[pl.ds(start, L2, stride=r1), :].astype(BF16))


def _prenorm(x, g):
    return pl.pallas_call(
        _prenorm_kernel,
        grid=(BATCH,),
        in_specs=[pl.BlockSpec((1, SEQ, D_MODEL), lambda b: (b, 0, 0)),
                  pl.BlockSpec((1, D_MODEL), lambda b: (0, 0))],
        out_specs=pl.BlockSpec((N_GROUPS, 1, SEQ, D_MODEL), lambda b: (0, b, 0, 0)),
        out_shape=jax.ShapeDtypeStruct((N_GROUPS, BATCH, SEQ, D_MODEL), BF16),
        scratch_shapes=[pltpu.VMEM((D_MODEL // LANES, SEQ, LANES), F32),
                        pltpu.VMEM((SEQ, LANES), F32)],
        compiler_params=_params(("arbitrary",)),
        name="prenorm",
    )(x, g.reshape(1, D_MODEL))


QKV_TN = 1536
QKV_TM = 512
Q_WIDTH = N_HEADS * HEAD_DIM
assert QKV_TN >= Q_WIDTH
LOG2E = math.log2(math.e)
Q_SCALE = HEAD_DIM ** -0.5 * LOG2E


def _qkv_kernel(h_ref, w_ref, o_ref, wb_ref):
    @pl.when(pl.program_id(1) == 0)
    def _():
        wb_ref[...] = w_ref[...].astype(BF16)

    col = lax.broadcasted_iota(jnp.int32, (1, QKV_TN), 1)
    first_tile = pl.program_id(0) % (GROUP_WIDTH // QKV_TN) == 0
    scale = jnp.where(first_tile & (col < Q_WIDTH), Q_SCALE, 1.0)
    for m in range(SEQ // QKV_TM):
        rows = slice(m * QKV_TM, (m + 1) * QKV_TM)
        res = jnp.dot(h_ref[0, rows, :], wb_ref[...], preferred_element_type=F32)
        o_ref[0, rows, :] = (res * scale).astype(BF16)


def _qkv_proj(h, w_qkv):
    nj = GROUP_WIDTH // QKV_TN
    return pl.pallas_call(
        _qkv_kernel,
        grid=(N_GROUPS * nj, BATCH),
        in_specs=[pl.BlockSpec((None, 1, SEQ, D_MODEL), lambda i, b: (i // nj, b, 0, 0)),
                  pl.BlockSpec((D_MODEL, QKV_TN), lambda i, b: (0, i))],
        out_specs=pl.BlockSpec((None, 1, SEQ, QKV_TN), lambda i, b: (i // nj, b, 0, i % nj)),
        out_shape=jax.ShapeDtypeStruct((N_GROUPS, BATCH, SEQ, GROUP_WIDTH), BF16),
        scratch_shapes=[pltpu.VMEM((D_MODEL, QKV_TN), BF16)],
        compiler_params=_params(("arbitrary", "arbitrary")),
        name="qkv_proj",
    )(h, w_qkv)


HEADS_PER_STEP = 4
PAIRS_PER_STEP = HEADS_PER_STEP // 2
STEP_LANES = HEADS_PER_STEP * HEAD_DIM
N_Q_TILES = SEQ // Q_TILE
N_KINDS = 3
PADDED_PITCH = 24
GROUP_ROWS = tuple(SEQ // 16 * PADDED_PITCH if r % 16 == 0 else SEQ for r in DILATIONS)
GROUP_BASE = tuple(sum(GROUP_ROWS[:g]) for g in range(N_GROUPS))


def _bias_row(rel_ref, g, head):
    r = DILATIONS[g]
    nb = N_BUCKETS // 2
    max_exact = nb // 2
    delta = lax.broadcasted_iota(jnp.int32, (8, K_TILE), 1) - N_SIDE
    rel = delta * r
    n = jnp.abs(rel)
    nf = jnp.maximum(n, 1).astype(F32)
    large = max_exact + (jnp.log(nf / max_exact) / math.log(MAX_DISTANCE / max_exact)
                         * (nb - max_exact)).astype(jnp.int32)
    large = jnp.minimum(large, nb - 1)
    bucket = jnp.where(rel > 0, nb, 0) + jnp.where(n < max_exact, n, large)
    u = jnp.zeros((8, K_TILE), F32)
    for b in range(N_BUCKETS):
        u = jnp.where(bucket == b, rel_ref[b, g * N_HEADS + head], u)
    return jnp.where(jnp.abs(delta) <= N_SIDE, u * LOG2E, NEG_INF)


def _build_bias_tiles(rel_ref, bias_ref, head0):
    qi = lax.broadcasted_iota(jnp.int32, (Q_TILE, K_TILE), 0)
    kj = lax.broadcasted_iota(jnp.int32, (Q_TILE, K_TILE), 1)
    for g in range(N_GROUPS):
        for h in range(HEADS_PER_STEP):
            u = _bias_row(rel_ref, g, head0 + h)
            ub = jnp.broadcast_to(u[0:1, :], (Q_TILE, K_TILE))
            for kind in range(N_KINDS):
                shift = (N_SIDE * kind - N_SIDE) % K_TILE
                t = pltpu.roll(ub, shift, 1, stride=1, stride_axis=0)
                in_band = jnp.abs(kj - qi - N_SIDE * kind) <= N_SIDE
                bias_ref[g, h, kind] = jnp.where(in_band, t, NEG_INF)


def _token_rows(g, c, l0):
    r = DILATIONS[g]
    if r == 1:
        return pl.ds(GROUP_BASE[g] + l0, Q_TILE)
    pitch = PADDED_PITCH if r % 16 == 0 else r
    return pl.ds(GROUP_BASE[g] + l0 * pitch + c, Q_TILE, stride=pitch)


def _load_tokens(ref, pair, g, t):
    r = DILATIONS[g]
    if r % 16:
        return ref[pair, pl.ds(pl.multiple_of(GROUP_BASE[g] + t * Q_TILE, Q_TILE), Q_TILE), :]
    n = Q_TILE // r
    base = pl.multiple_of(GROUP_BASE[g] + t * n * PADDED_PITCH, 8)
    return jnp.concatenate([ref[pair, pl.ds(base + i * PADDED_PITCH, r), :] for i in range(n)],
                           axis=0)


def _split_heads(q):
    first = lax.broadcasted_iota(jnp.int32, (1, LANES), 1) < HEAD_DIM
    zero = jnp.zeros_like(q)
    return jnp.concatenate([jnp.where(first, q, zero), jnp.where(first, zero, q)], axis=0)


def _softmax_pv(s, v):
    first = lax.broadcasted_iota(jnp.int32, (1, LANES), 1) < HEAD_DIM
    m = jnp.max(s, axis=-1, keepdims=True)
    p = jnp.exp2(s - m).astype(BF16)
    v1 = jnp.concatenate([v, jnp.ones((v.shape[0], LANES), BF16)], axis=1)
    o2 = jnp.dot(p, v1, preferred_element_type=F32)
    acc = jnp.where(first, o2[:Q_TILE, :LANES], o2[Q_TILE:, :LANES])
    l = jnp.where(first, o2[:Q_TILE, LANES:], o2[Q_TILE:, LANES:])
    m_t = jnp.where(first, m[:Q_TILE], m[Q_TILE:])
    return acc, m_t, l


_NT = (((1,), (1,)), ((), ()))


def _attn_kernel(rel_ref, q0_ref, k0_ref, v0_ref, q1_ref, k1_ref, v1_ref, q2_ref, k2_ref, v2_ref,
                 o_ref, bias_ref, acc_ref, m_ref, l_ref):
    hp = pl.program_id(0)

    @pl.when(pl.program_id(1) == 0)
    def _():
        _build_bias_tiles(rel_ref, bias_ref, hp * HEADS_PER_STEP)

    qkv = ((q0_ref, k0_ref, v0_ref), (q1_ref, k1_ref, v1_ref), (q2_ref, k2_ref, v2_ref))
    for pair in range(PAIRS_PER_STEP):
        lanes = slice(pair * LANES, (pair + 1) * LANES)
        for g in range(N_GROUPS):
            r = DILATIONS[g]
            tiles_per_seq = SEQ // r // Q_TILE
            q_ref, k_ref, v_ref = qkv[g]

            def put(t, res, g=g, tiles_per_seq=tiles_per_seq, pair=pair):
                rows = _token_rows(g, t // tiles_per_seq, (t % tiles_per_seq) * Q_TILE)
                for ref, val in zip((acc_ref, m_ref, l_ref), res):
                    ref[pair, rows, :] = val

            if tiles_per_seq > 1:
                for t in range(N_Q_TILES):
                    tl = t % tiles_per_seq
                    kind = 0 if tl == 0 else (2 if tl == tiles_per_seq - 1 else 1)
                    q0 = t * Q_TILE
                    k0 = q0 - N_SIDE * kind
                    q2 = _split_heads(q_ref[0, q0:q0 + Q_TILE, lanes])
                    s = lax.dot_general(q2, k_ref[0, k0:k0 + K_TILE, lanes], _NT,
                                        preferred_element_type=F32)
                    s = s + jnp.concatenate([bias_ref[g, 2 * pair, kind],
                                             bias_ref[g, 2 * pair + 1, kind]], axis=0)
                    put(t, _softmax_pv(s, v_ref[0, k0:k0 + K_TILE, lanes]))
            else:
                bias = jnp.concatenate([bias_ref[g, 2 * pair, 0, :, :Q_TILE],
                                        bias_ref[g, 2 * pair + 1, 0, :, :Q_TILE]], axis=0)
                for t in range(0, N_Q_TILES, 2):
                    q0 = t * Q_TILE
                    q = q_ref[0, q0:q0 + 2 * Q_TILE, lanes]
                    q2 = jnp.concatenate([_split_heads(q[:Q_TILE]), _split_heads(q[Q_TILE:])], axis=0)
                    s = lax.dot_general(q2, k_ref[0, q0:q0 + 2 * Q_TILE, lanes], _NT,
                                        preferred_element_type=F32)
                    for i in range(2):
                        si = s[2 * i * Q_TILE:2 * (i + 1) * Q_TILE, i * Q_TILE:(i + 1) * Q_TILE]
                        v = v_ref[0, q0 + i * Q_TILE:q0 + (i + 1) * Q_TILE, lanes]
                        put(t + i, _softmax_pv(si + bias, v))

    for pair in range(PAIRS_PER_STEP):
        lanes = slice(pair * LANES, (pair + 1) * LANES)

        def merge_body(t, carry, lanes=lanes, pair=pair):
            m = [_load_tokens(m_ref, pair, g, t) for g in range(N_GROUPS)]
            top = jnp.maximum(jnp.maximum(m[0], m[1]), m[2])
            w = [jnp.exp2(x - top) for x in m]
            num = sum(w[g] * _load_tokens(acc_ref, pair, g, t) for g in range(N_GROUPS))
            den = sum(w[g] * _load_tokens(l_ref, pair, g, t) for g in range(N_GROUPS))
            rows = pl.ds(pl.multiple_of(t * Q_TILE, Q_TILE), Q_TILE)
            o_ref[0, rows, lanes] = (num / den).astype(BF16)
            return carry

        lax.fori_loop(0, N_Q_TILES, merge_body, 0, unroll=4)


def _attention(qkv, rel_bias):
    n_hp = N_HEADS // HEADS_PER_STEP
    blk = (1, SEQ, STEP_LANES)
    grp_blk = (None,) + blk
    specs = [pl.BlockSpec(memory_space=pltpu.SMEM)]
    args = [rel_bias]
    for g in range(N_GROUPS):
        for part in range(3):
            specs.append(pl.BlockSpec(grp_blk,
                                      lambda hp, b, g=g, part=part: (g, b, 0, part * n_hp + hp)))
            args.append(qkv)
    return pl.pallas_call(
        _attn_kernel,
        grid=(n_hp, BATCH),
        in_specs=specs,
        out_specs=pl.BlockSpec(blk, lambda hp, b: (b, 0, hp)),
        out_shape=jax.ShapeDtypeStruct((BATCH, SEQ, D_MODEL), BF16),
        scratch_shapes=[pltpu.VMEM((N_GROUPS, HEADS_PER_STEP, N_KINDS, Q_TILE, K_TILE), F32),
                        pltpu.VMEM((PAIRS_PER_STEP, sum(GROUP_ROWS), LANES), F32),
                        pltpu.VMEM((PAIRS_PER_STEP, sum(GROUP_ROWS), LANES), F32),
                        pltpu.VMEM((PAIRS_PER_STEP, sum(GROUP_ROWS), LANES), F32)],
        compiler_params=_params(("arbitrary", "arbitrary")),
        name="dilated_attention",
    )(*args)


MLP_TM = 512
MLP_FC = 1024
MLP_STAGES = 2 + 2 * (D_FF // MLP_FC)


_DONE = object()


def _interleave(order, **stage_generators):
    for key in order:
        next(stage_generators[key], _DONE)
    for gen in stage_generators.values():
        assert next(gen, _DONE) is _DONE, "order does not cover every stage"


def _mlp_stages(read_x, gpre_ref, gpost_ref, wup_ref, wdn_ref, o_ref):
    h = _rms(read_x(), gpre_ref[...]).astype(BF16)
    acc = jnp.zeros((MLP_TM, D_MODEL), F32)
    yield
    for c in range(D_FF // MLP_FC):
        cols = slice(c * MLP_FC, (c + 1) * MLP_FC)
        u = jnp.dot(h, wup_ref[:, cols].astype(BF16), preferred_element_type=F32)
        u = jnp.square(jnp.maximum(u, 0.0)).astype(BF16)
        yield
        acc = acc + jnp.dot(u, wdn_ref[cols, :].astype(BF16), preferred_element_type=F32)
        yield
    o_ref[...] = read_x() + _rms(acc, gpost_ref[...])


def _proj_mlp_kernel(a_ref, wo_ref, gmix_ref, x_ref, gpre_ref, gpost_ref, wup_ref, wdn_ref,
                     *refs, n_cast):
    f32_refs, o_ref, bf16_refs = refs[:n_cast], refs[n_cast], refs[n_cast + 1:]
    for src_ref, dst_ref in zip(f32_refs, bf16_refs):
        dst_ref[...] = src_ref[...].astype(BF16)
    m = jnp.dot(a_ref[...], wo_ref[...].astype(BF16), preferred_element_type=F32)
    x = x_ref[...] + _rms(m, gmix_ref[...])
    _interleave("m" * MLP_STAGES,
                m=_mlp_stages(lambda: x, gpre_ref, gpost_ref, wup_ref, wdn_ref, o_ref))


def _proj_mlp(x2d, a2d, w_o, g_mix, g_pre, g_post, w_up, w_down, to_bf16):
    n = x2d.shape[0]
    steps = n // MLP_TM
    resident = pl.Buffered(1)
    tile = pl.BlockSpec((MLP_TM, D_MODEL), lambda i: (i, 0))
    gain = pl.BlockSpec((1, D_MODEL), lambda i: (0, 0))
    cast_in, cast_out, cast_shape = [], [], []
    for w, layer in to_bf16:
        _, rows, cols = w.shape
        cast_in.append(pl.BlockSpec((None, rows // steps, cols), lambda i, layer=layer: (layer, i, 0)))
        cast_out.append(pl.BlockSpec((rows // steps, cols), lambda i: (i, 0)))
        cast_shape.append(jax.ShapeDtypeStruct((rows, cols), BF16))
    out = pl.pallas_call(
        functools.partial(_proj_mlp_kernel, n_cast=len(to_bf16)),
        grid=(steps,),
        in_specs=[tile,
                  pl.BlockSpec((None, D_MODEL, D_MODEL), lambda i: (0, 0, 0), pipeline_mode=resident),
                  gain, tile, gain, gain,
                  pl.BlockSpec((None, D_MODEL, D_FF), lambda i: (0, 0, 0), pipeline_mode=resident),
                  pl.BlockSpec((None, D_FF, D_MODEL), lambda i: (0, 0, 0), pipeline_mode=resident)]
        + cast_in,
        out_specs=[tile] + cast_out,
        out_shape=[jax.ShapeDtypeStruct((n, D_MODEL), F32)] + cast_shape,
        compiler_params=_params(("arbitrary",)),
        name="attn_proj_mlp",
    )(a2d, w_o, g_mix.reshape(1, D_MODEL), x2d, g_pre.reshape(1, D_MODEL),
      g_post.reshape(1, D_MODEL), w_up, w_down, *[w for w, _ in to_bf16])
    return out[0], out[1:]


CONV_TS = 512
HALO = 16
CONV_ROWS = CONV_TS + 2 * HALO
CONV_RB = 128
CONV_NC = 256


N_TILES = BATCH * SEQ // CONV_TS
TILES_PER_SEQ = SEQ // CONV_TS
assert CONV_TS == MLP_TM


def _conv_stages(xm_ref, xt_ref, xb_ref, first, last, gpre_ref, w1_ref, b1_ref, wdw_ref, bdw_ref,
                 lng_ref, lnb_ref, w2_ref, b2_ref, gpost_ref, o_ref, h_ref, u_ref, c_ref):
    gpre = gpre_ref[...]
    h_ref[0:HALO, :] = _rms(xt_ref[0], gpre).astype(BF16)
    h_ref[HALO:HALO + CONV_TS, :] = _rms(xm_ref[0], gpre).astype(BF16)
    h_ref[HALO + CONV_TS:, :] = _rms(xb_ref[0], gpre).astype(BF16)
    yield

    h = h_ref[...]
    half = CONV_RB // 2
    for nc in range(D_MODEL // CONV_NC):
        cols = slice(nc * CONV_NC, (nc + 1) * CONV_NC)
        gcols = slice(D_MODEL + nc * CONV_NC, D_MODEL + (nc + 1) * CONV_NC)
        a = jnp.dot(h, w1_ref[:, cols], preferred_element_type=F32) + b1_ref[:, cols]
        gate = jnp.dot(h, w1_ref[:, gcols], preferred_element_type=F32) + b1_ref[:, gcols]
        u = a * jax.nn.sigmoid(gate)
        for k in range(CONV_NC // LANES):
            lc = nc * CONV_NC // LANES + k
            lanes = slice(lc * LANES, (lc + 1) * LANES)
            uk = u[:, k * LANES:(k + 1) * LANES]
            u_ref[lc, 0:HALO, :] = jnp.where(first, 0.0, uk[0:HALO])
            u_ref[lc, HALO:HALO + CONV_TS, :] = uk[HALO:HALO + CONV_TS]
            u_ref[lc, HALO + CONV_TS:, :] = jnp.where(last, 0.0, uk[HALO + CONV_TS:])
        yield
        for k in range(CONV_NC // LANES):
            lc = nc * CONV_NC // LANES + k
            lanes = slice(lc * LANES, (lc + 1) * LANES)
            for r0 in range(0, CONV_TS, CONV_RB):
                for phase in range(2):
                    acc = bdw_ref[:, lanes]
                    for t in range(CONV_WIDTH):
                        start = r0 + phase + t + HALO - CONV_WIDTH // 2
                        acc = acc + (u_ref[lc, pl.ds(start, half, stride=2), :]
                                     * wdw_ref[t:t + 1, lanes])
                    c_ref[lc, pl.ds(r0 + phase, half, stride=2), :] = acc
            yield

    v = jnp.concatenate([c_ref[lc] for lc in range(D_MODEL // LANES)], axis=1)
    mu = jnp.mean(v, axis=-1, keepdims=True)
    var = jnp.mean(jnp.square(v - mu), axis=-1, keepdims=True)
    y = (v - mu) * lax.rsqrt(var + LN_EPS) * lng_ref[...] + lnb_ref[...]
    y = (y * jax.nn.sigmoid(y)).astype(BF16)
    yield
    z = jnp.dot(y, w2_ref[...], preferred_element_type=F32) + b2_ref[...]
    o_ref[...] = xm_ref[0] + _rms(z, gpost_ref[...])


def _conv_mlp_kernel(xm_ref, xt_ref, xb_ref, gpre_ref, w1_ref, b1_ref, wdw_ref, bdw_ref, lng_ref,
                     lnb_ref, w2_ref, b2_ref, gpost_ref, g2pre_ref, g2post_ref, wup_ref, wdn_ref,
                     o_ref, h_ref, u_ref, c_ref, x_ref, xnew_ref):
    s = pl.program_id(0)
    j = s % TILES_PER_SEQ

    def conv():
        return _conv_stages(xm_ref, xt_ref, xb_ref, j == 0, j == TILES_PER_SEQ - 1, gpre_ref,
                            w1_ref, b1_ref, wdw_ref, bdw_ref, lng_ref, lnb_ref, w2_ref, b2_ref,
                            gpost_ref, xnew_ref, h_ref, u_ref, c_ref)

    def mlp():
        return _mlp_stages(lambda: x_ref[...], g2pre_ref, g2post_ref, wup_ref, wdn_ref, o_ref)

    order = "ccccc" + "mm" + "cmccm" * 2 + "cmcm" * 2

    @pl.when(s == 0)
    def _():
        _interleave("c" * order.count("c"), c=conv())
        x_ref[...] = xnew_ref[...]

    @pl.when((s > 0) & (s < N_TILES))
    def _():
        _interleave(order, c=conv(), m=mlp())
        x_ref[...] = xnew_ref[...]

    @pl.when(s == N_TILES)
    def _():
        _interleave("m" * order.count("m"), m=mlp())


def _conv_mlp_layer(x, g_pre, w1, b1, wdw, bdw, lng, lnb, w2, b2, g_post, g2_pre, g2_post,
                    w_up, w_down):
    row = lambda v: v.reshape(1, -1)
    resident = pl.Buffered(1)
    const = lambda shape, **kw: pl.BlockSpec(shape, lambda s: (0,) * len(shape), **kw)
    halo_blocks = CONV_TS // HALO

    def tile(s):
        t = jnp.minimum(s, N_TILES - 1)
        return t // TILES_PER_SEQ, t % TILES_PER_SEQ

    def main_map(s):
        b, j = tile(s)
        return b, j, 0

    def top_map(s):
        b, j = tile(s)
        return b, jnp.maximum(j * halo_blocks - 1, 0), 0

    def bot_map(s):
        b, j = tile(s)
        return b, jnp.minimum((j + 1) * halo_blocks, SEQ // HALO - 1), 0

    gain = const((1, D_MODEL))
    return pl.pallas_call(
        _conv_mlp_kernel,
        grid=(N_TILES + 1,),
        in_specs=[pl.BlockSpec((1, CONV_TS, D_MODEL), main_map),
                  pl.BlockSpec((1, HALO, D_MODEL), top_map),
                  pl.BlockSpec((1, HALO, D_MODEL), bot_map),
                  gain,
                  const((D_MODEL, 2 * D_MODEL), pipeline_mode=resident),
                  const((1, 2 * D_MODEL)),
                  const((CONV_WIDTH, D_MODEL)),
                  gain, gain, gain,
                  const((D_MODEL, D_MODEL), pipeline_mode=resident),
                  gain, gain, gain, gain,
                  const((D_MODEL, D_FF), pipeline_mode=resident),
                  const((D_FF, D_MODEL), pipeline_mode=resident)],
        out_specs=pl.BlockSpec((CONV_TS, D_MODEL), lambda s: (jnp.maximum(s - 1, 0), 0)),
        out_shape=jax.ShapeDtypeStruct((BATCH * SEQ, D_MODEL), F32),
        scratch_shapes=[pltpu.VMEM((CONV_ROWS, D_MODEL), BF16),
                        pltpu.VMEM((D_MODEL // LANES, CONV_ROWS, LANES), F32),
                        pltpu.VMEM((D_MODEL // LANES, CONV_TS, LANES), F32),
                        pltpu.VMEM((CONV_TS, D_MODEL), F32),
                        pltpu.VMEM((CONV_TS, D_MODEL), F32)],
        compiler_params=_params(("arbitrary",)),
        name="conformer_conv_mlp",
    )(x, x, x, row(g_pre), w1, row(b1), wdw, row(bdw), row(lng), row(lnb), w2, row(b2),
      row(g_post), row(g2_pre), row(g2_post), w_up, w_down)


def kernel(x, rel_bias, norm_mix_pre, norm_mix_post, norm_mlp_pre, norm_mlp_post, attn_w_qkv,
           attn_w_o, conv_w_pw1, conv_b_pw1, conv_w_dw, conv_b_dw, conv_ln_g, conv_ln_b,
           conv_w_pw2, conv_b_pw2, mlp_w_up, mlp_w_down):
    n_tok = BATCH * SEQ
    x2d = x.reshape(n_tok, D_MODEL)

    h = _prenorm(x, norm_mix_pre[0])
    w_qkv = attn_w_qkv.reshape(D_MODEL, N_GROUPS * GROUP_WIDTH)
    qkv = _qkv_proj(h, w_qkv)
    a = _attention(qkv, rel_bias)
    x2d, (w_pw1, w_pw2, w_up1, w_down1) = _proj_mlp(
        x2d, a.reshape(n_tok, D_MODEL), attn_w_o, norm_mix_post[0], norm_mlp_pre[0],
        norm_mlp_post[0], mlp_w_up, mlp_w_down,
        to_bf16=[(conv_w_pw1, 0), (conv_w_pw2, 0), (mlp_w_up, 1), (mlp_w_down, 1)])

    x2d = _conv_mlp_layer(x2d.reshape(BATCH, SEQ, D_MODEL), norm_mix_pre[1], w_pw1, conv_b_pw1[0],
                          conv_w_dw[0], conv_b_dw[0], conv_ln_g[0], conv_ln_b[0], w_pw2,
                          conv_b_pw2[0], norm_mix_post[1], norm_mlp_pre[1], norm_mlp_post[1],
                          w_up1, w_down1)
    return x2d.reshape(BATCH, SEQ, D_MODEL)
```

```python
import functools
import math

import jax
import jax.numpy as jnp
from jax import lax
from jax.experimental import pallas as pl
from jax.experimental.pallas import tpu as pltpu

D_MODEL = 1024
BATCH = 8
SEQ = 2048
HEAD_DIM = 64
N_HEADS = 16
DILATIONS = (1, 4, 16)
N_SIDE = 64
N_GROUPS = 3
GROUP_WIDTH = 3 * N_HEADS * HEAD_DIM
N_BUCKETS = 32
MAX_DISTANCE = 1024
CONV_WIDTH = 31
D_FF = 4 * D_MODEL
RMS_EPS = 1e-6
LN_EPS = 1e-5
NEG_INF = -1e30

F32 = jnp.float32
BF16 = jnp.bfloat16

LANES = 128
Q_TILE = 128
K_TILE = Q_TILE + 2 * N_SIDE
VMEM_LIMIT = 56 * 1024 * 1024


def _rms(x, g):
    return x * lax.rsqrt(jnp.mean(x * x, axis=-1, keepdims=True) + RMS_EPS) * g


def _params(semantics):
    return pltpu.CompilerParams(dimension_semantics=semantics, vmem_limit_bytes=VMEM_LIMIT)


NORM_ROWS = 256


def _prenorm_kernel(x_ref, g_ref, *refs):
    out_ref, slab_ref, regroup_ref = refs
    assert DILATIONS == (1, 4, 16)

    def chunk(i, carry):
        rows = pl.ds(pl.multiple_of(i * NORM_ROWS, NORM_ROWS), NORM_ROWS)
        hn = _rms(x_ref[0, rows, :], g_ref[...])
        for k in range(D_MODEL // LANES):
            slab_ref[k, rows, :] = hn[:, k * LANES:(k + 1) * LANES]
        out_ref[0, 0, rows, :] = hn.astype(BF16)
        return carry

    lax.fori_loop(0, SEQ // NORM_ROWS, chunk, 0)
    r1 = DILATIONS[1]
    L1, L2 = SEQ // r1, SEQ // DILATIONS[2]
    for k in range(D_MODEL // LANES):
        lanes = slice(k * LANES, (k + 1) * LANES)
        for c in range(r1):
            v = slab_ref[k, pl.ds(c, L1, stride=r1), :]
            out_ref[1, 0, c * L1:(c + 1) * L1, lanes] = v.astype(BF16)
            regroup_ref[c * L1:(c + 1) * L1, :] = v
        for c in range(DILATIONS[2]):
            start = (c % r1) * L1 + c // r1
            out_ref[2, 0, c * L2:(c + 1) * L2, lanes] = (
                regroup_ref[pl.ds(start, L2, stride=r1), :].astype(BF16))


def _prenorm(x, g):
    return pl.pallas_call(
        _prenorm_kernel,
        grid=(BATCH,),
        in_specs=[pl.BlockSpec((1, SEQ, D_MODEL), lambda b: (b, 0, 0)),
                  pl.BlockSpec((1, D_MODEL), lambda b: (0, 0))],
        out_specs=pl.BlockSpec((N_GROUPS, 1, SEQ, D_MODEL), lambda b: (0, b, 0, 0)),
        out_shape=jax.ShapeDtypeStruct((N_GROUPS, BATCH, SEQ, D_MODEL), BF16),
        scratch_shapes=[pltpu.VMEM((D_MODEL // LANES, SEQ, LANES), F32),
                        pltpu.VMEM((SEQ, LANES), F32)],
        compiler_params=_params(("arbitrary",)),
        name="prenorm",
    )(x, g.reshape(1, D_MODEL))


QKV_TN = 1536
QKV_TM = 512
Q_WIDTH = N_HEADS * HEAD_DIM
assert QKV_TN >= Q_WIDTH
LOG2E = math.log2(math.e)
Q_SCALE = HEAD_DIM ** -0.5 * LOG2E


def _qkv_kernel(h_ref, w_ref, *refs, n_cast):
    f32_refs, o_ref, bf16_refs, wb_ref = (refs[:n_cast], refs[n_cast], refs[n_cast + 1:-1],
                                          refs[-1])
    for src_ref, dst_ref in zip(f32_refs, bf16_refs):
        dst_ref[...] = src_ref[...].astype(BF16)

    @pl.when(pl.program_id(1) == 0)
    def _():
        wb_ref[...] = w_ref[...].astype(BF16)

    col = lax.broadcasted_iota(jnp.int32, (1, QKV_TN), 1)
    first_tile = pl.program_id(0) % (GROUP_WIDTH // QKV_TN) == 0
    scale = jnp.where(first_tile & (col < Q_WIDTH), Q_SCALE, 1.0)
    for m in range(SEQ // QKV_TM):
        rows = slice(m * QKV_TM, (m + 1) * QKV_TM)
        res = jnp.dot(h_ref[0, rows, :], wb_ref[...], preferred_element_type=F32)
        o_ref[0, rows, :] = (res * scale).astype(BF16)


QKV_CAST_STEPS = 32


def _qkv_proj(h, w_qkv, to_bf16):
    nj = GROUP_WIDTH // QKV_TN
    assert N_GROUPS * nj * BATCH >= QKV_CAST_STEPS
    cast_step = lambda i, b: jnp.minimum(i * BATCH + b, QKV_CAST_STEPS - 1)
    cast_in, cast_out, cast_shape = [], [], []
    for w, layer in to_bf16:
        _, rows, cols = w.shape
        blk = (rows // QKV_CAST_STEPS, cols)
        cast_in.append(pl.BlockSpec((None,) + blk,
                                    lambda i, b, layer=layer: (layer, cast_step(i, b), 0)))
        cast_out.append(pl.BlockSpec(blk, lambda i, b: (cast_step(i, b), 0)))
        cast_shape.append(jax.ShapeDtypeStruct((rows, cols), BF16))
    out = pl.pallas_call(
        functools.partial(_qkv_kernel, n_cast=len(to_bf16)),
        grid=(N_GROUPS * nj, BATCH),
        in_specs=[pl.BlockSpec((None, 1, SEQ, D_MODEL), lambda i, b: (i // nj, b, 0, 0)),
                  pl.BlockSpec((D_MODEL, QKV_TN), lambda i, b: (0, i))] + cast_in,
        out_specs=[pl.BlockSpec((None, 1, SEQ, QKV_TN), lambda i, b: (i // nj, b, 0, i % nj))]
        + cast_out,
        out_shape=[jax.ShapeDtypeStruct((N_GROUPS, BATCH, SEQ, GROUP_WIDTH), BF16)] + cast_shape,
        scratch_shapes=[pltpu.VMEM((D_MODEL, QKV_TN), BF16)],
        compiler_params=_params(("arbitrary", "arbitrary")),
        name="qkv_proj",
    )(h, w_qkv, *[w for w, _ in to_bf16])
    return out[0], out[1:]


HEADS_PER_STEP = 4
PAIRS_PER_STEP = HEADS_PER_STEP // 2
STEP_LANES = HEADS_PER_STEP * HEAD_DIM
N_Q_TILES = SEQ // Q_TILE
N_KINDS = 3
PADDED_PITCH = 24
GROUP_ROWS = tuple(SEQ // 16 * PADDED_PITCH if r % 16 == 0 else SEQ for r in DILATIONS)
GROUP_BASE = tuple(sum(GROUP_ROWS[:g]) for g in range(N_GROUPS))


def _bias_row(rel_ref, g, head):
    r = DILATIONS[g]
    nb = N_BUCKETS // 2
    max_exact = nb // 2
    delta = lax.broadcasted_iota(jnp.int32, (8, K_TILE), 1) - N_SIDE
    rel = delta * r
    n = jnp.abs(rel)
    nf = jnp.maximum(n, 1).astype(F32)
    large = max_exact + (jnp.log(nf / max_exact) / math.log(MAX_DISTANCE / max_exact)
                         * (nb - max_exact)).astype(jnp.int32)
    large = jnp.minimum(large, nb - 1)
    bucket = jnp.where(rel > 0, nb, 0) + jnp.where(n < max_exact, n, large)
    u = jnp.zeros((8, K_TILE), F32)
    for b in range(N_BUCKETS):
        u = jnp.where(bucket == b, rel_ref[b, g * N_HEADS + head], u)
    return jnp.where(jnp.abs(delta) <= N_SIDE, u * LOG2E, NEG_INF)


def _build_bias_tiles(rel_ref, bias_ref, head0):
    qi = lax.broadcasted_iota(jnp.int32, (Q_TILE, K_TILE), 0)
    kj = lax.broadcasted_iota(jnp.int32, (Q_TILE, K_TILE), 1)
    for g in range(N_GROUPS):
        for h in range(HEADS_PER_STEP):
            u = _bias_row(rel_ref, g, head0 + h)
            ub = jnp.broadcast_to(u[0:1, :], (Q_TILE, K_TILE))
            for kind in range(N_KINDS):
                shift = (N_SIDE * kind - N_SIDE) % K_TILE
                t = pltpu.roll(ub, shift, 1, stride=1, stride_axis=0)
                in_band = jnp.abs(kj - qi - N_SIDE * kind) <= N_SIDE
                bias_ref[g, h, kind] = jnp.where(in_band, t, NEG_INF)


def _token_rows(g, c, l0):
    r = DILATIONS[g]
    if r == 1:
        return pl.ds(GROUP_BASE[g] + l0, Q_TILE)
    pitch = PADDED_PITCH if r % 16 == 0 else r
    return pl.ds(GROUP_BASE[g] + l0 * pitch + c, Q_TILE, stride=pitch)


def _load_tokens(ref, pair, g, t):
    r = DILATIONS[g]
    if r % 16:
        return ref[pair, pl.ds(pl.multiple_of(GROUP_BASE[g] + t * Q_TILE, Q_TILE), Q_TILE), :]
    n = Q_TILE // r
    base = pl.multiple_of(GROUP_BASE[g] + t * n * PADDED_PITCH, 8)
    return jnp.concatenate([ref[pair, pl.ds(base + i * PADDED_PITCH, r), :] for i in range(n)],
                           axis=0)


def _split_heads(q):
    first = lax.broadcasted_iota(jnp.int32, (1, LANES), 1) < HEAD_DIM
    zero = jnp.zeros_like(q)
    return jnp.concatenate([jnp.where(first, q, zero), jnp.where(first, zero, q)], axis=0)


def _softmax_pv(s, v):
    first = lax.broadcasted_iota(jnp.int32, (1, LANES), 1) < HEAD_DIM
    m = jnp.max(s, axis=-1, keepdims=True)
    p = jnp.exp2(s - m).astype(BF16)
    v1 = jnp.concatenate([v, jnp.ones((v.shape[0], LANES), BF16)], axis=1)
    o2 = jnp.dot(p, v1, preferred_element_type=F32)
    acc = jnp.where(first, o2[:Q_TILE, :LANES], o2[Q_TILE:, :LANES])
    l = jnp.where(first, o2[:Q_TILE, LANES:], o2[Q_TILE:, LANES:])
    m_t = jnp.where(first, m[:Q_TILE], m[Q_TILE:])
    return acc, m_t, l


_NT = (((1,), (1,)), ((), ()))


def _attn_kernel(rel_ref, q0_ref, k0_ref, v0_ref, q1_ref, k1_ref, v1_ref, q2_ref, k2_ref, v2_ref,
                 o_ref, bias_ref, acc_ref, m_ref, l_ref):
    hp = pl.program_id(0)

    @pl.when(pl.program_id(1) == 0)
    def _():
        _build_bias_tiles(rel_ref, bias_ref, hp * HEADS_PER_STEP)

    qkv = ((q0_ref, k0_ref, v0_ref), (q1_ref, k1_ref, v1_ref), (q2_ref, k2_ref, v2_ref))
    for pair in range(PAIRS_PER_STEP):
        lanes = slice(pair * LANES, (pair + 1) * LANES)
        for g in range(N_GROUPS):
            r = DILATIONS[g]
            tiles_per_seq = SEQ // r // Q_TILE
            q_ref, k_ref, v_ref = qkv[g]

            def put(t, res, g=g, tiles_per_seq=tiles_per_seq, pair=pair):
                rows = _token_rows(g, t // tiles_per_seq, (t % tiles_per_seq) * Q_TILE)
                for ref, val in zip((acc_ref, m_ref, l_ref), res):
                    ref[pair, rows, :] = val

            if tiles_per_seq > 1:
                for t in range(N_Q_TILES):
                    tl = t % tiles_per_seq
                    kind = 0 if tl == 0 else (2 if tl == tiles_per_seq - 1 else 1)
                    q0 = t * Q_TILE
                    k0 = q0 - N_SIDE * kind
                    q2 = _split_heads(q_ref[0, q0:q0 + Q_TILE, lanes])
                    s = lax.dot_general(q2, k_ref[0, k0:k0 + K_TILE, lanes], _NT,
                                        preferred_element_type=F32)
                    s = s + jnp.concatenate([bias_ref[g, 2 * pair, kind],
                                             bias_ref[g, 2 * pair + 1, kind]], axis=0)
                    put(t, _softmax_pv(s, v_ref[0, k0:k0 + K_TILE, lanes]))
            else:
                bias = jnp.concatenate([bias_ref[g, 2 * pair, 0, :, :Q_TILE],
                                        bias_ref[g, 2 * pair + 1, 0, :, :Q_TILE]], axis=0)
                for t in range(0, N_Q_TILES, 2):
                    q0 = t * Q_TILE
                    q = q_ref[0, q0:q0 + 2 * Q_TILE, lanes]
                    q2 = jnp.concatenate([_split_heads(q[:Q_TILE]), _split_heads(q[Q_TILE:])], axis=0)
                    s = lax.dot_general(q2, k_ref[0, q0:q0 + 2 * Q_TILE, lanes], _NT,
                                        preferred_element_type=F32)
                    for i in range(2):
                        si = s[2 * i * Q_TILE:2 * (i + 1) * Q_TILE, i * Q_TILE:(i + 1) * Q_TILE]
                        v = v_ref[0, q0 + i * Q_TILE:q0 + (i + 1) * Q_TILE, lanes]
                        put(t + i, _softmax_pv(si + bias, v))

    for pair in range(PAIRS_PER_STEP):
        lanes = slice(pair * LANES, (pair + 1) * LANES)

        def merge_body(t, carry, lanes=lanes, pair=pair):
            m = [_load_tokens(m_ref, pair, g, t) for g in range(N_GROUPS)]
            top = jnp.maximum(jnp.maximum(m[0], m[1]), m[2])
            w = [jnp.exp2(x - top) for x in m]
            num = sum(w[g] * _load_tokens(acc_ref, pair, g, t) for g in range(N_GROUPS))
            den = sum(w[g] * _load_tokens(l_ref, pair, g, t) for g in range(N_GROUPS))
            rows = pl.ds(pl.multiple_of(t * Q_TILE, Q_TILE), Q_TILE)
            o_ref[0, rows, lanes] = (num / den).astype(BF16)
            return carry

        lax.fori_loop(0, N_Q_TILES, merge_body, 0, unroll=4)


def _attention(qkv, rel_bias):
    n_hp = N_HEADS // HEADS_PER_STEP
    blk = (1, SEQ, STEP_LANES)
    grp_blk = (None,) + blk
    specs = [pl.BlockSpec(memory_space=pltpu.SMEM)]
    args = [rel_bias]
    for g in range(N_GROUPS):
        for part in range(3):
            specs.append(pl.BlockSpec(grp_blk,
                                      lambda hp, b, g=g, part=part: (g, b, 0, part * n_hp + hp)))
            args.append(qkv)
    return pl.pallas_call(
        _attn_kernel,
        grid=(n_hp, BATCH),
        in_specs=specs,
        out_specs=pl.BlockSpec(blk, lambda hp, b: (b, 0, hp)),
        out_shape=jax.ShapeDtypeStruct((BATCH, SEQ, D_MODEL), BF16),
        scratch_shapes=[pltpu.VMEM((N_GROUPS, HEADS_PER_STEP, N_KINDS, Q_TILE, K_TILE), F32),
                        pltpu.VMEM((PAIRS_PER_STEP, sum(GROUP_ROWS), LANES), F32),
                        pltpu.VMEM((PAIRS_PER_STEP, sum(GROUP_ROWS), LANES), F32),
                        pltpu.VMEM((PAIRS_PER_STEP, sum(GROUP_ROWS), LANES), F32)],
        compiler_params=_params(("arbitrary", "arbitrary")),
        name="dilated_attention",
    )(*args)


MLP_TM = 512
MLP_FC = 1024
MLP_STAGES = 2 + 2 * (D_FF // MLP_FC)


_DONE = object()


def _interleave(order, **stage_generators):
    for key in order:
        next(stage_generators[key], _DONE)
    for gen in stage_generators.values():
        assert next(gen, _DONE) is _DONE, "order does not cover every stage"


def _mlp_stages(read_x, gpre_ref, gpost_ref, wup_ref, wdn_ref, o_ref):
    h = _rms(read_x(), gpre_ref[...]).astype(BF16)
    acc = jnp.zeros((MLP_TM, D_MODEL), F32)
    yield
    for c in range(D_FF // MLP_FC):
        cols = slice(c * MLP_FC, (c + 1) * MLP_FC)
        u = jnp.dot(h, wup_ref[:, cols], preferred_element_type=F32)
        u = jnp.square(jnp.maximum(u, 0.0)).astype(BF16)
        yield
        acc = acc + jnp.dot(u, wdn_ref[cols, :], preferred_element_type=F32)
        yield
    o_ref[...] = read_x() + _rms(acc, gpost_ref[...])


def _proj_mlp_kernel(a_ref, wo_ref, gmix_ref, x_ref, gpre_ref, gpost_ref, wup_ref, wdn_ref,
                     *refs, n_cast):
    f32_refs, o_ref, bf16_refs = refs[:n_cast], refs[n_cast], refs[n_cast + 1:]
    for src_ref, dst_ref in zip(f32_refs, bf16_refs):
        dst_ref[...] = src_ref[...].astype(BF16)
    m = jnp.dot(a_ref[...], wo_ref[...], preferred_element_type=F32)
    x = x_ref[...] + _rms(m, gmix_ref[...])
    _interleave("m" * MLP_STAGES,
                m=_mlp_stages(lambda: x, gpre_ref, gpost_ref, wup_ref, wdn_ref, o_ref))


def _proj_mlp(x2d, a2d, w_o, g_mix, g_pre, g_post, w_up, w_down, to_bf16):
    n = x2d.shape[0]
    steps = n // MLP_TM
    resident = pl.Buffered(1)
    tile = pl.BlockSpec((MLP_TM, D_MODEL), lambda i: (i, 0))
    gain = pl.BlockSpec((1, D_MODEL), lambda i: (0, 0))
    cast_in, cast_out, cast_shape = [], [], []
    for w, layer in to_bf16:
        _, rows, cols = w.shape
        cast_in.append(pl.BlockSpec((None, rows // steps, cols), lambda i, layer=layer: (layer, i, 0)))
        cast_out.append(pl.BlockSpec((rows // steps, cols), lambda i: (i, 0)))
        cast_shape.append(jax.ShapeDtypeStruct((rows, cols), BF16))
    out = pl.pallas_call(
        functools.partial(_proj_mlp_kernel, n_cast=len(to_bf16)),
        grid=(steps,),
        in_specs=[tile,
                  pl.BlockSpec((D_MODEL, D_MODEL), lambda i: (0, 0), pipeline_mode=resident),
                  gain, tile, gain, gain,
                  pl.BlockSpec((D_MODEL, D_FF), lambda i: (0, 0), pipeline_mode=resident),
                  pl.BlockSpec((D_FF, D_MODEL), lambda i: (0, 0), pipeline_mode=resident)]
        + cast_in,
        out_specs=[tile] + cast_out,
        out_shape=[jax.ShapeDtypeStruct((n, D_MODEL), F32)] + cast_shape,
        compiler_params=_params(("arbitrary",)),
        name="attn_proj_mlp",
    )(a2d, w_o, g_mix.reshape(1, D_MODEL), x2d, g_pre.reshape(1, D_MODEL),
      g_post.reshape(1, D_MODEL), w_up, w_down, *[w for w, _ in to_bf16])
    return out[0], out[1:]


CONV_TS = 512
HALO = 16
CONV_ROWS = CONV_TS + 2 * HALO
CONV_RB = 128
CONV_NC = 256


N_TILES = BATCH * SEQ // CONV_TS
TILES_PER_SEQ = SEQ // CONV_TS
assert CONV_TS == MLP_TM


def _conv_stages(xm_ref, xt_ref, xb_ref, first, last, gpre_ref, w1_ref, b1_ref, wdw_ref, bdw_ref,
                 lng_ref, lnb_ref, w2_ref, b2_ref, gpost_ref, o_ref, h_ref, u_ref, c_ref):
    gpre = gpre_ref[...]
    h_ref[0:HALO, :] = _rms(xt_ref[0], gpre).astype(BF16)
    h_ref[HALO:HALO + CONV_TS, :] = _rms(xm_ref[0], gpre).astype(BF16)
    h_ref[HALO + CONV_TS:, :] = _rms(xb_ref[0], gpre).astype(BF16)
    yield

    h = h_ref[...]
    half = CONV_RB // 2
    for nc in range(D_MODEL // CONV_NC):
        cols = slice(nc * CONV_NC, (nc + 1) * CONV_NC)
        gcols = slice(D_MODEL + nc * CONV_NC, D_MODEL + (nc + 1) * CONV_NC)
        a = jnp.dot(h, w1_ref[:, cols], preferred_element_type=F32) + b1_ref[:, cols]
        gate = jnp.dot(h, w1_ref[:, gcols], preferred_element_type=F32) + b1_ref[:, gcols]
        u = a * jax.nn.sigmoid(gate)
        for k in range(CONV_NC // LANES):
            lc = nc * CONV_NC // LANES + k
            lanes = slice(lc * LANES, (lc + 1) * LANES)
            uk = u[:, k * LANES:(k + 1) * LANES]
            u_ref[lc, 0:HALO, :] = jnp.where(first, 0.0, uk[0:HALO])
            u_ref[lc, HALO:HALO + CONV_TS, :] = uk[HALO:HALO + CONV_TS]
            u_ref[lc, HALO + CONV_TS:, :] = jnp.where(last, 0.0, uk[HALO + CONV_TS:])
        yield
        for k in range(CONV_NC // LANES):
            lc = nc * CONV_NC // LANES + k
            lanes = slice(lc * LANES, (lc + 1) * LANES)
            for r0 in range(0, CONV_TS, CONV_RB):
                for phase in range(2):
                    acc = bdw_ref[:, lanes]
                    for t in range(CONV_WIDTH):
                        start = r0 + phase + t + HALO - CONV_WIDTH // 2
                        acc = acc + (u_ref[lc, pl.ds(start, half, stride=2), :]
                                     * wdw_ref[t:t + 1, lanes])
                    c_ref[lc, pl.ds(r0 + phase, half, stride=2), :] = acc
            yield

    v = jnp.concatenate([c_ref[lc] for lc in range(D_MODEL // LANES)], axis=1)
    mu = jnp.mean(v, axis=-1, keepdims=True)
    var = jnp.mean(jnp.square(v - mu), axis=-1, keepdims=True)
    y = (v - mu) * lax.rsqrt(var + LN_EPS) * lng_ref[...] + lnb_ref[...]
    y = (y * jax.nn.sigmoid(y)).astype(BF16)
    yield
    z = jnp.dot(y, w2_ref[...], preferred_element_type=F32) + b2_ref[...]
    o_ref[...] = xm_ref[0] + _rms(z, gpost_ref[...])


def _conv_mlp_kernel(xm_ref, xt_ref, xb_ref, gpre_ref, w1_ref, b1_ref, wdw_ref, bdw_ref, lng_ref,
                     lnb_ref, w2_ref, b2_ref, gpost_ref, g2pre_ref, g2post_ref, wup_ref, wdn_ref,
                     o_ref, h_ref, u_ref, c_ref, x_ref, xnew_ref):
    s = pl.program_id(0)
    j = s % TILES_PER_SEQ

    def conv():
        return _conv_stages(xm_ref, xt_ref, xb_ref, j == 0, j == TILES_PER_SEQ - 1, gpre_ref,
                            w1_ref, b1_ref, wdw_ref, bdw_ref, lng_ref, lnb_ref, w2_ref, b2_ref,
                            gpost_ref, xnew_ref, h_ref, u_ref, c_ref)

    def mlp():
        return _mlp_stages(lambda: x_ref[...], g2pre_ref, g2post_ref, wup_ref, wdn_ref, o_ref)

    order = "ccccc" + "mm" + "cmccm" * 2 + "cmcm" * 2

    @pl.when(s == 0)
    def _():
        _interleave("c" * order.count("c"), c=conv())
        x_ref[...] = xnew_ref[...]

    @pl.when((s > 0) & (s < N_TILES))
    def _():
        _interleave(order, c=conv(), m=mlp())
        x_ref[...] = xnew_ref[...]

    @pl.when(s == N_TILES)
    def _():
        _interleave("m" * order.count("m"), m=mlp())


def _conv_mlp_layer(x, g_pre, w1, b1, wdw, bdw, lng, lnb, w2, b2, g_post, g2_pre, g2_post,
                    w_up, w_down):
    row = lambda v: v.reshape(1, -1)
    resident = pl.Buffered(1)
    const = lambda shape, **kw: pl.BlockSpec(shape, lambda s: (0,) * len(shape), **kw)
    halo_blocks = CONV_TS // HALO

    def tile(s):
        t = jnp.minimum(s, N_TILES - 1)
        return t // TILES_PER_SEQ, t % TILES_PER_SEQ

    def main_map(s):
        b, j = tile(s)
        return b, j, 0

    def top_map(s):
        b, j = tile(s)
        return b, jnp.maximum(j * halo_blocks - 1, 0), 0

    def bot_map(s):
        b, j = tile(s)
        return b, jnp.minimum((j + 1) * halo_blocks, SEQ // HALO - 1), 0

    gain = const((1, D_MODEL))
    return pl.pallas_call(
        _conv_mlp_kernel,
        grid=(N_TILES + 1,),
        in_specs=[pl.BlockSpec((1, CONV_TS, D_MODEL), main_map),
                  pl.BlockSpec((1, HALO, D_MODEL), top_map),
                  pl.BlockSpec((1, HALO, D_MODEL), bot_map),
                  gain,
                  const((D_MODEL, 2 * D_MODEL), pipeline_mode=resident),
                  const((1, 2 * D_MODEL)),
                  const((CONV_WIDTH, D_MODEL)),
                  gain, gain, gain,
                  const((D_MODEL, D_MODEL), pipeline_mode=resident),
                  gain, gain, gain, gain,
                  const((D_MODEL, D_FF), pipeline_mode=resident),
                  const((D_FF, D_MODEL), pipeline_mode=resident)],
        out_specs=pl.BlockSpec((CONV_TS, D_MODEL), lambda s: (jnp.maximum(s - 1, 0), 0)),
        out_shape=jax.ShapeDtypeStruct((BATCH * SEQ, D_MODEL), F32),
        scratch_shapes=[pltpu.VMEM((CONV_ROWS, D_MODEL), BF16),
                        pltpu.VMEM((D_MODEL // LANES, CONV_ROWS, LANES), F32),
                        pltpu.VMEM((D_MODEL // LANES, CONV_TS, LANES), F32),
                        pltpu.VMEM((CONV_TS, D_MODEL), F32),
                        pltpu.VMEM((CONV_TS, D_MODEL), F32)],
        compiler_params=_params(("arbitrary",)),
        name="conformer_conv_mlp",
    )(x, x, x, row(g_pre), w1, row(b1), wdw, row(bdw), row(lng), row(lnb), w2, row(b2),
      row(g_post), row(g2_pre), row(g2_post), w_up, w_down)


def kernel(x, rel_bias, norm_mix_pre, norm_mix_post, norm_mlp_pre, norm_mlp_post, attn_w_qkv,
           attn_w_o, conv_w_pw1, conv_b_pw1, conv_w_dw, conv_b_dw, conv_ln_g, conv_ln_b,
           conv_w_pw2, conv_b_pw2, mlp_w_up, mlp_w_down):
    n_tok = BATCH * SEQ
    x2d = x.reshape(n_tok, D_MODEL)

    h = _prenorm(x, norm_mix_pre[0])
    w_qkv = attn_w_qkv.reshape(D_MODEL, N_GROUPS * GROUP_WIDTH)
    qkv, (w_o, w_up0, w_down0) = _qkv_proj(
        h, w_qkv, to_bf16=[(attn_w_o, 0), (mlp_w_up, 0), (mlp_w_down, 0)])
    a = _attention(qkv, rel_bias)
    x2d, (w_pw1, w_pw2, w_up1, w_down1) = _proj_mlp(
        x2d, a.reshape(n_tok, D_MODEL), w_o, norm_mix_post[0], norm_mlp_pre[0],
        norm_mlp_post[0], w_up0, w_down0,
        to_bf16=[(conv_w_pw1, 0), (conv_w_pw2, 0), (mlp_w_up, 1), (mlp_w_down, 1)])

    x2d = _conv_mlp_layer(x2d.reshape(BATCH, SEQ, D_MODEL), norm_mix_pre[1], w_pw1, conv_b_pw1[0],
                          conv_w_dw[0], conv_b_dw[0], conv_ln_g[0], conv_ln_b[0], w_pw2,
                          conv_b_pw2[0], norm_mix_post[1], norm_mlp_pre[1], norm_mlp_post[1],
                          w_up1, w_down1)
    return x2d.reshape(BATCH, SEQ, D_MODEL)
```

```python
import functools
import math

import jax
import jax.numpy as jnp
from jax import lax
from jax.experimental import pallas as pl
from jax.experimental.pallas import tpu as pltpu

D_MODEL = 1024
BATCH = 8
SEQ = 2048
HEAD_DIM = 64
N_HEADS = 16
DILATIONS = (1, 4, 16)
N_SIDE = 64
N_GROUPS = 3
GROUP_WIDTH = 3 * N_HEADS * HEAD_DIM
N_BUCKETS = 32
MAX_DISTANCE = 1024
CONV_WIDTH = 31
D_FF = 4 * D_MODEL
RMS_EPS = 1e-6
LN_EPS = 1e-5
NEG_INF = -1e30

F32 = jnp.float32
BF16 = jnp.bfloat16

LANES = 128
Q_TILE = 128
K_TILE = Q_TILE + 2 * N_SIDE
VMEM_LIMIT = 56 * 1024 * 1024


def _rms(x, g):
    return x * lax.rsqrt(jnp.mean(x * x, axis=-1, keepdims=True) + RMS_EPS) * g


def _params(semantics):
    return pltpu.CompilerParams(dimension_semantics=semantics, vmem_limit_bytes=VMEM_LIMIT)


NORM_ROWS = 256


def _prenorm_kernel(x_ref, g_ref, *refs):
    out_ref, slab_ref, regroup_ref = refs
    assert DILATIONS == (1, 4, 16)

    def chunk(i, carry):
        rows = pl.ds(pl.multiple_of(i * NORM_ROWS, NORM_ROWS), NORM_ROWS)
        hn = _rms(x_ref[0, rows, :], g_ref[...])
        for k in range(D_MODEL // LANES):
            slab_ref[k, rows, :] = hn[:, k * LANES:(k + 1) * LANES]
        out_ref[0, 0, rows, :] = hn.astype(BF16)
        return carry

    lax.fori_loop(0, SEQ // NORM_ROWS, chunk, 0)
    r1 = DILATIONS[1]
    L1, L2 = SEQ // r1, SEQ // DILATIONS[2]
    for k in range(D_MODEL // LANES):
        lanes = slice(k * LANES, (k + 1) * LANES)
        for c in range(r1):
            v = slab_ref[k, pl.ds(c, L1, stride=r1), :]
            out_ref[1, 0, c * L1:(c + 1) * L1, lanes] = v.astype(BF16)
            regroup_ref[c * L1:(c + 1) * L1, :] = v
        for c in range(DILATIONS[2]):
            start = (c % r1) * L1 + c // r1
            out_ref[2, 0, c * L2:(c + 1) * L2, lanes] = (
                regroup_ref[pl.ds(start, L2, stride=r1), :].astype(BF16))


def _prenorm(x, g):
    return pl.pallas_call(
        _prenorm_kernel,
        grid=(BATCH,),
        in_specs=[pl.BlockSpec((1, SEQ, D_MODEL), lambda b: (b, 0, 0)),
                  pl.BlockSpec((1, D_MODEL), lambda b: (0, 0))],
        out_specs=pl.BlockSpec((N_GROUPS, 1, SEQ, D_MODEL), lambda b: (0, b, 0, 0)),
        out_shape=jax.ShapeDtypeStruct((N_GROUPS, BATCH, SEQ, D_MODEL), BF16),
        scratch_shapes=[pltpu.VMEM((D_MODEL // LANES, SEQ, LANES), F32),
                        pltpu.VMEM((SEQ, LANES), F32)],
        compiler_params=_params(("arbitrary",)),
        name="prenorm",
    )(x, g.reshape(1, D_MODEL))


QKV_TN = 1536
QKV_TM = 512
Q_WIDTH = N_HEADS * HEAD_DIM
assert QKV_TN >= Q_WIDTH
LOG2E = math.log2(math.e)
Q_SCALE = HEAD_DIM ** -0.5 * LOG2E


def _qkv_kernel(h_ref, w_ref, *refs, n_cast):
    f32_refs, o_ref, bf16_refs, wb_ref = (refs[:n_cast], refs[n_cast], refs[n_cast + 1:-1],
                                          refs[-1])
    for src_ref, dst_ref in zip(f32_refs, bf16_refs):
        dst_ref[...] = src_ref[...].astype(BF16)

    @pl.when(pl.program_id(1) == 0)
    def _():
        wb_ref[...] = w_ref[...].astype(BF16)

    col = lax.broadcasted_iota(jnp.int32, (1, QKV_TN), 1)
    first_tile = pl.program_id(0) % (GROUP_WIDTH // QKV_TN) == 0
    scale = jnp.where(first_tile & (col < Q_WIDTH), Q_SCALE, 1.0)
    for m in range(SEQ // QKV_TM):
        rows = slice(m * QKV_TM, (m + 1) * QKV_TM)
        res = jnp.dot(h_ref[0, rows, :], wb_ref[...], preferred_element_type=F32)
        o_ref[0, rows, :] = (res * scale).astype(BF16)


QKV_CAST_STEPS = 32


def _qkv_proj(h, w_qkv, to_bf16):
    nj = GROUP_WIDTH // QKV_TN
    assert N_GROUPS * nj * BATCH >= QKV_CAST_STEPS
    cast_step = lambda i, b: jnp.minimum(i * BATCH + b, QKV_CAST_STEPS - 1)
    cast_in, cast_out, cast_shape = [], [], []
    for w, layer in to_bf16:
        _, rows, cols = w.shape
        blk = (rows // QKV_CAST_STEPS, cols)
        cast_in.append(pl.BlockSpec((None,) + blk,
                                    lambda i, b, layer=layer: (layer, cast_step(i, b), 0)))
        cast_out.append(pl.BlockSpec(blk, lambda i, b: (cast_step(i, b), 0)))
        cast_shape.append(jax.ShapeDtypeStruct((rows, cols), BF16))
    out = pl.pallas_call(
        functools.partial(_qkv_kernel, n_cast=len(to_bf16)),
        grid=(N_GROUPS * nj, BATCH),
        in_specs=[pl.BlockSpec((None, 1, SEQ, D_MODEL), lambda i, b: (i // nj, b, 0, 0)),
                  pl.BlockSpec((D_MODEL, QKV_TN), lambda i, b: (0, i))] + cast_in,
        out_specs=[pl.BlockSpec((None, 1, SEQ, QKV_TN), lambda i, b: (i // nj, b, 0, i % nj))]
        + cast_out,
        out_shape=[jax.ShapeDtypeStruct((N_GROUPS, BATCH, SEQ, GROUP_WIDTH), BF16)] + cast_shape,
        scratch_shapes=[pltpu.VMEM((D_MODEL, QKV_TN), BF16)],
        compiler_params=_params(("arbitrary", "arbitrary")),
        name="qkv_proj",
    )(h, w_qkv, *[w for w, _ in to_bf16])
    return out[0], out[1:]


HEADS_PER_STEP = 4
PAIRS_PER_STEP = HEADS_PER_STEP // 2
STEP_LANES = HEADS_PER_STEP * HEAD_DIM
N_Q_TILES = SEQ // Q_TILE
N_KINDS = 3
PADDED_PITCH = 24
GROUP_ROWS = tuple(SEQ // 16 * PADDED_PITCH if r % 16 == 0 else SEQ for r in DILATIONS)
GROUP_BASE = tuple(sum(GROUP_ROWS[:g]) for g in range(N_GROUPS))


def _bias_row(rel_ref, g, head):
    r = DILATIONS[g]
    nb = N_BUCKETS // 2
    max_exact = nb // 2
    delta = lax.broadcasted_iota(jnp.int32, (8, K_TILE), 1) - N_SIDE
    rel = delta * r
    n = jnp.abs(rel)
    nf = jnp.maximum(n, 1).astype(F32)
    large = max_exact + (jnp.log(nf / max_exact) / math.log(MAX_DISTANCE / max_exact)
                         * (nb - max_exact)).astype(jnp.int32)
    large = jnp.minimum(large, nb - 1)
    bucket = jnp.where(rel > 0, nb, 0) + jnp.where(n < max_exact, n, large)
    u = jnp.zeros((8, K_TILE), F32)
    for b in range(N_BUCKETS):
        u = jnp.where(bucket == b, rel_ref[b, g * N_HEADS + head], u)
    return jnp.where(jnp.abs(delta) <= N_SIDE, u * LOG2E, NEG_INF)


def _build_bias_tiles(rel_ref, bias_ref, head0):
    qi = lax.broadcasted_iota(jnp.int32, (Q_TILE, K_TILE), 0)
    kj = lax.broadcasted_iota(jnp.int32, (Q_TILE, K_TILE), 1)
    for g in range(N_GROUPS):
        for h in range(HEADS_PER_STEP):
            u = _bias_row(rel_ref, g, head0 + h)
            ub = jnp.broadcast_to(u[0:1, :], (Q_TILE, K_TILE))
            for kind in range(N_KINDS):
                shift = (N_SIDE * kind - N_SIDE) % K_TILE
                t = pltpu.roll(ub, shift, 1, stride=1, stride_axis=0)
                in_band = jnp.abs(kj - qi - N_SIDE * kind) <= N_SIDE
                bias_ref[g, h, kind] = jnp.where(in_band, t, NEG_INF)


def _token_rows(g, c, l0):
    r = DILATIONS[g]
    if r == 1:
        return pl.ds(GROUP_BASE[g] + l0, Q_TILE)
    pitch = PADDED_PITCH if r % 16 == 0 else r
    return pl.ds(GROUP_BASE[g] + l0 * pitch + c, Q_TILE, stride=pitch)


def _load_tokens(ref, pair, g, t):
    r = DILATIONS[g]
    if r % 16:
        return ref[pair, pl.ds(pl.multiple_of(GROUP_BASE[g] + t * Q_TILE, Q_TILE), Q_TILE), :]
    n = Q_TILE // r
    base = pl.multiple_of(GROUP_BASE[g] + t * n * PADDED_PITCH, 8)
    return jnp.concatenate([ref[pair, pl.ds(base + i * PADDED_PITCH, r), :] for i in range(n)],
                           axis=0)


def _split_heads(q):
    first = lax.broadcasted_iota(jnp.int32, (1, LANES), 1) < HEAD_DIM
    zero = jnp.zeros_like(q)
    return jnp.concatenate([jnp.where(first, q, zero), jnp.where(first, zero, q)], axis=0)


def _softmax_pv(s, v):
    first = lax.broadcasted_iota(jnp.int32, (1, LANES), 1) < HEAD_DIM
    m = jnp.max(s, axis=-1, keepdims=True)
    p = jnp.exp2(s - m).astype(BF16)
    v1 = jnp.concatenate([v, jnp.ones((v.shape[0], LANES), BF16)], axis=1)
    o2 = jnp.dot(p, v1, preferred_element_type=F32)
    acc = jnp.where(first, o2[:Q_TILE, :LANES], o2[Q_TILE:, :LANES])
    l = jnp.where(first, o2[:Q_TILE, LANES:], o2[Q_TILE:, LANES:])
    m_t = jnp.where(first, m[:Q_TILE], m[Q_TILE:])
    return acc, m_t, l


_NT = (((1,), (1,)), ((), ()))


def _attn_kernel(rel_ref, q0_ref, k0_ref, v0_ref, q1_ref, k1_ref, v1_ref, q2_ref, k2_ref, v2_ref,
                 o_ref, bias_ref, acc_ref, m_ref, l_ref):
    hp = pl.program_id(0)

    @pl.when(pl.program_id(1) == 0)
    def _():
        _build_bias_tiles(rel_ref, bias_ref, hp * HEADS_PER_STEP)

    qkv = ((q0_ref, k0_ref, v0_ref), (q1_ref, k1_ref, v1_ref), (q2_ref, k2_ref, v2_ref))
    for pair in range(PAIRS_PER_STEP):
        lanes = slice(pair * LANES, (pair + 1) * LANES)
        for g in range(N_GROUPS):
            r = DILATIONS[g]
            tiles_per_seq = SEQ // r // Q_TILE
            q_ref, k_ref, v_ref = qkv[g]

            def put(t, res, g=g, tiles_per_seq=tiles_per_seq, pair=pair):
                rows = _token_rows(g, t // tiles_per_seq, (t % tiles_per_seq) * Q_TILE)
                for ref, val in zip((acc_ref, m_ref, l_ref), res):
                    ref[pair, rows, :] = val

            if tiles_per_seq > 1:
                for t in range(N_Q_TILES):
                    tl = t % tiles_per_seq
                    kind = 0 if tl == 0 else (2 if tl == tiles_per_seq - 1 else 1)
                    q0 = t * Q_TILE
                    k0 = q0 - N_SIDE * kind
                    q2 = _split_heads(q_ref[0, q0:q0 + Q_TILE, lanes])
                    s = lax.dot_general(q2, k_ref[0, k0:k0 + K_TILE, lanes], _NT,
                                        preferred_element_type=F32)
                    s = s + jnp.concatenate([bias_ref[g, 2 * pair, kind],
                                             bias_ref[g, 2 * pair + 1, kind]], axis=0)
                    put(t, _softmax_pv(s, v_ref[0, k0:k0 + K_TILE, lanes]))
            else:
                bias = jnp.concatenate([bias_ref[g, 2 * pair, 0, :, :Q_TILE],
                                        bias_ref[g, 2 * pair + 1, 0, :, :Q_TILE]], axis=0)
                for t in range(0, N_Q_TILES, 2):
                    q0 = t * Q_TILE
                    q = q_ref[0, q0:q0 + 2 * Q_TILE, lanes]
                    q2 = jnp.concatenate([_split_heads(q[:Q_TILE]), _split_heads(q[Q_TILE:])], axis=0)
                    s = lax.dot_general(q2, k_ref[0, q0:q0 + 2 * Q_TILE, lanes], _NT,
                                        preferred_element_type=F32)
                    for i in range(2):
                        si = s[2 * i * Q_TILE:2 * (i + 1) * Q_TILE, i * Q_TILE:(i + 1) * Q_TILE]
                        v = v_ref[0, q0 + i * Q_TILE:q0 + (i + 1) * Q_TILE, lanes]
                        put(t + i, _softmax_pv(si + bias, v))

    for pair in range(PAIRS_PER_STEP):
        lanes = slice(pair * LANES, (pair + 1) * LANES)

        def merge_body(t, carry, lanes=lanes, pair=pair):
            m = [_load_tokens(m_ref, pair, g, t) for g in range(N_GROUPS)]
            top = jnp.maximum(jnp.maximum(m[0], m[1]), m[2])
            w = [jnp.exp2(x - top) for x in m]
            num = sum(w[g] * _load_tokens(acc_ref, pair, g, t) for g in range(N_GROUPS))
            den = sum(w[g] * _load_tokens(l_ref, pair, g, t) for g in range(N_GROUPS))
            rows = pl.ds(pl.multiple_of(t * Q_TILE, Q_TILE), Q_TILE)
            o_ref[0, rows, lanes] = (num / den).astype(BF16)
            return carry

        lax.fori_loop(0, N_Q_TILES, merge_body, 0, unroll=4)


def _attention(qkv, rel_bias):
    n_hp = N_HEADS // HEADS_PER_STEP
    blk = (1, SEQ, STEP_LANES)
    grp_blk = (None,) + blk
    specs = [pl.BlockSpec(memory_space=pltpu.SMEM)]
    args = [rel_bias]
    for g in range(N_GROUPS):
        for part in range(3):
            specs.append(pl.BlockSpec(grp_blk,
                                      lambda hp, b, g=g, part=part: (g, b, 0, part * n_hp + hp)))
            args.append(qkv)
    return pl.pallas_call(
        _attn_kernel,
        grid=(n_hp, BATCH),
        in_specs=specs,
        out_specs=pl.BlockSpec(blk, lambda hp, b: (b, 0, hp)),
        out_shape=jax.ShapeDtypeStruct((BATCH, SEQ, D_MODEL), BF16),
        scratch_shapes=[pltpu.VMEM((N_GROUPS, HEADS_PER_STEP, N_KINDS, Q_TILE, K_TILE), F32),
                        pltpu.VMEM((PAIRS_PER_STEP, sum(GROUP_ROWS), LANES), F32),
                        pltpu.VMEM((PAIRS_PER_STEP, sum(GROUP_ROWS), LANES), F32),
                        pltpu.VMEM((PAIRS_PER_STEP, sum(GROUP_ROWS), LANES), F32)],
        compiler_params=_params(("arbitrary", "arbitrary")),
        name="dilated_attention",
    )(*args)


MLP_TM = 512
MLP_FC = 1024
MLP_STAGES = 2 + 2 * (D_FF // MLP_FC)


_DONE = object()


def _interleave(order, **stage_generators):
    for key in order:
        next(stage_generators[key], _DONE)
    for gen in stage_generators.values():
        assert next(gen, _DONE) is _DONE, "order does not cover every stage"


def _mlp_stages(read_x, gpre_ref, gpost_ref, wup_ref, wdn_ref, o_ref):
    h = _rms(read_x(), gpre_ref[...]).astype(BF16)
    acc = jnp.zeros(h.shape, F32)
    yield
    for c in range(D_FF // MLP_FC):
        cols = slice(c * MLP_FC, (c + 1) * MLP_FC)
        u = jnp.dot(h, wup_ref[:, cols], preferred_element_type=F32)
        u = jnp.square(jnp.maximum(u, 0.0)).astype(BF16)
        yield
        acc = acc + jnp.dot(u, wdn_ref[cols, :], preferred_element_type=F32)
        yield
    o_ref[...] = read_x() + _rms(acc, gpost_ref[...])


def _proj_mlp_kernel(a_ref, wo_ref, gmix_ref, x_ref, gpre_ref, gpost_ref, wup_ref, wdn_ref,
                     *refs, n_cast):
    f32_refs, o_ref, bf16_refs = refs[:n_cast], refs[n_cast], refs[n_cast + 1:]
    for src_ref, dst_ref in zip(f32_refs, bf16_refs):
        dst_ref[...] = src_ref[...].astype(BF16)

    def half(rows):
        m = jnp.dot(a_ref[rows, :], wo_ref[...], preferred_element_type=F32)
        x = x_ref[rows, :] + _rms(m, gmix_ref[...])
        yield
        yield from _mlp_stages(lambda: x, gpre_ref, gpost_ref, wup_ref, wdn_ref,
                               o_ref.at[rows, :])

    n = MLP_TM // 2
    _interleave("a" + "ab" * MLP_STAGES + "b", a=half(slice(0, n)), b=half(slice(n, 2 * n)))


def _proj_mlp(x2d, a2d, w_o, g_mix, g_pre, g_post, w_up, w_down, to_bf16):
    n = x2d.shape[0]
    steps = n // MLP_TM
    resident = pl.Buffered(1)
    tile = pl.BlockSpec((MLP_TM, D_MODEL), lambda i: (i, 0))
    gain = pl.BlockSpec((1, D_MODEL), lambda i: (0, 0))
    cast_in, cast_out, cast_shape = [], [], []
    for w, layer in to_bf16:
        _, rows, cols = w.shape
        cast_in.append(pl.BlockSpec((None, rows // steps, cols), lambda i, layer=layer: (layer, i, 0)))
        cast_out.append(pl.BlockSpec((rows // steps, cols), lambda i: (i, 0)))
        cast_shape.append(jax.ShapeDtypeStruct((rows, cols), BF16))
    out = pl.pallas_call(
        functools.partial(_proj_mlp_kernel, n_cast=len(to_bf16)),
        grid=(steps,),
        in_specs=[tile,
                  pl.BlockSpec((D_MODEL, D_MODEL), lambda i: (0, 0), pipeline_mode=resident),
                  gain, tile, gain, gain,
                  pl.BlockSpec((D_MODEL, D_FF), lambda i: (0, 0), pipeline_mode=resident),
                  pl.BlockSpec((D_FF, D_MODEL), lambda i: (0, 0), pipeline_mode=resident)]
        + cast_in,
        out_specs=[tile] + cast_out,
        out_shape=[jax.ShapeDtypeStruct((n, D_MODEL), F32)] + cast_shape,
        compiler_params=_params(("arbitrary",)),
        name="attn_proj_mlp",
    )(a2d, w_o, g_mix.reshape(1, D_MODEL), x2d, g_pre.reshape(1, D_MODEL),
      g_post.reshape(1, D_MODEL), w_up, w_down, *[w for w, _ in to_bf16])
    return out[0], out[1:]


CONV_TS = 512
HALO = 16
CONV_ROWS = CONV_TS + 2 * HALO
CONV_RB = 128
CONV_NC = 256


N_TILES = BATCH * SEQ // CONV_TS
TILES_PER_SEQ = SEQ // CONV_TS
assert CONV_TS == MLP_TM


def _conv_stages(xm_ref, xt_ref, xb_ref, first, last, gpre_ref, w1_ref, b1_ref, wdw_ref, bdw_ref,
                 lng_ref, lnb_ref, w2_ref, b2_ref, gpost_ref, o_ref, h_ref, u_ref, c_ref):
    gpre = gpre_ref[...]
    h_ref[0:HALO, :] = _rms(xt_ref[0], gpre).astype(BF16)
    h_ref[HALO:HALO + CONV_TS, :] = _rms(xm_ref[0], gpre).astype(BF16)
    h_ref[HALO + CONV_TS:, :] = _rms(xb_ref[0], gpre).astype(BF16)
    yield

    h = h_ref[...]
    half = CONV_RB // 2
    for nc in range(D_MODEL // CONV_NC):
        cols = slice(nc * CONV_NC, (nc + 1) * CONV_NC)
        gcols = slice(D_MODEL + nc * CONV_NC, D_MODEL + (nc + 1) * CONV_NC)
        a = jnp.dot(h, w1_ref[:, cols], preferred_element_type=F32) + b1_ref[:, cols]
        gate = jnp.dot(h, w1_ref[:, gcols], preferred_element_type=F32) + b1_ref[:, gcols]
        u = a * jax.nn.sigmoid(gate)
        for k in range(CONV_NC // LANES):
            lc = nc * CONV_NC // LANES + k
            lanes = slice(lc * LANES, (lc + 1) * LANES)
            uk = u[:, k * LANES:(k + 1) * LANES]
            u_ref[lc, 0:HALO, :] = jnp.where(first, 0.0, uk[0:HALO])
            u_ref[lc, HALO:HALO + CONV_TS, :] = uk[HALO:HALO + CONV_TS]
            u_ref[lc, HALO + CONV_TS:, :] = jnp.where(last, 0.0, uk[HALO + CONV_TS:])
        yield
        for k in range(CONV_NC // LANES):
            lc = nc * CONV_NC // LANES + k
            lanes = slice(lc * LANES, (lc + 1) * LANES)
            for r0 in range(0, CONV_TS, CONV_RB):
                for phase in range(2):
                    acc = bdw_ref[:, lanes]
                    for t in range(CONV_WIDTH):
                        start = r0 + phase + t + HALO - CONV_WIDTH // 2
                        acc = acc + (u_ref[lc, pl.ds(start, half, stride=2), :]
                                     * wdw_ref[t:t + 1, lanes])
                    c_ref[lc, pl.ds(r0 + phase, half, stride=2), :] = acc
            yield

    v = jnp.concatenate([c_ref[lc] for lc in range(D_MODEL // LANES)], axis=1)
    mu = jnp.mean(v, axis=-1, keepdims=True)
    var = jnp.mean(jnp.square(v - mu), axis=-1, keepdims=True)
    y = (v - mu) * lax.rsqrt(var + LN_EPS) * lng_ref[...] + lnb_ref[...]
    y = (y * jax.nn.sigmoid(y)).astype(BF16)
    yield
    z = jnp.dot(y, w2_ref[...], preferred_element_type=F32) + b2_ref[...]
    o_ref[...] = xm_ref[0] + _rms(z, gpost_ref[...])


def _conv_mlp_kernel(xm_ref, xt_ref, xb_ref, gpre_ref, w1_ref, b1_ref, wdw_ref, bdw_ref, lng_ref,
                     lnb_ref, w2_ref, b2_ref, gpost_ref, g2pre_ref, g2post_ref, wup_ref, wdn_ref,
                     o_ref, h_ref, u_ref, c_ref, x_ref, xnew_ref):
    s = pl.program_id(0)
    j = s % TILES_PER_SEQ

    def conv():
        return _conv_stages(xm_ref, xt_ref, xb_ref, j == 0, j == TILES_PER_SEQ - 1, gpre_ref,
                            w1_ref, b1_ref, wdw_ref, bdw_ref, lng_ref, lnb_ref, w2_ref, b2_ref,
                            gpost_ref, xnew_ref, h_ref, u_ref, c_ref)

    def mlp():
        return _mlp_stages(lambda: x_ref[...], g2pre_ref, g2post_ref, wup_ref, wdn_ref, o_ref)

    order = "ccccc" + "mm" + "cmccm" * 2 + "cmcm" * 2

    @pl.when(s == 0)
    def _():
        _interleave("c" * order.count("c"), c=conv())
        x_ref[...] = xnew_ref[...]

    @pl.when((s > 0) & (s < N_TILES))
    def _():
        _interleave(order, c=conv(), m=mlp())
        x_ref[...] = xnew_ref[...]

    @pl.when(s == N_TILES)
    def _():
        _interleave("m" * order.count("m"), m=mlp())


def _conv_mlp_layer(x, g_pre, w1, b1, wdw, bdw, lng, lnb, w2, b2, g_post, g2_pre, g2_post,
                    w_up, w_down):
    row = lambda v: v.reshape(1, -1)
    resident = pl.Buffered(1)
    const = lambda shape, **kw: pl.BlockSpec(shape, lambda s: (0,) * len(shape), **kw)
    halo_blocks = CONV_TS // HALO

    def tile(s):
        t = jnp.minimum(s, N_TILES - 1)
        return t // TILES_PER_SEQ, t % TILES_PER_SEQ

    def main_map(s):
        b, j = tile(s)
        return b, j, 0

    def top_map(s):
        b, j = tile(s)
        return b, jnp.maximum(j * halo_blocks - 1, 0), 0

    def bot_map(s):
        b, j = tile(s)
        return b, jnp.minimum((j + 1) * halo_blocks, SEQ // HALO - 1), 0

    gain = const((1, D_MODEL))
    return pl.pallas_call(
        _conv_mlp_kernel,
        grid=(N_TILES + 1,),
        in_specs=[pl.BlockSpec((1, CONV_TS, D_MODEL), main_map),
                  pl.BlockSpec((1, HALO, D_MODEL), top_map),
                  pl.BlockSpec((1, HALO, D_MODEL), bot_map),
                  gain,
                  const((D_MODEL, 2 * D_MODEL), pipeline_mode=resident),
                  const((1, 2 * D_MODEL)),
                  const((CONV_WIDTH, D_MODEL)),
                  gain, gain, gain,
                  const((D_MODEL, D_MODEL), pipeline_mode=resident),
                  gain, gain, gain, gain,
                  const((D_MODEL, D_FF), pipeline_mode=resident),
                  const((D_FF, D_MODEL), pipeline_mode=resident)],
        out_specs=pl.BlockSpec((CONV_TS, D_MODEL), lambda s: (jnp.maximum(s - 1, 0), 0)),
        out_shape=jax.ShapeDtypeStruct((BATCH * SEQ, D_MODEL), F32),
        scratch_shapes=[pltpu.VMEM((CONV_ROWS, D_MODEL), BF16),
                        pltpu.VMEM((D_MODEL // LANES, CONV_ROWS, LANES), F32),
                        pltpu.VMEM((D_MODEL // LANES, CONV_TS, LANES), F32),
                        pltpu.VMEM((CONV_TS, D_MODEL), F32),
                        pltpu.VMEM((CONV_TS, D_MODEL), F32)],
        compiler_params=_params(("arbitrary",)),
        name="conformer_conv_mlp",
    )(x, x, x, row(g_pre), w1, row(b1), wdw, row(bdw), row(lng), row(lnb), w2, row(b2),
      row(g_post), row(g2_pre), row(g2_post), w_up, w_down)


def kernel(x, rel_bias, norm_mix_pre, norm_mix_post, norm_mlp_pre, norm_mlp_post, attn_w_qkv,
           attn_w_o, conv_w_pw1, conv_b_pw1, conv_w_dw, conv_b_dw, conv_ln_g, conv_ln_b,
           conv_w_pw2, conv_b_pw2, mlp_w_up, mlp_w_down):
    n_tok = BATCH * SEQ
    x2d = x.reshape(n_tok, D_MODEL)

    h = _prenorm(x, norm_mix_pre[0])
    w_qkv = attn_w_qkv.reshape(D_MODEL, N_GROUPS * GROUP_WIDTH)
    qkv, (w_o, w_up0, w_down0) = _qkv_proj(
        h, w_qkv, to_bf16=[(attn_w_o, 0), (mlp_w_up, 0), (mlp_w_down, 0)])
    a = _attention(qkv, rel_bias)
    x2d, (w_pw1, w_pw2, w_up1, w_down1) = _proj_mlp(
        x2d, a.reshape(n_tok, D_MODEL), w_o, norm_mix_post[0], norm_mlp_pre[0],
        norm_mlp_post[0], w_up0, w_down0,
        to_bf16=[(conv_w_pw1, 0), (conv_w_pw2, 0), (mlp_w_up, 1), (mlp_w_down, 1)])

    x2d = _conv_mlp_layer(x2d.reshape(BATCH, SEQ, D_MODEL), norm_mix_pre[1], w_pw1, conv_b_pw1[0],
                          conv_w_dw[0], conv_b_dw[0], conv_ln_g[0], conv_ln_b[0], w_pw2,
                          conv_b_pw2[0], norm_mix_post[1], norm_mlp_pre[1], norm_mlp_post[1],
                          w_up1, w_down1)
    return x2d.reshape(BATCH, SEQ, D_MODEL)
```

```python
import functools
import math

import jax
import jax.numpy as jnp
from jax import lax
from jax.experimental import pallas as pl
from jax.experimental.pallas import tpu as pltpu

D_MODEL = 1024
BATCH = 8
SEQ = 2048
HEAD_DIM = 64
N_HEADS = 16
DILATIONS = (1, 4, 16)
N_SIDE = 64
N_GROUPS = 3
GROUP_WIDTH = 3 * N_HEADS * HEAD_DIM
N_BUCKETS = 32
MAX_DISTANCE = 1024
CONV_WIDTH = 31
D_FF = 4 * D_MODEL
RMS_EPS = 1e-6
LN_EPS = 1e-5
NEG_INF = -1e30

F32 = jnp.float32
BF16 = jnp.bfloat16

LANES = 128
Q_TILE = 128
K_TILE = Q_TILE + 2 * N_SIDE
VMEM_LIMIT = 56 * 1024 * 1024


def _rms(x, g):
    return x * lax.rsqrt(jnp.mean(x * x, axis=-1, keepdims=True) + RMS_EPS) * g


def _params(semantics):
    return pltpu.CompilerParams(dimension_semantics=semantics, vmem_limit_bytes=VMEM_LIMIT)


NORM_ROWS = 256


def _prenorm_kernel(x_ref, g_ref, *refs):
    out_ref, slab_ref, regroup_ref = refs
    assert DILATIONS == (1, 4, 16)

    def chunk(i, carry):
        rows = pl.ds(pl.multiple_of(i * NORM_ROWS, NORM_ROWS), NORM_ROWS)
        hn = _rms(x_ref[0, rows, :], g_ref[...])
        for k in range(D_MODEL // LANES):
            slab_ref[k, rows, :] = hn[:, k * LANES:(k + 1) * LANES]
        out_ref[0, 0, rows, :] = hn.astype(BF16)
        return carry

    lax.fori_loop(0, SEQ // NORM_ROWS, chunk, 0)
    r1 = DILATIONS[1]
    L1, L2 = SEQ // r1, SEQ // DILATIONS[2]
    for k in range(D_MODEL // LANES):
        lanes = slice(k * LANES, (k + 1) * LANES)
        for c in range(r1):
            v = slab_ref[k, pl.ds(c, L1, stride=r1), :]
            out_ref[1, 0, c * L1:(c + 1) * L1, lanes] = v.astype(BF16)
            regroup_ref[c * L1:(c + 1) * L1, :] = v
        for c in range(DILATIONS[2]):
            start = (c % r1) * L1 + c // r1
            out_ref[2, 0, c * L2:(c + 1) * L2, lanes] = (
                regroup_ref[pl.ds(start, L2, stride=r1), :].astype(BF16))


def _prenorm(x, g):
    return pl.pallas_call(
        _prenorm_kernel,
        grid=(BATCH,),
        in_specs=[pl.BlockSpec((1, SEQ, D_MODEL), lambda b: (b, 0, 0)),
                  pl.BlockSpec((1, D_MODEL), lambda b: (0, 0))],
        out_specs=pl.BlockSpec((N_GROUPS, 1, SEQ, D_MODEL), lambda b: (0, b, 0, 0)),
        out_shape=jax.ShapeDtypeStruct((N_GROUPS, BATCH, SEQ, D_MODEL), BF16),
        scratch_shapes=[pltpu.VMEM((D_MODEL // LANES, SEQ, LANES), F32),
                        pltpu.VMEM((SEQ, LANES), F32)],
        compiler_params=_params(("arbitrary",)),
        name="prenorm",
    )(x, g.reshape(1, D_MODEL))


QKV_TN = 1536
QKV_TM = 512
Q_WIDTH = N_HEADS * HEAD_DIM
assert QKV_TN >= Q_WIDTH
LOG2E = math.log2(math.e)
Q_SCALE = HEAD_DIM ** -0.5 * LOG2E


def _qkv_kernel(h_ref, w_ref, *refs, n_cast):
    f32_refs, o_ref, bf16_refs, wb_ref = (refs[:n_cast], refs[n_cast], refs[n_cast + 1:-1],
                                          refs[-1])
    for src_ref, dst_ref in zip(f32_refs, bf16_refs):
        dst_ref[...] = src_ref[...].astype(BF16)

    @pl.when(pl.program_id(1) == 0)
    def _():
        wb_ref[...] = w_ref[...].astype(BF16)

    col = lax.broadcasted_iota(jnp.int32, (1, QKV_TN), 1)
    first_tile = pl.program_id(0) % (GROUP_WIDTH // QKV_TN) == 0
    scale = jnp.where(first_tile & (col < Q_WIDTH), Q_SCALE, 1.0)
    for m in range(SEQ // QKV_TM):
        rows = slice(m * QKV_TM, (m + 1) * QKV_TM)
        res = jnp.dot(h_ref[0, rows, :], wb_ref[...], preferred_element_type=F32)
        o_ref[0, rows, :] = (res * scale).astype(BF16)


QKV_CAST_STEPS = 32


def _qkv_proj(h, w_qkv, to_bf16):
    nj = GROUP_WIDTH // QKV_TN
    assert N_GROUPS * nj * BATCH >= QKV_CAST_STEPS
    cast_step = lambda i, b: jnp.minimum(i * BATCH + b, QKV_CAST_STEPS - 1)
    cast_in, cast_out, cast_shape = [], [], []
    for w, layer in to_bf16:
        _, rows, cols = w.shape
        blk = (rows // QKV_CAST_STEPS, cols)
        cast_in.append(pl.BlockSpec((None,) + blk,
                                    lambda i, b, layer=layer: (layer, cast_step(i, b), 0)))
        cast_out.append(pl.BlockSpec(blk, lambda i, b: (cast_step(i, b), 0)))
        cast_shape.append(jax.ShapeDtypeStruct((rows, cols), BF16))
    out = pl.pallas_call(
        functools.partial(_qkv_kernel, n_cast=len(to_bf16)),
        grid=(N_GROUPS * nj, BATCH),
        in_specs=[pl.BlockSpec((None, 1, SEQ, D_MODEL), lambda i, b: (i // nj, b, 0, 0)),
                  pl.BlockSpec((D_MODEL, QKV_TN), lambda i, b: (0, i))] + cast_in,
        out_specs=[pl.BlockSpec((None, 1, SEQ, QKV_TN), lambda i, b: (i // nj, b, 0, i % nj))]
        + cast_out,
        out_shape=[jax.ShapeDtypeStruct((N_GROUPS, BATCH, SEQ, GROUP_WIDTH), BF16)] + cast_shape,
        scratch_shapes=[pltpu.VMEM((D_MODEL, QKV_TN), BF16)],
        compiler_params=_params(("arbitrary", "arbitrary")),
        name="qkv_proj",
    )(h, w_qkv, *[w for w, _ in to_bf16])
    return out[0], out[1:]


HEADS_PER_STEP = 4
PAIRS_PER_STEP = HEADS_PER_STEP // 2
STEP_LANES = HEADS_PER_STEP * HEAD_DIM
N_Q_TILES = SEQ // Q_TILE
N_KINDS = 3
PADDED_PITCH = 24
GROUP_ROWS = tuple(SEQ // 16 * PADDED_PITCH if r % 16 == 0 else SEQ for r in DILATIONS)
GROUP_BASE = tuple(sum(GROUP_ROWS[:g]) for g in range(N_GROUPS))


def _bias_row(rel_ref, g, head):
    r = DILATIONS[g]
    nb = N_BUCKETS // 2
    max_exact = nb // 2
    delta = lax.broadcasted_iota(jnp.int32, (8, K_TILE), 1) - N_SIDE
    rel = delta * r
    n = jnp.abs(rel)
    nf = jnp.maximum(n, 1).astype(F32)
    large = max_exact + (jnp.log(nf / max_exact) / math.log(MAX_DISTANCE / max_exact)
                         * (nb - max_exact)).astype(jnp.int32)
    large = jnp.minimum(large, nb - 1)
    bucket = jnp.where(rel > 0, nb, 0) + jnp.where(n < max_exact, n, large)
    u = jnp.zeros((8, K_TILE), F32)
    for b in range(N_BUCKETS):
        u = jnp.where(bucket == b, rel_ref[b, g * N_HEADS + head], u)
    return jnp.where(jnp.abs(delta) <= N_SIDE, u * LOG2E, NEG_INF)


def _build_bias_tiles(rel_ref, bias_ref, head0):
    qi = lax.broadcasted_iota(jnp.int32, (Q_TILE, K_TILE), 0)
    kj = lax.broadcasted_iota(jnp.int32, (Q_TILE, K_TILE), 1)
    for g in range(N_GROUPS):
        for h in range(HEADS_PER_STEP):
            u = _bias_row(rel_ref, g, head0 + h)
            ub = jnp.broadcast_to(u[0:1, :], (Q_TILE, K_TILE))
            for kind in range(N_KINDS):
                shift = (N_SIDE * kind - N_SIDE) % K_TILE
                t = pltpu.roll(ub, shift, 1, stride=1, stride_axis=0)
                in_band = jnp.abs(kj - qi - N_SIDE * kind) <= N_SIDE
                bias_ref[g, h, kind] = jnp.where(in_band, t, NEG_INF)


def _token_rows(g, c, l0):
    r = DILATIONS[g]
    if r == 1:
        return pl.ds(GROUP_BASE[g] + l0, Q_TILE)
    pitch = PADDED_PITCH if r % 16 == 0 else r
    return pl.ds(GROUP_BASE[g] + l0 * pitch + c, Q_TILE, stride=pitch)


def _load_tokens(ref, pair, g, t):
    r = DILATIONS[g]
    if r % 16:
        return ref[pair, pl.ds(pl.multiple_of(GROUP_BASE[g] + t * Q_TILE, Q_TILE), Q_TILE), :]
    n = Q_TILE // r
    base = pl.multiple_of(GROUP_BASE[g] + t * n * PADDED_PITCH, 8)
    return jnp.concatenate([ref[pair, pl.ds(base + i * PADDED_PITCH, r), :] for i in range(n)],
                           axis=0)


def _split_heads(q):
    first = lax.broadcasted_iota(jnp.int32, (1, LANES), 1) < HEAD_DIM
    zero = jnp.zeros_like(q)
    return jnp.concatenate([jnp.where(first, q, zero), jnp.where(first, zero, q)], axis=0)


def _softmax_pv(s, v):
    first = lax.broadcasted_iota(jnp.int32, (1, LANES), 1) < HEAD_DIM
    m = jnp.max(s, axis=-1, keepdims=True)
    p = jnp.exp2(s - m).astype(BF16)
    v1 = jnp.concatenate([v, jnp.ones((v.shape[0], LANES), BF16)], axis=1)
    o2 = jnp.dot(p, v1, preferred_element_type=F32)
    acc = jnp.where(first, o2[:Q_TILE, :LANES], o2[Q_TILE:, :LANES])
    l = jnp.where(first, o2[:Q_TILE, LANES:], o2[Q_TILE:, LANES:])
    m_t = jnp.where(first, m[:Q_TILE], m[Q_TILE:])
    return acc, m_t, l


_NT = (((1,), (1,)), ((), ()))


def _attn_kernel(rel_ref, q0_ref, k0_ref, v0_ref, q1_ref, k1_ref, v1_ref, q2_ref, k2_ref, v2_ref,
                 o_ref, bias_ref, acc_ref, m_ref, l_ref):
    hp = pl.program_id(0)

    @pl.when(pl.program_id(1) == 0)
    def _():
        _build_bias_tiles(rel_ref, bias_ref, hp * HEADS_PER_STEP)

    qkv = ((q0_ref, k0_ref, v0_ref), (q1_ref, k1_ref, v1_ref), (q2_ref, k2_ref, v2_ref))
    for pair in range(PAIRS_PER_STEP):
        lanes = slice(pair * LANES, (pair + 1) * LANES)
        for g in range(N_GROUPS):
            r = DILATIONS[g]
            tiles_per_seq = SEQ // r // Q_TILE
            q_ref, k_ref, v_ref = qkv[g]

            def put(t, res, g=g, tiles_per_seq=tiles_per_seq, pair=pair):
                rows = _token_rows(g, t // tiles_per_seq, (t % tiles_per_seq) * Q_TILE)
                for ref, val in zip((acc_ref, m_ref, l_ref), res):
                    ref[pair, rows, :] = val

            if tiles_per_seq > 1:
                for t in range(N_Q_TILES):
                    tl = t % tiles_per_seq
                    kind = 0 if tl == 0 else (2 if tl == tiles_per_seq - 1 else 1)
                    q0 = t * Q_TILE
                    k0 = q0 - N_SIDE * kind
                    q2 = _split_heads(q_ref[0, q0:q0 + Q_TILE, lanes])
                    s = lax.dot_general(q2, k_ref[0, k0:k0 + K_TILE, lanes], _NT,
                                        preferred_element_type=F32)
                    s = s + jnp.concatenate([bias_ref[g, 2 * pair, kind],
                                             bias_ref[g, 2 * pair + 1, kind]], axis=0)
                    put(t, _softmax_pv(s, v_ref[0, k0:k0 + K_TILE, lanes]))
            else:
                bias = jnp.concatenate([bias_ref[g, 2 * pair, 0, :, :Q_TILE],
                                        bias_ref[g, 2 * pair + 1, 0, :, :Q_TILE]], axis=0)
                for t in range(0, N_Q_TILES, 2):
                    q0 = t * Q_TILE
                    q = q_ref[0, q0:q0 + 2 * Q_TILE, lanes]
                    q2 = jnp.concatenate([_split_heads(q[:Q_TILE]), _split_heads(q[Q_TILE:])], axis=0)
                    s = lax.dot_general(q2, k_ref[0, q0:q0 + 2 * Q_TILE, lanes], _NT,
                                        preferred_element_type=F32)
                    for i in range(2):
                        si = s[2 * i * Q_TILE:2 * (i + 1) * Q_TILE, i * Q_TILE:(i + 1) * Q_TILE]
                        v = v_ref[0, q0 + i * Q_TILE:q0 + (i + 1) * Q_TILE, lanes]
                        put(t + i, _softmax_pv(si + bias, v))

    for pair in range(PAIRS_PER_STEP):
        lanes = slice(pair * LANES, (pair + 1) * LANES)

        def merge_body(t, carry, lanes=lanes, pair=pair):
            m = [_load_tokens(m_ref, pair, g, t) for g in range(N_GROUPS)]
            top = jnp.maximum(jnp.maximum(m[0], m[1]), m[2])
            w = [jnp.exp2(x - top) for x in m]
            num = sum(w[g] * _load_tokens(acc_ref, pair, g, t) for g in range(N_GROUPS))
            den = sum(w[g] * _load_tokens(l_ref, pair, g, t) for g in range(N_GROUPS))
            rows = pl.ds(pl.multiple_of(t * Q_TILE, Q_TILE), Q_TILE)
            o_ref[0, rows, lanes] = (num / den).astype(BF16)
            return carry

        lax.fori_loop(0, N_Q_TILES, merge_body, 0, unroll=4)


def _attention(qkv, rel_bias):
    n_hp = N_HEADS // HEADS_PER_STEP
    blk = (1, SEQ, STEP_LANES)
    grp_blk = (None,) + blk
    specs = [pl.BlockSpec(memory_space=pltpu.SMEM)]
    args = [rel_bias]
    for g in range(N_GROUPS):
        for part in range(3):
            specs.append(pl.BlockSpec(grp_blk,
                                      lambda hp, b, g=g, part=part: (g, b, 0, part * n_hp + hp)))
            args.append(qkv)
    return pl.pallas_call(
        _attn_kernel,
        grid=(n_hp, BATCH),
        in_specs=specs,
        out_specs=pl.BlockSpec(blk, lambda hp, b: (b, 0, hp)),
        out_shape=jax.ShapeDtypeStruct((BATCH, SEQ, D_MODEL), BF16),
        scratch_shapes=[pltpu.VMEM((N_GROUPS, HEADS_PER_STEP, N_KINDS, Q_TILE, K_TILE), F32),
                        pltpu.VMEM((PAIRS_PER_STEP, sum(GROUP_ROWS), LANES), F32),
                        pltpu.VMEM((PAIRS_PER_STEP, sum(GROUP_ROWS), LANES), F32),
                        pltpu.VMEM((PAIRS_PER_STEP, sum(GROUP_ROWS), LANES), F32)],
        compiler_params=_params(("arbitrary", "arbitrary")),
        name="dilated_attention",
    )(*args)


MLP_TM = 512
MLP_FC = 1024
MLP_STAGES = 2 + 2 * (D_FF // MLP_FC)


_DONE = object()


def _interleave(order, **stage_generators):
    for key in order:
        next(stage_generators[key], _DONE)
    for gen in stage_generators.values():
        assert next(gen, _DONE) is _DONE, "order does not cover every stage"


def _mlp_stages(read_x, gpre_ref, gpost_ref, wup_ref, wdn_ref, o_ref):
    h = _rms(read_x(), gpre_ref[...]).astype(BF16)
    acc = jnp.zeros(h.shape, F32)
    yield
    for c in range(D_FF // MLP_FC):
        cols = slice(c * MLP_FC, (c + 1) * MLP_FC)
        u = jnp.dot(h, wup_ref[:, cols], preferred_element_type=F32)
        u = jnp.square(jnp.maximum(u, 0.0)).astype(BF16)
        yield
        acc = acc + jnp.dot(u, wdn_ref[cols, :], preferred_element_type=F32)
        yield
    o_ref[...] = read_x() + _rms(acc, gpost_ref[...])


def _proj_mlp_kernel(a_ref, wo_ref, gmix_ref, x_ref, gpre_ref, gpost_ref, wup_ref, wdn_ref,
                     *refs, n_cast):
    f32_refs, o_ref, bf16_refs = refs[:n_cast], refs[n_cast], refs[n_cast + 1:]
    for src_ref, dst_ref in zip(f32_refs, bf16_refs):
        dst_ref[...] = src_ref[...].astype(BF16)

    def half(rows):
        m = jnp.dot(a_ref[rows, :], wo_ref[...], preferred_element_type=F32)
        x = x_ref[rows, :] + _rms(m, gmix_ref[...])
        yield
        yield from _mlp_stages(lambda: x, gpre_ref, gpost_ref, wup_ref, wdn_ref,
                               o_ref.at[rows, :])

    n = MLP_TM // 2
    _interleave("a" + "ab" * MLP_STAGES + "b", a=half(slice(0, n)), b=half(slice(n, 2 * n)))


def _proj_mlp(x2d, a2d, w_o, g_mix, g_pre, g_post, w_up, w_down, to_bf16):
    n = x2d.shape[0]
    steps = n // MLP_TM
    resident = pl.Buffered(1)
    tile = pl.BlockSpec((MLP_TM, D_MODEL), lambda i: (i, 0))
    gain = pl.BlockSpec((1, D_MODEL), lambda i: (0, 0))
    cast_in, cast_out, cast_shape = [], [], []
    for w, layer in to_bf16:
        _, rows, cols = w.shape
        cast_in.append(pl.BlockSpec((None, rows // steps, cols), lambda i, layer=layer: (layer, i, 0)))
        cast_out.append(pl.BlockSpec((rows // steps, cols), lambda i: (i, 0)))
        cast_shape.append(jax.ShapeDtypeStruct((rows, cols), BF16))
    out = pl.pallas_call(
        functools.partial(_proj_mlp_kernel, n_cast=len(to_bf16)),
        grid=(steps,),
        in_specs=[tile,
                  pl.BlockSpec((D_MODEL, D_MODEL), lambda i: (0, 0), pipeline_mode=resident),
                  gain, tile, gain, gain,
                  pl.BlockSpec((D_MODEL, D_FF), lambda i: (0, 0), pipeline_mode=resident),
                  pl.BlockSpec((D_FF, D_MODEL), lambda i: (0, 0), pipeline_mode=resident)]
        + cast_in,
        out_specs=[tile] + cast_out,
        out_shape=[jax.ShapeDtypeStruct((n, D_MODEL), F32)] + cast_shape,
        compiler_params=_params(("arbitrary",)),
        name="attn_proj_mlp",
    )(a2d, w_o, g_mix.reshape(1, D_MODEL), x2d, g_pre.reshape(1, D_MODEL),
      g_post.reshape(1, D_MODEL), w_up, w_down, *[w for w, _ in to_bf16])
    return out[0], out[1:]


CONV_TS = 512
HALO = 16
CONV_ROWS = CONV_TS + 2 * HALO
CONV_RB = 128
CONV_NC = 256


N_TILES = BATCH * SEQ // CONV_TS
TILES_PER_SEQ = SEQ // CONV_TS
assert CONV_TS == MLP_TM


def _conv_stages(xm_ref, xt_ref, xb_ref, first, last, gpre_ref, w1_ref, b1_ref, wdw_ref, bdw_ref,
                 lng_ref, lnb_ref, w2_ref, b2_ref, gpost_ref, o_ref, h_ref, u_ref, c_ref):
    gpre = gpre_ref[...]
    h_ref[0:HALO, :] = _rms(xt_ref[0], gpre).astype(BF16)
    h_ref[HALO:HALO + CONV_TS, :] = _rms(xm_ref[0], gpre).astype(BF16)
    h_ref[HALO + CONV_TS:, :] = _rms(xb_ref[0], gpre).astype(BF16)
    yield

    h = h_ref[...]
    half = CONV_RB // 2
    for nc in range(D_MODEL // CONV_NC):
        cols = slice(nc * CONV_NC, (nc + 1) * CONV_NC)
        gcols = slice(D_MODEL + nc * CONV_NC, D_MODEL + (nc + 1) * CONV_NC)
        a = jnp.dot(h, w1_ref[:, cols], preferred_element_type=F32) + b1_ref[:, cols]
        gate = jnp.dot(h, w1_ref[:, gcols], preferred_element_type=F32) + b1_ref[:, gcols]
        u = a * jax.nn.sigmoid(gate)
        for k in range(CONV_NC // LANES):
            lc = nc * CONV_NC // LANES + k
            lanes = slice(lc * LANES, (lc + 1) * LANES)
            uk = u[:, k * LANES:(k + 1) * LANES]
            u_ref[lc, 0:HALO, :] = jnp.where(first, 0.0, uk[0:HALO])
            u_ref[lc, HALO:HALO + CONV_TS, :] = uk[HALO:HALO + CONV_TS]
            u_ref[lc, HALO + CONV_TS:, :] = jnp.where(last, 0.0, uk[HALO + CONV_TS:])
        yield
        for k in range(CONV_NC // LANES):
            lc = nc * CONV_NC // LANES + k
            lanes = slice(lc * LANES, (lc + 1) * LANES)
            for r0 in range(0, CONV_TS, CONV_RB):
                for phase in range(2):
                    acc = bdw_ref[:, lanes]
                    for t in range(CONV_WIDTH):
                        start = r0 + phase + t + HALO - CONV_WIDTH // 2
                        acc = acc + (u_ref[lc, pl.ds(start, half, stride=2), :]
                                     * wdw_ref[t:t + 1, lanes])
                    c_ref[lc, pl.ds(r0 + phase, half, stride=2), :] = acc
            yield

    v = jnp.concatenate([c_ref[lc] for lc in range(D_MODEL // LANES)], axis=1)
    mu = jnp.mean(v, axis=-1, keepdims=True)
    var = jnp.mean(jnp.square(v - mu), axis=-1, keepdims=True)
    y = (v - mu) * lax.rsqrt(var + LN_EPS) * lng_ref[...] + lnb_ref[...]
    y = (y * jax.nn.sigmoid(y)).astype(BF16)
    yield
    z = jnp.dot(y, w2_ref[...], preferred_element_type=F32) + b2_ref[...]
    o_ref[...] = xm_ref[0] + _rms(z, gpost_ref[...])


def _conv_mlp_kernel(xm_ref, xt_ref, xb_ref, gpre_ref, w1_ref, b1_ref, wdw_ref, bdw_ref, lng_ref,
                     lnb_ref, w2_ref, b2_ref, gpost_ref, g2pre_ref, g2post_ref, wup_ref, wdn_ref,
                     o_ref, h_ref, u_ref, c_ref, x_ref):
    s = pl.program_id(0)
    j = s % TILES_PER_SEQ

    def conv():
        return _conv_stages(xm_ref, xt_ref, xb_ref, j == 0, j == TILES_PER_SEQ - 1, gpre_ref,
                            w1_ref, b1_ref, wdw_ref, bdw_ref, lng_ref, lnb_ref, w2_ref, b2_ref,
                            gpost_ref, x_ref.at[s % 2], h_ref, u_ref, c_ref)

    def mlp():
        return _mlp_stages(lambda: x_ref[(s + 1) % 2], g2pre_ref, g2post_ref, wup_ref, wdn_ref,
                           o_ref)

    order = "ccccc" + "mm" + "cmccm" * 2 + "cmcm" * 2

    @pl.when(s == 0)
    def _():
        _interleave("c" * order.count("c"), c=conv())

    @pl.when((s > 0) & (s < N_TILES))
    def _():
        _interleave(order, c=conv(), m=mlp())

    @pl.when(s == N_TILES)
    def _():
        _interleave("m" * order.count("m"), m=mlp())


def _conv_mlp_layer(x, g_pre, w1, b1, wdw, bdw, lng, lnb, w2, b2, g_post, g2_pre, g2_post,
                    w_up, w_down):
    row = lambda v: v.reshape(1, -1)
    resident = pl.Buffered(1)
    const = lambda shape, **kw: pl.BlockSpec(shape, lambda s: (0,) * len(shape), **kw)
    halo_blocks = CONV_TS // HALO

    def tile(s):
        t = jnp.minimum(s, N_TILES - 1)
        return t // TILES_PER_SEQ, t % TILES_PER_SEQ

    def main_map(s):
        b, j = tile(s)
        return b, j, 0

    def top_map(s):
        b, j = tile(s)
        return b, jnp.maximum(j * halo_blocks - 1, 0), 0

    def bot_map(s):
        b, j = tile(s)
        return b, jnp.minimum((j + 1) * halo_blocks, SEQ // HALO - 1), 0

    gain = const((1, D_MODEL))
    return pl.pallas_call(
        _conv_mlp_kernel,
        grid=(N_TILES + 1,),
        in_specs=[pl.BlockSpec((1, CONV_TS, D_MODEL), main_map),
                  pl.BlockSpec((1, HALO, D_MODEL), top_map),
                  pl.BlockSpec((1, HALO, D_MODEL), bot_map),
                  gain,
                  const((D_MODEL, 2 * D_MODEL), pipeline_mode=resident),
                  const((1, 2 * D_MODEL)),
                  const((CONV_WIDTH, D_MODEL)),
                  gain, gain, gain,
                  const((D_MODEL, D_MODEL), pipeline_mode=resident),
                  gain, gain, gain, gain,
                  const((D_MODEL, D_FF), pipeline_mode=resident),
                  const((D_FF, D_MODEL), pipeline_mode=resident)],
        out_specs=pl.BlockSpec((CONV_TS, D_MODEL), lambda s: (jnp.maximum(s - 1, 0), 0)),
        out_shape=jax.ShapeDtypeStruct((BATCH * SEQ, D_MODEL), F32),
        scratch_shapes=[pltpu.VMEM((CONV_ROWS, D_MODEL), BF16),
                        pltpu.VMEM((D_MODEL // LANES, CONV_ROWS, LANES), F32),
                        pltpu.VMEM((D_MODEL // LANES, CONV_TS, LANES), F32),
                        pltpu.VMEM((2, CONV_TS, D_MODEL), F32)],
        compiler_params=_params(("arbitrary",)),
        name="conformer_conv_mlp",
    )(x, x, x, row(g_pre), w1, row(b1), wdw, row(bdw), row(lng), row(lnb), w2, row(b2),
      row(g_post), row(g2_pre), row(g2_post), w_up, w_down)


def kernel(x, rel_bias, norm_mix_pre, norm_mix_post, norm_mlp_pre, norm_mlp_post, attn_w_qkv,
           attn_w_o, conv_w_pw1, conv_b_pw1, conv_w_dw, conv_b_dw, conv_ln_g, conv_ln_b,
           conv_w_pw2, conv_b_pw2, mlp_w_up, mlp_w_down):
    n_tok = BATCH * SEQ
    x2d = x.reshape(n_tok, D_MODEL)

    h = _prenorm(x, norm_mix_pre[0])
    w_qkv = attn_w_qkv.reshape(D_MODEL, N_GROUPS * GROUP_WIDTH)
    qkv, (w_o, w_up0, w_down0) = _qkv_proj(
        h, w_qkv, to_bf16=[(attn_w_o, 0), (mlp_w_up, 0), (mlp_w_down, 0)])
    a = _attention(qkv, rel_bias)
    x2d, (w_pw1, w_pw2, w_up1, w_down1) = _proj_mlp(
        x2d, a.reshape(n_tok, D_MODEL), w_o, norm_mix_post[0], norm_mlp_pre[0],
        norm_mlp_post[0], w_up0, w_down0,
        to_bf16=[(conv_w_pw1, 0), (conv_w_pw2, 0), (mlp_w_up, 1), (mlp_w_down, 1)])

    x2d = _conv_mlp_layer(x2d.reshape(BATCH, SEQ, D_MODEL), norm_mix_pre[1], w_pw1, conv_b_pw1[0],
                          conv_w_dw[0], conv_b_dw[0], conv_ln_g[0], conv_ln_b[0], w_pw2,
                          conv_b_pw2[0], norm_mix_post[1], norm_mlp_pre[1], norm_mlp_post[1],
                          w_up1, w_down1)
    return x2d.reshape(BATCH, SEQ, D_MODEL)
```

```python
import functools
import math

import jax
import jax.numpy as jnp
from jax import lax
from jax.experimental import pallas as pl
from jax.experimental.pallas import tpu as pltpu

D_MODEL = 1024
BATCH = 8
SEQ = 2048
HEAD_DIM = 64
N_HEADS = 16
DILATIONS = (1, 4, 16)
N_SIDE = 64
N_GROUPS = 3
GROUP_WIDTH = 3 * N_HEADS * HEAD_DIM
N_BUCKETS = 32
MAX_DISTANCE = 1024
CONV_WIDTH = 31
D_FF = 4 * D_MODEL
RMS_EPS = 1e-6
LN_EPS = 1e-5
NEG_INF = -1e30

F32 = jnp.float32
BF16 = jnp.bfloat16

LANES = 128
Q_TILE = 128
K_TILE = Q_TILE + 2 * N_SIDE
VMEM_LIMIT = 56 * 1024 * 1024


def _rms(x, g):
    return x * lax.rsqrt(jnp.mean(x * x, axis=-1, keepdims=True) + RMS_EPS) * g


def _params(semantics):
    return pltpu.CompilerParams(dimension_semantics=semantics, vmem_limit_bytes=VMEM_LIMIT)


NORM_ROWS = 256


def _prenorm_kernel(x_ref, g_ref, *refs):
    out_ref, slab_ref, regroup_ref = refs
    assert DILATIONS == (1, 4, 16)

    def chunk(i, carry):
        rows = pl.ds(pl.multiple_of(i * NORM_ROWS, NORM_ROWS), NORM_ROWS)
        hn = _rms(x_ref[0, rows, :], g_ref[...])
        for k in range(D_MODEL // LANES):
            slab_ref[k, rows, :] = hn[:, k * LANES:(k + 1) * LANES]
        out_ref[0, 0, rows, :] = hn.astype(BF16)
        return carry

    lax.fori_loop(0, SEQ // NORM_ROWS, chunk, 0)
    r1 = DILATIONS[1]
    L1, L2 = SEQ // r1, SEQ // DILATIONS[2]
    for k in range(D_MODEL // LANES):
        lanes = slice(k * LANES, (k + 1) * LANES)
        for c in range(r1):
            v = slab_ref[k, pl.ds(c, L1, stride=r1), :]
            out_ref[1, 0, c * L1:(c + 1) * L1, lanes] = v.astype(BF16)
            regroup_ref[c * L1:(c + 1) * L1, :] = v
        for c in range(DILATIONS[2]):
            start = (c % r1) * L1 + c // r1
            out_ref[2, 0, c * L2:(c + 1) * L2, lanes] = (
                regroup_ref[pl.ds(start, L2, stride=r1), :].astype(BF16))


def _prenorm(x, g):
    return pl.pallas_call(
        _prenorm_kernel,
        grid=(BATCH,),
        in_specs=[pl.BlockSpec((1, SEQ, D_MODEL), lambda b: (b, 0, 0)),
                  pl.BlockSpec((1, D_MODEL), lambda b: (0, 0))],
        out_specs=pl.BlockSpec((N_GROUPS, 1, SEQ, D_MODEL), lambda b: (0, b, 0, 0)),
        out_shape=jax.ShapeDtypeStruct((N_GROUPS, BATCH, SEQ, D_MODEL), BF16),
        scratch_shapes=[pltpu.VMEM((D_MODEL // LANES, SEQ, LANES), F32),
                        pltpu.VMEM((SEQ, LANES), F32)],
        compiler_params=_params(("arbitrary",)),
        name="prenorm",
    )(x, g.reshape(1, D_MODEL))


QKV_TN = 1536
QKV_TM = 512
Q_WIDTH = N_HEADS * HEAD_DIM
assert QKV_TN >= Q_WIDTH
LOG2E = math.log2(math.e)
Q_SCALE = HEAD_DIM ** -0.5 * LOG2E


def _qkv_kernel(h_ref, w_ref, *refs, n_cast):
    f32_refs, o_ref, bf16_refs, wb_ref = (refs[:n_cast], refs[n_cast], refs[n_cast + 1:-1],
                                          refs[-1])
    for src_ref, dst_ref in zip(f32_refs, bf16_refs):
        dst_ref[...] = src_ref[...].astype(BF16)

    @pl.when(pl.program_id(1) == 0)
    def _():
        wb_ref[...] = w_ref[...].astype(BF16)

    col = lax.broadcasted_iota(jnp.int32, (1, QKV_TN), 1)
    first_tile = pl.program_id(0) % (GROUP_WIDTH // QKV_TN) == 0
    scale = jnp.where(first_tile & (col < Q_WIDTH), Q_SCALE, 1.0)
    for m in range(SEQ // QKV_TM):
        rows = slice(m * QKV_TM, (m + 1) * QKV_TM)
        res = jnp.dot(h_ref[0, rows, :], wb_ref[...], preferred_element_type=F32)
        o_ref[0, rows, :] = (res * scale).astype(BF16)


QKV_CAST_STEPS = 32


def _qkv_proj(h, w_qkv, to_bf16):
    nj = GROUP_WIDTH // QKV_TN
    assert N_GROUPS * nj * BATCH >= QKV_CAST_STEPS
    cast_step = lambda i, b: jnp.minimum(i * BATCH + b, QKV_CAST_STEPS - 1)
    cast_in, cast_out, cast_shape = [], [], []
    for w, layer in to_bf16:
        _, rows, cols = w.shape
        blk = (rows // QKV_CAST_STEPS, cols)
        cast_in.append(pl.BlockSpec((None,) + blk,
                                    lambda i, b, layer=layer: (layer, cast_step(i, b), 0)))
        cast_out.append(pl.BlockSpec(blk, lambda i, b: (cast_step(i, b), 0)))
        cast_shape.append(jax.ShapeDtypeStruct((rows, cols), BF16))
    out = pl.pallas_call(
        functools.partial(_qkv_kernel, n_cast=len(to_bf16)),
        grid=(N_GROUPS * nj, BATCH),
        in_specs=[pl.BlockSpec((None, 1, SEQ, D_MODEL), lambda i, b: (i // nj, b, 0, 0)),
                  pl.BlockSpec((D_MODEL, QKV_TN), lambda i, b: (0, i))] + cast_in,
        out_specs=[pl.BlockSpec((None, 1, SEQ, QKV_TN), lambda i, b: (i // nj, b, 0, i % nj))]
        + cast_out,
        out_shape=[jax.ShapeDtypeStruct((N_GROUPS, BATCH, SEQ, GROUP_WIDTH), BF16)] + cast_shape,
        scratch_shapes=[pltpu.VMEM((D_MODEL, QKV_TN), BF16)],
        compiler_params=_params(("arbitrary", "arbitrary")),
        name="qkv_proj",
    )(h, w_qkv, *[w for w, _ in to_bf16])
    return out[0], out[1:]


HEADS_PER_STEP = 4
PAIRS_PER_STEP = HEADS_PER_STEP // 2
STEP_LANES = HEADS_PER_STEP * HEAD_DIM
N_Q_TILES = SEQ // Q_TILE
N_KINDS = 3
PADDED_PITCH = 24
GROUP_ROWS = tuple(SEQ // 16 * PADDED_PITCH if r % 16 == 0 else SEQ for r in DILATIONS)
GROUP_BASE = tuple(sum(GROUP_ROWS[:g]) for g in range(N_GROUPS))


def _bias_row(rel_ref, g, head):
    r = DILATIONS[g]
    nb = N_BUCKETS // 2
    max_exact = nb // 2
    delta = lax.broadcasted_iota(jnp.int32, (8, K_TILE), 1) - N_SIDE
    rel = delta * r
    n = jnp.abs(rel)
    nf = jnp.maximum(n, 1).astype(F32)
    large = max_exact + (jnp.log(nf / max_exact) / math.log(MAX_DISTANCE / max_exact)
                         * (nb - max_exact)).astype(jnp.int32)
    large = jnp.minimum(large, nb - 1)
    bucket = jnp.where(rel > 0, nb, 0) + jnp.where(n < max_exact, n, large)
    u = jnp.zeros((8, K_TILE), F32)
    for b in range(N_BUCKETS):
        u = jnp.where(bucket == b, rel_ref[b, g * N_HEADS + head], u)
    return jnp.where(jnp.abs(delta) <= N_SIDE, u * LOG2E, NEG_INF)


def _build_bias_tiles(rel_ref, bias_ref, head0):
    qi = lax.broadcasted_iota(jnp.int32, (Q_TILE, K_TILE), 0)
    kj = lax.broadcasted_iota(jnp.int32, (Q_TILE, K_TILE), 1)
    for g in range(N_GROUPS):
        for h in range(HEADS_PER_STEP):
            u = _bias_row(rel_ref, g, head0 + h)
            ub = jnp.broadcast_to(u[0:1, :], (Q_TILE, K_TILE))
            for kind in range(N_KINDS):
                shift = (N_SIDE * kind - N_SIDE) % K_TILE
                t = pltpu.roll(ub, shift, 1, stride=1, stride_axis=0)
                in_band = jnp.abs(kj - qi - N_SIDE * kind) <= N_SIDE
                bias_ref[g, h, kind] = jnp.where(in_band, t, NEG_INF)


def _token_rows(g, c, l0):
    r = DILATIONS[g]
    if r == 1:
        return pl.ds(GROUP_BASE[g] + l0, Q_TILE)
    pitch = PADDED_PITCH if r % 16 == 0 else r
    return pl.ds(GROUP_BASE[g] + l0 * pitch + c, Q_TILE, stride=pitch)


def _load_tokens(ref, pair, g, t):
    r = DILATIONS[g]
    if r % 16:
        return ref[pair, pl.ds(pl.multiple_of(GROUP_BASE[g] + t * Q_TILE, Q_TILE), Q_TILE), :]
    n = Q_TILE // r
    base = pl.multiple_of(GROUP_BASE[g] + t * n * PADDED_PITCH, 8)
    return jnp.concatenate([ref[pair, pl.ds(base + i * PADDED_PITCH, r), :] for i in range(n)],
                           axis=0)


def _split_heads(q):
    first = lax.broadcasted_iota(jnp.int32, (1, LANES), 1) < HEAD_DIM
    zero = jnp.zeros_like(q)
    return jnp.concatenate([jnp.where(first, q, zero), jnp.where(first, zero, q)], axis=0)


def _softmax_pv(s, v):
    first = lax.broadcasted_iota(jnp.int32, (1, LANES), 1) < HEAD_DIM
    m = jnp.max(s, axis=-1, keepdims=True)
    p = jnp.exp2(s - m).astype(BF16)
    v1 = jnp.concatenate([v, jnp.ones((v.shape[0], LANES), BF16)], axis=1)
    o2 = jnp.dot(p, v1, preferred_element_type=F32)
    acc = jnp.where(first, o2[:Q_TILE, :LANES], o2[Q_TILE:, :LANES])
    l = jnp.where(first, o2[:Q_TILE, LANES:], o2[Q_TILE:, LANES:])
    m_t = jnp.where(first, m[:Q_TILE], m[Q_TILE:])
    return acc, m_t, l


_NT = (((1,), (1,)), ((), ()))


def _attn_kernel(rel_ref, q0_ref, k0_ref, v0_ref, q1_ref, k1_ref, v1_ref, q2_ref, k2_ref, v2_ref,
                 o_ref, bias_ref, acc_ref, m_ref, l_ref):
    hp = pl.program_id(0)

    @pl.when(pl.program_id(1) == 0)
    def _():
        _build_bias_tiles(rel_ref, bias_ref, hp * HEADS_PER_STEP)

    qkv = ((q0_ref, k0_ref, v0_ref), (q1_ref, k1_ref, v1_ref), (q2_ref, k2_ref, v2_ref))
    for pair in range(PAIRS_PER_STEP):
        lanes = slice(pair * LANES, (pair + 1) * LANES)
        for g in range(N_GROUPS):
            r = DILATIONS[g]
            tiles_per_seq = SEQ // r // Q_TILE
            q_ref, k_ref, v_ref = qkv[g]

            def put(t, res, g=g, tiles_per_seq=tiles_per_seq, pair=pair):
                rows = _token_rows(g, t // tiles_per_seq, (t % tiles_per_seq) * Q_TILE)
                for ref, val in zip((acc_ref, m_ref, l_ref), res):
                    ref[pair, rows, :] = val

            if tiles_per_seq > 1:
                for t in range(N_Q_TILES):
                    tl = t % tiles_per_seq
                    kind = 0 if tl == 0 else (2 if tl == tiles_per_seq - 1 else 1)
                    q0 = t * Q_TILE
                    k0 = q0 - N_SIDE * kind
                    q2 = _split_heads(q_ref[0, q0:q0 + Q_TILE, lanes])
                    s = lax.dot_general(q2, k_ref[0, k0:k0 + K_TILE, lanes], _NT,
                                        preferred_element_type=F32)
                    s = s + jnp.concatenate([bias_ref[g, 2 * pair, kind],
                                             bias_ref[g, 2 * pair + 1, kind]], axis=0)
                    put(t, _softmax_pv(s, v_ref[0, k0:k0 + K_TILE, lanes]))
            else:
                bias = jnp.concatenate([bias_ref[g, 2 * pair, 0, :, :Q_TILE],
                                        bias_ref[g, 2 * pair + 1, 0, :, :Q_TILE]], axis=0)
                for t in range(0, N_Q_TILES, 2):
                    q0 = t * Q_TILE
                    q = q_ref[0, q0:q0 + 2 * Q_TILE, lanes]
                    q2 = jnp.concatenate([_split_heads(q[:Q_TILE]), _split_heads(q[Q_TILE:])], axis=0)
                    s = lax.dot_general(q2, k_ref[0, q0:q0 + 2 * Q_TILE, lanes], _NT,
                                        preferred_element_type=F32)
                    for i in range(2):
                        si = s[2 * i * Q_TILE:2 * (i + 1) * Q_TILE, i * Q_TILE:(i + 1) * Q_TILE]
                        v = v_ref[0, q0 + i * Q_TILE:q0 + (i + 1) * Q_TILE, lanes]
                        put(t + i, _softmax_pv(si + bias, v))

    for pair in range(PAIRS_PER_STEP):
        lanes = slice(pair * LANES, (pair + 1) * LANES)

        def merge_body(t, carry, lanes=lanes, pair=pair):
            m = [_load_tokens(m_ref, pair, g, t) for g in range(N_GROUPS)]
            top = jnp.maximum(jnp.maximum(m[0], m[1]), m[2])
            w = [jnp.exp2(x - top) for x in m]
            num = sum(w[g] * _load_tokens(acc_ref, pair, g, t) for g in range(N_GROUPS))
            den = sum(w[g] * _load_tokens(l_ref, pair, g, t) for g in range(N_GROUPS))
            rows = pl.ds(pl.multiple_of(t * Q_TILE, Q_TILE), Q_TILE)
            o_ref[0, rows, lanes] = (num / den).astype(BF16)
            return carry

        lax.fori_loop(0, N_Q_TILES, merge_body, 0, unroll=4)


def _attention(qkv, rel_bias):
    n_hp = N_HEADS // HEADS_PER_STEP
    blk = (1, SEQ, STEP_LANES)
    grp_blk = (None,) + blk
    specs = [pl.BlockSpec(memory_space=pltpu.SMEM)]
    args = [rel_bias]
    for g in range(N_GROUPS):
        for part in range(3):
            specs.append(pl.BlockSpec(grp_blk,
                                      lambda hp, b, g=g, part=part: (g, b, 0, part * n_hp + hp)))
            args.append(qkv)
    return pl.pallas_call(
        _attn_kernel,
        grid=(n_hp, BATCH),
        in_specs=specs,
        out_specs=pl.BlockSpec(blk, lambda hp, b: (b, 0, hp)),
        out_shape=jax.ShapeDtypeStruct((BATCH, SEQ, D_MODEL), BF16),
        scratch_shapes=[pltpu.VMEM((N_GROUPS, HEADS_PER_STEP, N_KINDS, Q_TILE, K_TILE), F32),
                        pltpu.VMEM((PAIRS_PER_STEP, sum(GROUP_ROWS), LANES), F32),
                        pltpu.VMEM((PAIRS_PER_STEP, sum(GROUP_ROWS), LANES), F32),
                        pltpu.VMEM((PAIRS_PER_STEP, sum(GROUP_ROWS), LANES), F32)],
        compiler_params=_params(("arbitrary", "arbitrary")),
        name="dilated_attention",
    )(*args)


MLP_TM = 512
MLP_FC = 1024
MLP_STAGES = 2 + 2 * (D_FF // MLP_FC)


_DONE = object()


def _interleave(order, **stage_generators):
    for key in order:
        next(stage_generators[key], _DONE)
    for gen in stage_generators.values():
        assert next(gen, _DONE) is _DONE, "order does not cover every stage"


def _mlp_stages(read_x, gpre_ref, gpost_ref, wup_ref, wdn_ref, o_ref):
    h = _rms(read_x(), gpre_ref[...]).astype(BF16)
    acc = jnp.zeros(h.shape, F32)
    yield
    for c in range(D_FF // MLP_FC):
        cols = slice(c * MLP_FC, (c + 1) * MLP_FC)
        u = jnp.dot(h, wup_ref[:, cols], preferred_element_type=F32)
        u = jnp.square(jnp.maximum(u, 0.0)).astype(BF16)
        yield
        acc = acc + jnp.dot(u, wdn_ref[cols, :], preferred_element_type=F32)
        yield
    o_ref[...] = read_x() + _rms(acc, gpost_ref[...])


def _proj_mlp_kernel(a_ref, wo_ref, gmix_ref, x_ref, gpre_ref, gpost_ref, wup_ref, wdn_ref,
                     *refs, n_cast):
    f32_refs, o_ref, bf16_refs = refs[:n_cast], refs[n_cast], refs[n_cast + 1:]
    for src_ref, dst_ref in zip(f32_refs, bf16_refs):
        dst_ref[...] = src_ref[...].astype(BF16)

    def half(rows):
        m = jnp.dot(a_ref[rows, :], wo_ref[...], preferred_element_type=F32)
        x = x_ref[rows, :] + _rms(m, gmix_ref[...])
        yield
        yield from _mlp_stages(lambda: x, gpre_ref, gpost_ref, wup_ref, wdn_ref,
                               o_ref.at[rows, :])

    n = MLP_TM // 2
    _interleave("aa" + "ab" * (MLP_STAGES - 1) + "bb", a=half(slice(0, n)),
                b=half(slice(n, 2 * n)))


def _proj_mlp(x2d, a2d, w_o, g_mix, g_pre, g_post, w_up, w_down, to_bf16):
    n = x2d.shape[0]
    steps = n // MLP_TM
    resident = pl.Buffered(1)
    tile = pl.BlockSpec((MLP_TM, D_MODEL), lambda i: (i, 0))
    gain = pl.BlockSpec((1, D_MODEL), lambda i: (0, 0))
    cast_in, cast_out, cast_shape = [], [], []
    for w, layer in to_bf16:
        _, rows, cols = w.shape
        cast_in.append(pl.BlockSpec((None, rows // steps, cols), lambda i, layer=layer: (layer, i, 0)))
        cast_out.append(pl.BlockSpec((rows // steps, cols), lambda i: (i, 0)))
        cast_shape.append(jax.ShapeDtypeStruct((rows, cols), BF16))
    out = pl.pallas_call(
        functools.partial(_proj_mlp_kernel, n_cast=len(to_bf16)),
        grid=(steps,),
        in_specs=[tile,
                  pl.BlockSpec((D_MODEL, D_MODEL), lambda i: (0, 0), pipeline_mode=resident),
                  gain, tile, gain, gain,
                  pl.BlockSpec((D_MODEL, D_FF), lambda i: (0, 0), pipeline_mode=resident),
                  pl.BlockSpec((D_FF, D_MODEL), lambda i: (0, 0), pipeline_mode=resident)]
        + cast_in,
        out_specs=[tile] + cast_out,
        out_shape=[jax.ShapeDtypeStruct((n, D_MODEL), F32)] + cast_shape,
        compiler_params=_params(("arbitrary",)),
        name="attn_proj_mlp",
    )(a2d, w_o, g_mix.reshape(1, D_MODEL), x2d, g_pre.reshape(1, D_MODEL),
      g_post.reshape(1, D_MODEL), w_up, w_down, *[w for w, _ in to_bf16])
    return out[0], out[1:]


CONV_TS = 512
HALO = 16
CONV_ROWS = CONV_TS + 2 * HALO
CONV_RB = 128
CONV_NC = 256


N_TILES = BATCH * SEQ // CONV_TS
TILES_PER_SEQ = SEQ // CONV_TS
assert CONV_TS == MLP_TM


def _conv_stages(xm_ref, xt_ref, xb_ref, first, last, gpre_ref, w1_ref, b1_ref, wdw_ref, bdw_ref,
                 lng_ref, lnb_ref, w2_ref, b2_ref, gpost_ref, o_ref, h_ref, u_ref, c_ref):
    gpre = gpre_ref[...]
    h_ref[0:HALO, :] = _rms(xt_ref[0], gpre).astype(BF16)
    h_ref[HALO:HALO + CONV_TS, :] = _rms(xm_ref[0], gpre).astype(BF16)
    h_ref[HALO + CONV_TS:, :] = _rms(xb_ref[0], gpre).astype(BF16)
    yield

    h = h_ref[...]
    half = CONV_RB // 2
    for nc in range(D_MODEL // CONV_NC):
        cols = slice(nc * CONV_NC, (nc + 1) * CONV_NC)
        gcols = slice(D_MODEL + nc * CONV_NC, D_MODEL + (nc + 1) * CONV_NC)
        a = jnp.dot(h, w1_ref[:, cols], preferred_element_type=F32) + b1_ref[:, cols]
        gate = jnp.dot(h, w1_ref[:, gcols], preferred_element_type=F32) + b1_ref[:, gcols]
        u = a * jax.nn.sigmoid(gate)
        for k in range(CONV_NC // LANES):
            lc = nc * CONV_NC // LANES + k
            lanes = slice(lc * LANES, (lc + 1) * LANES)
            uk = u[:, k * LANES:(k + 1) * LANES]
            u_ref[lc, 0:HALO, :] = jnp.where(first, 0.0, uk[0:HALO])
            u_ref[lc, HALO:HALO + CONV_TS, :] = uk[HALO:HALO + CONV_TS]
            u_ref[lc, HALO + CONV_TS:, :] = jnp.where(last, 0.0, uk[HALO + CONV_TS:])
        yield
        for k in range(CONV_NC // LANES):
            lc = nc * CONV_NC // LANES + k
            lanes = slice(lc * LANES, (lc + 1) * LANES)
            for r0 in range(0, CONV_TS, CONV_RB):
                for phase in range(2):
                    acc = bdw_ref[:, lanes]
                    for t in range(CONV_WIDTH):
                        start = r0 + phase + t + HALO - CONV_WIDTH // 2
                        acc = acc + (u_ref[lc, pl.ds(start, half, stride=2), :]
                                     * wdw_ref[t:t + 1, lanes])
                    c_ref[lc, pl.ds(r0 + phase, half, stride=2), :] = acc
            yield

    v = jnp.concatenate([c_ref[lc] for lc in range(D_MODEL // LANES)], axis=1)
    mu = jnp.mean(v, axis=-1, keepdims=True)
    var = jnp.mean(jnp.square(v - mu), axis=-1, keepdims=True)
    y = (v - mu) * lax.rsqrt(var + LN_EPS) * lng_ref[...] + lnb_ref[...]
    y = (y * jax.nn.sigmoid(y)).astype(BF16)
    yield
    z = jnp.dot(y, w2_ref[...], preferred_element_type=F32) + b2_ref[...]
    o_ref[...] = xm_ref[0] + _rms(z, gpost_ref[...])


def _conv_mlp_kernel(xm_ref, xt_ref, xb_ref, gpre_ref, w1_ref, b1_ref, wdw_ref, bdw_ref, lng_ref,
                     lnb_ref, w2_ref, b2_ref, gpost_ref, g2pre_ref, g2post_ref, wup_ref, wdn_ref,
                     o_ref, h_ref, u_ref, c_ref, x_ref, xnew_ref):
    s = pl.program_id(0)
    j = s % TILES_PER_SEQ

    def conv():
        return _conv_stages(xm_ref, xt_ref, xb_ref, j == 0, j == TILES_PER_SEQ - 1, gpre_ref,
                            w1_ref, b1_ref, wdw_ref, bdw_ref, lng_ref, lnb_ref, w2_ref, b2_ref,
                            gpost_ref, xnew_ref, h_ref, u_ref, c_ref)

    def mlp():
        return _mlp_stages(lambda: x_ref[...], g2pre_ref, g2post_ref, wup_ref, wdn_ref, o_ref)

    order = "ccccc" + "mm" + "cmccm" * 2 + "cmcm" * 2

    @pl.when(s == 0)
    def _():
        _interleave("c" * order.count("c"), c=conv())
        x_ref[...] = xnew_ref[...]

    @pl.when((s > 0) & (s < N_TILES))
    def _():
        _interleave(order, c=conv(), m=mlp())
        x_ref[...] = xnew_ref[...]

    @pl.when(s == N_TILES)
    def _():
        _interleave("m" * order.count("m"), m=mlp())


def _conv_mlp_layer(x, g_pre, w1, b1, wdw, bdw, lng, lnb, w2, b2, g_post, g2_pre, g2_post,
                    w_up, w_down):
    row = lambda v: v.reshape(1, -1)
    resident = pl.Buffered(1)
    const = lambda shape, **kw: pl.BlockSpec(shape, lambda s: (0,) * len(shape), **kw)
    halo_blocks = CONV_TS // HALO

    def tile(s):
        t = jnp.minimum(s, N_TILES - 1)
        return t // TILES_PER_SEQ, t % TILES_PER_SEQ

    def main_map(s):
        b, j = tile(s)
        return b, j, 0

    def top_map(s):
        b, j = tile(s)
        return b, jnp.maximum(j * halo_blocks - 1, 0), 0

    def bot_map(s):
        b, j = tile(s)
        return b, jnp.minimum((j + 1) * halo_blocks, SEQ // HALO - 1), 0

    gain = const((1, D_MODEL))
    return pl.pallas_call(
        _conv_mlp_kernel,
        grid=(N_TILES + 1,),
        in_specs=[pl.BlockSpec((1, CONV_TS, D_MODEL), main_map),
                  pl.BlockSpec((1, HALO, D_MODEL), top_map),
                  pl.BlockSpec((1, HALO, D_MODEL), bot_map),
                  gain,
                  const((D_MODEL, 2 * D_MODEL), pipeline_mode=resident),
                  const((1, 2 * D_MODEL)),
                  const((CONV_WIDTH, D_MODEL)),
                  gain, gain, gain,
                  const((D_MODEL, D_MODEL), pipeline_mode=resident),
                  gain, gain, gain, gain,
                  const((D_MODEL, D_FF), pipeline_mode=resident),
                  const((D_FF, D_MODEL), pipeline_mode=resident)],
        out_specs=pl.BlockSpec((CONV_TS, D_MODEL), lambda s: (jnp.maximum(s - 1, 0), 0)),
        out_shape=jax.ShapeDtypeStruct((BATCH * SEQ, D_MODEL), F32),
        scratch_shapes=[pltpu.VMEM((CONV_ROWS, D_MODEL), BF16),
                        pltpu.VMEM((D_MODEL // LANES, CONV_ROWS, LANES), F32),
                        pltpu.VMEM((D_MODEL // LANES, CONV_TS, LANES), F32),
                        pltpu.VMEM((CONV_TS, D_MODEL), F32),
                        pltpu.VMEM((CONV_TS, D_MODEL), F32)],
        compiler_params=_params(("arbitrary",)),
        name="conformer_conv_mlp",
    )(x, x, x, row(g_pre), w1, row(b1), wdw, row(bdw), row(lng), row(lnb), w2, row(b2),
      row(g_post), row(g2_pre), row(g2_post), w_up, w_down)


def kernel(x, rel_bias, norm_mix_pre, norm_mix_post, norm_mlp_pre, norm_mlp_post, attn_w_qkv,
           attn_w_o, conv_w_pw1, conv_b_pw1, conv_w_dw, conv_b_dw, conv_ln_g, conv_ln_b,
           conv_w_pw2, conv_b_pw2, mlp_w_up, mlp_w_down):
    n_tok = BATCH * SEQ
    x2d = x.reshape(n_tok, D_MODEL)

    h = _prenorm(x, norm_mix_pre[0])
    w_qkv = attn_w_qkv.reshape(D_MODEL, N_GROUPS * GROUP_WIDTH)
    qkv, (w_o, w_up0, w_down0) = _qkv_proj(
        h, w_qkv, to_bf16=[(attn_w_o, 0), (mlp_w_up, 0), (mlp_w_down, 0)])
    a = _attention(qkv, rel_bias)
    x2d, (w_pw1, w_pw2, w_up1, w_down1) = _proj_mlp(
        x2d, a.reshape(n_tok, D_MODEL), w_o, norm_mix_post[0], norm_mlp_pre[0],
        norm_mlp_post[0], w_up0, w_down0,
        to_bf16=[(conv_w_pw1, 0), (conv_w_pw2, 0), (mlp_w_up, 1), (mlp_w_down, 1)])

    x2d = _conv_mlp_layer(x2d.reshape(BATCH, SEQ, D_MODEL), norm_mix_pre[1], w_pw1, conv_b_pw1[0],
                          conv_w_dw[0], conv_b_dw[0], conv_ln_g[0], conv_ln_b[0], w_pw2,
                          conv_b_pw2[0], norm_mix_post[1], norm_mlp_pre[1], norm_mlp_post[1],
                          w_up1, w_down1)
    return x2d.reshape(BATCH, SEQ, D_MODEL)
```
